```python
import math
import jax, jax.numpy as jnp
from jax import lax
import numpy as np

D_MODEL = 2048
BATCH = 16
SEQ = 2048
DEPTH = 1

PLE_DIM = 256
MIX_WIDTH = D_MODEL
LRU_WIDTH = MIX_WIDTH // 2
LRU_BLOCKS = 16
LRU_BLOCK_DIM = LRU_WIDTH // LRU_BLOCKS
CONV_WIDTH = 4
LRU_C = 8.0
HEAD_DIM = 64
N_HEADS = (MIX_WIDTH - LRU_WIDTH) // HEAD_DIM
N_KV = 4
HEADS_PER_KV = N_HEADS // N_KV
KV_W = N_KV * HEAD_DIM
CMP_LEN = 32
CMP_STRIDE = 16
CMP_HIDDEN = 256
SEL_BLOCK = 64
N_SELECT = 16
N_LOCAL_FORCED = 2
WINDOW = 512
Q_BLOCK = 128
SEL_Q_CHUNK = 32
N_BUCKETS = 32
MAX_DISTANCE = 128
D_FF = 4 * D_MODEL
NORM_EPS = 1e-6
NEG_INF = -1e30
FORCE_SCORE = 1e4

IN_SIZES = (LRU_WIDTH, LRU_WIDTH, N_HEADS * HEAD_DIM, KV_W, KV_W, KV_W, KV_W, KV_W, KV_W, 3 * N_HEADS)
IN_DIM = sum(IN_SIZES)
SPLIT_POINTS = tuple(int(v) for v in np.cumsum(IN_SIZES)[:-1])

kernel_name = "hymba_rglru_nsa_hybrid_layer"


def rms_norm(x, g):
    xf = x.astype(jnp.float32)
    y = xf * lax.rsqrt(jnp.mean(xf * xf, axis=-1, keepdims=True) + NORM_EPS)
    return (y * g.astype(jnp.float32)).astype(x.dtype)


def rel_bucket(dist):
    n = jnp.maximum(dist, 0)
    max_exact = N_BUCKETS // 2
    nf = jnp.maximum(n, 1).astype(jnp.float32)
    large = max_exact + (jnp.log(nf / max_exact) / math.log(MAX_DISTANCE / max_exact)
                         * (N_BUCKETS - max_exact)).astype(jnp.int32)
    large = jnp.minimum(large, N_BUCKETS - 1)
    return jnp.where(n < max_exact, n, large)


def rglru_mixer(u, gate_in, conv_w, conv_b, wa, ba, wx, bx, lam):
    B, S = u.shape[0], u.shape[1]
    up = jnp.pad(u, ((0, 0), (CONV_WIDTH - 1, 0), (0, 0)))
    xc = conv_b + sum(up[:, k:k + S] * conv_w[k] for k in range(CONV_WIDTH))
    xb = xc.reshape(B, S, LRU_BLOCKS, LRU_BLOCK_DIM)
    r = jax.nn.sigmoid(jnp.einsum('bsni,nij->bsnj', xb, wa) + ba).reshape(B, S, LRU_WIDTH)
    ig = jax.nn.sigmoid(jnp.einsum('bsni,nij->bsnj', xb, wx) + bx).reshape(B, S, LRU_WIDTH)
    log_a = -LRU_C * r.astype(jnp.float32) * jax.nn.softplus(-lam.astype(jnp.float32))
    a = jnp.exp(log_a)
    b = jnp.sqrt(-jnp.expm1(2.0 * log_a)) * (ig * xc).astype(jnp.float32)

    def combine(left, right):
        a1, b1 = left
        a2, b2 = right
        return a1 * a2, a2 * b1 + b2

    _, h = lax.associative_scan(combine, (a, b), axis=1)
    return h.astype(u.dtype) * jax.nn.gelu(gate_in)


def nsa_mixer(q, k_c, v_c, k_s, v_s, k_w, v_w, gates,
              pe_k, w1_k, w2_k, pe_v, w1_v, w2_v, rel_bias):
    B, S = q.shape[0], q.shape[1]
    G, R, dk = N_KV, HEADS_PER_KV, HEAD_DIM
    q = q.reshape(B, S, G, R, dk)
    k_c, v_c, k_s, v_s, k_w, v_w = [t.reshape(B, S, G, dk) for t in (k_c, v_c, k_s, v_s, k_w, v_w)]
    scale = dk ** -0.5
    pos = jnp.arange(S, dtype=jnp.int32)
    bias_tab = rel_bias.astype(jnp.float32).reshape(N_BUCKETS, G, R)

    n_cmp = (S - CMP_LEN) // CMP_STRIDE + 1
    cmp_idx = np.arange(n_cmp)[:, None] * CMP_STRIDE + np.arange(CMP_LEN)[None, :]

    def compress(t, pe, w1, w2):
        blk = t[:, cmp_idx] + pe[:, None, :]
        blk = blk.transpose(0, 1, 3, 2, 4).reshape(B, n_cmp, G, CMP_LEN * dk)
        return jax.nn.gelu(blk @ w1) @ w2

    kc = compress(k_c, pe_k, w1_k, w2_k)
    vc = compress(v_c, pe_v, w1_v, w2_v)
    cmp_end = jnp.asarray(np.arange(n_cmp) * CMP_STRIDE + CMP_LEN - 1, dtype=jnp.int32)
    dist_c = pos[:, None] - cmp_end[None, :]
    valid_c = dist_c >= 0
    bias_c = bias_tab[rel_bucket(dist_c)].transpose(2, 3, 0, 1)
    s_c = jnp.einsum('bsgrd,bngd->bgrsn', q, kc).astype(jnp.float32) * scale + bias_c
    s_c = jnp.where(valid_c, s_c, NEG_INF)
    any_c = (pos >= CMP_LEN - 1)[:, None]
    p_c = jax.nn.softmax(s_c, axis=-1) * any_c
    o_c = jnp.einsum('bgrsn,bngd->bsgrd', p_c.astype(vc.dtype), vc)

    n_sel = S // SEL_BLOCK
    ratio_sel = SEL_BLOCK // CMP_STRIDE
    ratio_cmp = CMP_LEN // CMP_STRIDE
    jj_np = np.arange(n_sel)[:, None, None]
    ci = ratio_sel * jj_np + np.arange(ratio_sel)[None, :, None] - np.arange(ratio_cmp)[None, None, :]
    jb = np.broadcast_to(jj_np, ci.shape)
    ok = (ci >= 0) & (ci < n_cmp)
    M = np.zeros((n_cmp, n_sel), np.float32)
    np.add.at(M, (ci[ok], jb[ok]), 1.0)
    imp = jnp.einsum('bgrsn,nj->bgsj', p_c, jnp.asarray(M))
    jj = jnp.arange(n_sel, dtype=jnp.int32)
    dblk = (pos // SEL_BLOCK)[:, None] - jj[None, :]
    forced = (jj[None, :] == 0) | ((dblk >= 0) & (dblk < N_LOCAL_FORCED))
    causal_blk = dblk >= 0
    imp = jnp.where(forced, FORCE_SCORE, jnp.where(causal_blk, imp, -FORCE_SCORE))
    n_top = min(N_SELECT, n_sel)
    _, sel_idx = lax.top_k(imp, n_top)

    kb = k_s.reshape(B, n_sel, SEL_BLOCK, G, dk).transpose(0, 3, 1, 2, 4)
    vb = v_s.reshape(B, n_sel, SEL_BLOCK, G, dk).transpose(0, 3, 1, 2, 4)
    n_ch = S // SEL_Q_CHUNK
    q_ch = q.reshape(B, n_ch, SEL_Q_CHUNK, G, R, dk).transpose(1, 0, 2, 3, 4, 5)
    idx_ch = sel_idx.reshape(B, G, n_ch, SEL_Q_CHUNK, n_top).transpose(2, 0, 1, 3, 4)
    pos_ch = pos.reshape(n_ch, SEL_Q_CHUNK)
    b_ar = jnp.arange(B)[:, None, None]
    g_ar = jnp.arange(G)[None, :, None]
    g_ar4 = jnp.arange(G)[None, :, None, None]
    n_keys = n_top * SEL_BLOCK
    in_blk = jnp.arange(SEL_BLOCK, dtype=jnp.int32)

    def sel_chunk(args):
        qc, ic, tc = args
        flat = ic.reshape(B, G, SEL_Q_CHUNK * n_top)
        ks = kb[b_ar, g_ar, flat].reshape(B, G, SEL_Q_CHUNK, n_keys, dk)
        vs = vb[b_ar, g_ar, flat].reshape(B, G, SEL_Q_CHUNK, n_keys, dk)
        kpos = (ic[..., None] * SEL_BLOCK + in_blk).reshape(B, G, SEL_Q_CHUNK, n_keys)
        dist = tc[None, None, :, None] - kpos
        bias = bias_tab[rel_bucket(dist), g_ar4].transpose(0, 1, 4, 2, 3)
        s = jnp.einsum('bqgrd,bgqkd->bgrqk', qc, ks).astype(jnp.float32) * scale + bias
        s = jnp.where((dist >= 0)[:, :, None], s, NEG_INF)
        pr = jax.nn.softmax(s, axis=-1)
        return jnp.einsum('bgrqk,bgqkd->bqgrd', pr.astype(vs.dtype), vs)

    o_s = lax.map(sel_chunk, (q_ch, idx_ch, pos_ch))
    o_s = o_s.transpose(1, 0, 2, 3, 4, 5).reshape(B, S, G, R, dk)

    n_qb = S // Q_BLOCK
    band = Q_BLOCK + WINDOW
    kw_pad = jnp.pad(k_w, ((0, 0), (WINDOW, 0), (0, 0), (0, 0)))
    vw_pad = jnp.pad(v_w, ((0, 0), (WINDOW, 0), (0, 0), (0, 0)))
    kj = jnp.arange(band, dtype=jnp.int32)[None, :]
    dist_w = WINDOW + jnp.arange(Q_BLOCK, dtype=jnp.int32)[:, None] - kj
    band_ok = (dist_w >= 0) & (dist_w < WINDOW)
    bias_w = bias_tab[rel_bucket(dist_w)].transpose(2, 3, 0, 1)
    q_blk = q.reshape(B, n_qb, Q_BLOCK, G, R, dk).transpose(1, 0, 2, 3, 4, 5)
    starts = jnp.arange(n_qb, dtype=jnp.int32) * Q_BLOCK

    def win_block(args):
        qb, st = args
        kw = lax.dynamic_slice_in_dim(kw_pad, st, band, axis=1)
        vw = lax.dynamic_slice_in_dim(vw_pad, st, band, axis=1)
        ok_w = band_ok & (st - WINDOW + kj >= 0)
        s = jnp.einsum('bqgrd,bkgd->bgrqk', qb, kw).astype(jnp.float32) * scale + bias_w
        s = jnp.where(ok_w, s, NEG_INF)
        pr = jax.nn.softmax(s, axis=-1)
        return jnp.einsum('bgrqk,bkgd->bqgrd', pr.astype(vw.dtype), vw)

    o_w = lax.map(win_block, (q_blk, starts))
    o_w = o_w.transpose(1, 0, 2, 3, 4, 5).reshape(B, S, G, R, dk)

    gt = jax.nn.sigmoid(gates).reshape(B, S, G, R, 3)
    o = gt[..., 0:1] * o_c + gt[..., 1:2] * o_s + gt[..., 2:3] * o_w
    return o.reshape(B, S, N_HEADS * dk)


def setup_inputs(seed: int = 0) -> dict:
    key = jax.random.key(seed)
    ks = jax.random.split(key, 32)
    f32 = jnp.float32

    def nrm(k, shape, scale):
        return jax.random.normal(k, shape, f32) * scale

    def gain(k, shape):
        return 1.0 + 0.05 * jax.random.normal(k, shape, f32)

    a0 = jax.random.uniform(ks[10], (DEPTH, LRU_WIDTH), f32, 0.9, 0.999)
    return {
        "x": nrm(ks[0], (BATCH, SEQ, D_MODEL), 1.0),
        "p": nrm(ks[1], (DEPTH, BATCH, SEQ, PLE_DIM), 1.0),
        "norm_mix_pre": gain(ks[2], (DEPTH, D_MODEL)),
        "norm_mix_post": gain(ks[3], (DEPTH, D_MODEL)),
        "norm_mlp_pre": gain(ks[4], (DEPTH, D_MODEL)),
        "norm_mlp_post": gain(ks[5], (DEPTH, D_MODEL)),
        "w_in": nrm(ks[6], (DEPTH, D_MODEL, IN_DIM), D_MODEL ** -0.5),
        "conv_w": nrm(ks[7], (DEPTH, CONV_WIDTH, LRU_WIDTH), CONV_WIDTH ** -0.5),
        "conv_b": nrm(ks[8], (DEPTH, LRU_WIDTH), 0.02),
        "lru_wa": nrm(ks[9], (DEPTH, LRU_BLOCKS, LRU_BLOCK_DIM, LRU_BLOCK_DIM), LRU_BLOCK_DIM ** -0.5),
        "lru_ba": nrm(ks[11], (DEPTH, LRU_BLOCKS, LRU_BLOCK_DIM), 0.02),
        "lru_wx": nrm(ks[12], (DEPTH, LRU_BLOCKS, LRU_BLOCK_DIM, LRU_BLOCK_DIM), LRU_BLOCK_DIM ** -0.5),
        "lru_bx": nrm(ks[13], (DEPTH, LRU_BLOCKS, LRU_BLOCK_DIM), 0.02),
        "lru_lambda": jnp.log(a0 / (1.0 - a0)),
        "cmp_pe_k": nrm(ks[14], (DEPTH, CMP_LEN, HEAD_DIM), 0.1),
        "cmp_w1_k": nrm(ks[15], (DEPTH, CMP_LEN * HEAD_DIM, CMP_HIDDEN), (CMP_LEN * HEAD_DIM) ** -0.5),
        "cmp_w2_k": nrm(ks[16], (DEPTH, CMP_HIDDEN, HEAD_DIM), CMP_HIDDEN ** -0.5),
        "cmp_pe_v": nrm(ks[17], (DEPTH, CMP_LEN, HEAD_DIM), 0.1),
        "cmp_w1_v": nrm(ks[18], (DEPTH, CMP_LEN * HEAD_DIM, CMP_HIDDEN), (CMP_LEN * HEAD_DIM) ** -0.5),
        "cmp_w2_v": nrm(ks[19], (DEPTH, CMP_HIDDEN, HEAD_DIM), CMP_HIDDEN ** -0.5),
        "rel_bias": nrm(ks[20], (N_BUCKETS, N_HEADS), 0.1),
        "gnorm_lru": gain(ks[21], (DEPTH, LRU_WIDTH)),
        "gnorm_nsa": gain(ks[22], (DEPTH, N_HEADS * HEAD_DIM)),
        "w_out": nrm(ks[23], (DEPTH, MIX_WIDTH, D_MODEL), MIX_WIDTH ** -0.5),
        "mlp_w1": nrm(ks[24], (DEPTH, D_MODEL, D_FF), D_MODEL ** -0.5),
        "mlp_w2": nrm(ks[25], (DEPTH, D_FF, D_MODEL), D_FF ** -0.5),
        "ple_gate": nrm(ks[26], (DEPTH, D_MODEL, D_MODEL), D_MODEL ** -0.5),
        "ple_proj": nrm(ks[27], (DEPTH, PLE_DIM, D_MODEL), PLE_DIM ** -0.5),
    }


def reference(x, p, norm_mix_pre, norm_mix_post, norm_mlp_pre, norm_mlp_post, w_in,
              conv_w, conv_b, lru_wa, lru_ba, lru_wx, lru_bx, lru_lambda,
              cmp_pe_k, cmp_w1_k, cmp_w2_k, cmp_pe_v, cmp_w1_v, cmp_w2_v, rel_bias,
              gnorm_lru, gnorm_nsa, w_out, mlp_w1, mlp_w2, ple_gate, ple_proj):
    h = x
    for i in range(DEPTH):
        a = rms_norm(h, norm_mix_pre[i])
        z = a @ w_in[i]
        u, g_lru, q, kc, vc, ksel, vsel, kw, vw, gts = jnp.split(z, SPLIT_POINTS, axis=-1)
        y_lru = rglru_mixer(u, g_lru, conv_w[i], conv_b[i], lru_wa[i], lru_ba[i],
                            lru_wx[i], lru_bx[i], lru_lambda[i])
        y_nsa = nsa_mixer(q, kc, vc, ksel, vsel, kw, vw, gts,
                          cmp_pe_k[i], cmp_w1_k[i], cmp_w2_k[i],
                          cmp_pe_v[i], cmp_w1_v[i], cmp_w2_v[i], rel_bias)
        y = jnp.concatenate([rms_norm(y_lru, gnorm_lru[i]), rms_norm(y_nsa, gnorm_nsa[i])], axis=-1)
        h = h + rms_norm(y @ w_out[i], norm_mix_post[i])
        f = jnp.square(jax.nn.relu(rms_norm(h, norm_mlp_pre[i]) @ mlp_w1[i])) @ mlp_w2[i]
        h = h + rms_norm(f, norm_mlp_post[i])
        h = h + jax.nn.sigmoid(h @ ple_gate[i]) * (p[i] @ ple_proj[i])
    return h
```

```python
import functools
import math

import numpy as np
import jax
import jax.numpy as jnp
from jax import lax
from jax.experimental import pallas as pl
from jax.experimental.pallas import tpu as pltpu

D_MODEL = 2048
PLE_DIM = 256
LRU_WIDTH = 1024
LRU_BLOCKS = 16
LRU_BLOCK_DIM = 64
CONV_WIDTH = 4
LRU_C = 8.0
HEAD_DIM = 64
N_HEADS = 16
N_KV = 4
HEADS_PER_KV = 4
KV_W = N_KV * HEAD_DIM
CMP_LEN = 32
CMP_STRIDE = 16
CMP_HIDDEN = 256
SEL_BLOCK = 64
N_SELECT = 16
N_LOCAL_FORCED = 2
WINDOW = 512
N_BUCKETS = 32
MAX_DISTANCE = 128
D_FF = 4 * D_MODEL
NORM_EPS = 1e-6
NEG_INF = -1e30
FORCE_SCORE = 1e4
IN_DIM = 4656
IN_PAD = 4736
GATE_LO = 4608

Q_TILE = 128
SEL_K_TILE = 256
WIN_SPAN = Q_TILE + WINDOW
N_SEL_BIAS = 4

VMEM_LIMIT = 56 * 1024 * 1024

f32 = jnp.float32
bf16 = jnp.bfloat16


def _rms(x, g):
    return x * lax.rsqrt(jnp.mean(x * x, axis=-1, keepdims=True) + NORM_EPS) * g


def _dot(a, b):
    return jnp.dot(a, b, preferred_element_type=f32)


def _dot_nt(a, b):
    return lax.dot_general(a, b, (((1,), (1,)), ((), ())), preferred_element_type=f32)


def _dot_tn(a, b):
    return lax.dot_general(a, b, (((0,), (0,)), ((), ())), preferred_element_type=f32)


def _const_spec(shape):
    nd = len(shape)
    return pl.BlockSpec(shape, lambda *_: (0,) * nd)


def _params(sem):
    return pltpu.CompilerParams(dimension_semantics=sem, vmem_limit_bytes=VMEM_LIMIT)


def _in_proj_body(x_ref, g_ref, w_ref, u_ref, gl_ref, q_ref, kc_ref, vc_ref,
                  ks_ref, vs_ref, kw_ref, vw_ref, gt_ref):
    a = _rms(x_ref[...], g_ref[...]).astype(bf16)

    def proj(lo, hi):
        return _dot(a, w_ref[:, lo:hi])

    u_ref[...] = proj(0, 1024)
    gl_ref[...] = proj(1024, 2048)
    q_ref[...] = (proj(2048, 3072) * (HEAD_DIM ** -0.5)).astype(bf16)
    kc_ref[...] = proj(3072, 3328)
    vc_ref[...] = proj(3328, 3584)
    ks_ref[...] = proj(3584, 3840).astype(bf16)
    vs_ref[...] = proj(3840, 4096).astype(bf16)
    kw_ref[...] = proj(4096, 4352).astype(bf16)
    vw_ref[...] = proj(4352, 4608).astype(bf16)
    gt_ref[...] = proj(GATE_LO, IN_PAD)


def _in_proj(x2, g, w, tm=512):
    m = x2.shape[0]
    widths = [(1024, f32), (1024, f32), (1024, bf16), (256, f32), (256, f32),
              (256, bf16), (256, bf16), (256, bf16), (256, bf16), (128, f32)]
    return pl.pallas_call(
        _in_proj_body,
        grid=(m // tm,),
        in_specs=[pl.BlockSpec((tm, D_MODEL), lambda i: (i, 0)),
                  _const_spec((1, D_MODEL)),
                  pl.BlockSpec((D_MODEL, IN_PAD), lambda i: (0, 0), pipeline_mode=pl.Buffered(1))],
        out_specs=[pl.BlockSpec((tm, n), lambda i: (i, 0)) for n, _ in widths],
        out_shape=[jax.ShapeDtypeStruct((m, n), dt) for n, dt in widths],
        compiler_params=_params(("parallel",)),
        name="in_proj",
    )(x2, g, w)


def _rglru_body(u_ref, gl_ref, cw_ref, cb_ref, wa_ref, ba_ref, wx_ref, bx_ref, lam_ref, gn_ref,
                o_ref, ubuf, a_sc, b_sc, h_sc, hc_sc):
    t = u_ref.shape[0]
    pad = 8

    @pl.when(pl.program_id(1) == 0)
    def _():
        ubuf[0:pad, :] = jnp.zeros((pad, LRU_WIDTH), f32)
        hc_sc[...] = jnp.zeros_like(hc_sc)

    u = u_ref[...]
    ubuf[pad:pad + t, :] = u
    cw = cw_ref[...]
    xc = (cb_ref[...] + ubuf[pad - 3:pad - 3 + t, :] * cw[0:1] + ubuf[pad - 2:pad - 2 + t, :] * cw[1:2]
          + ubuf[pad - 1:pad - 1 + t, :] * cw[2:3] + u * cw[3:4])
    ubuf[0:pad, :] = u[t - pad:t, :]

    xb = xc.astype(bf16)
    sp = jax.nn.softplus(-lam_ref[...])
    row8 = lax.broadcasted_iota(jnp.int32, (t, 256), 0) & 7
    for c in range(LRU_WIDTH // 256):
        sl = slice(c * 256, (c + 1) * 256)
        xcb = xb[:, sl]
        r = jax.nn.sigmoid(_dot(xcb, wa_ref[c]) + ba_ref[:, sl])
        ig = jax.nn.sigmoid(_dot(xcb, wx_ref[c]) + bx_ref[:, sl])
        log_a = (-LRU_C) * r * sp[:, sl]
        a = jnp.exp(log_a)
        b = jnp.sqrt(1.0 - jnp.exp(2.0 * log_a)) * (ig * xc[:, sl])
        for d in (1, 2, 4):
            keep = row8 >= d
            a_prev = pltpu.roll(a, d, 0)
            b_prev = pltpu.roll(b, d, 0)
            b = jnp.where(keep, a * b_prev + b, b)
            a = jnp.where(keep, a * a_prev, a)
        a_sc[:, sl] = a
        b_sc[:, sl] = b

    def group(gi, h):
        off = pl.multiple_of(gi * 8, 8)
        hg = b_sc[pl.ds(off, 8), :] + a_sc[pl.ds(off, 8), :] * h
        h_sc[pl.ds(off, 8), :] = hg
        return hg[7:8, :]

    h_last = lax.fori_loop(0, t // 8, group, hc_sc[0:1, :])
    hc_sc[0:1, :] = h_last
    y = h_sc[...] * jax.nn.gelu(gl_ref[...])
    o_ref[...] = _rms(y, gn_ref[...]).astype(o_ref.dtype)


def _rglru(u, gl, cw, cb, wa, ba, wx, bx, lam, gn, batch, seq, t=256):
    ns = seq // t
    row = pl.BlockSpec((t, LRU_WIDTH), lambda b, s: (b * ns + s, 0))
    vec = _const_spec((1, LRU_WIDTH))
    wspec = _const_spec((LRU_WIDTH // 256, 256, 256))
    return pl.pallas_call(
        _rglru_body,
        grid=(batch, ns),
        in_specs=[row, row, _const_spec((CONV_WIDTH, LRU_WIDTH)), vec, wspec, vec, wspec, vec, vec, vec],
        out_specs=row,
        out_shape=jax.ShapeDtypeStruct((batch * seq, LRU_WIDTH), bf16),
        scratch_shapes=[pltpu.VMEM((t + 8, LRU_WIDTH), f32), pltpu.VMEM((t, LRU_WIDTH), f32),
                        pltpu.VMEM((t, LRU_WIDTH), f32), pltpu.VMEM((t, LRU_WIDTH), f32),
                        pltpu.VMEM((8, LRU_WIDTH), f32)],
        compiler_params=_params(("parallel", "arbitrary")),
        name="rglru",
    )(u, gl, cw, cb, wa, ba, wx, bx, lam, gn)


def _compress_body(ck_ref, cv_ref, pek_ref, w1k_ref, w2k_ref, pev_ref, w1v_ref, w2v_ref, ko_ref, vo_ref):
    nrow = ck_ref.shape[2]
    half = CMP_STRIDE * HEAD_DIM
    rows = lax.broadcasted_iota(jnp.int32, (nrow, HEAD_DIM), 0)

    def one(c_ref, pe_ref, w1_ref, w2_ref, o_ref):
        c = c_ref[0, 0].astype(bf16)
        pe = jnp.broadcast_to(pe_ref[...], (8, 2 * half)).astype(bf16)
        c0 = _dot(pe, w1_ref[...])[0:1, :]
        lo = _dot(c, w1_ref[0:half, :])
        hi = _dot(c, w1_ref[half:2 * half, :])
        hid = jax.nn.gelu(lo + pltpu.roll(hi, nrow - 1, 0) + c0)
        out = _dot(hid.astype(bf16), w2_ref[...])
        o_ref[0, 0] = jnp.where(rows < nrow - 1, out, 0.0).astype(o_ref.dtype)

    one(ck_ref, pek_ref, w1k_ref, w2k_ref, ko_ref)
    one(cv_ref, pev_ref, w1v_ref, w2v_ref, vo_ref)


def _compress(ck, cv, pek, w1k, w2k, pev, w1v, w2v):
    batch, g, nrow, width = ck.shape
    cspec = pl.BlockSpec((1, 1, nrow, width), lambda b, gi: (b, gi, 0, 0))
    ospec = pl.BlockSpec((1, 1, nrow, HEAD_DIM), lambda b, gi: (b, gi, 0, 0))
    wts = [_const_spec((1, 2 * width)), _const_spec((2 * width, CMP_HIDDEN)),
           _const_spec((CMP_HIDDEN, HEAD_DIM))]
    return pl.pallas_call(
        _compress_body,
        grid=(batch, g),
        in_specs=[cspec, cspec] + wts + wts,
        out_specs=[ospec, ospec],
        out_shape=[jax.ShapeDtypeStruct((batch, g, nrow, HEAD_DIM), bf16)] * 2,
        compiler_params=_params(("parallel", "parallel")),
        name="compress",
    )(ck, cv, pek, w1k, w2k, pev, w1v, w2v)


def _bucket_thresholds():
    n = np.arange(0, 4096)
    max_exact = N_BUCKETS // 2
    nf = np.maximum(n, 1).astype(np.float32)
    large = max_exact + (np.log(nf / np.float32(max_exact)) / np.float32(math.log(MAX_DISTANCE / max_exact))
                         * np.float32(N_BUCKETS - max_exact)).astype(np.int32)
    large = np.minimum(large, N_BUCKETS - 1)
    bucket = np.where(n < max_exact, n, large)
    assert np.all(np.diff(bucket) >= 0) and bucket[0] == 0 and bucket[-1] == N_BUCKETS - 1
    return [int(np.argmax(bucket >= k)) for k in range(N_BUCKETS)]


_BUCKET_THR = _bucket_thresholds()
assert _BUCKET_THR[-1] <= (N_SEL_BIAS - 1) * Q_TILE - SEL_K_TILE + 1


def _bias_of_dist(dist, ok, tab_ref, head):
    val = jnp.full(dist.shape, tab_ref[0, head], f32)
    for k in range(1, N_BUCKETS):
        val = jnp.where(dist >= _BUCKET_THR[k], tab_ref[k, head], val)
    return jnp.where(ok, val, NEG_INF)


def _bias_body(tab_ref, bc_ref, bs_ref, bw_ref):
    g = pl.program_id(0)
    qt = pl.program_id(1)
    ncmp = bc_ref.shape[3]
    for r in range(HEADS_PER_KV):
        head = g * HEADS_PER_KV + r
        rows = slice(r * Q_TILE, (r + 1) * Q_TILE)
        i = lax.broadcasted_iota(jnp.int32, (Q_TILE, ncmp), 0)
        n = lax.broadcasted_iota(jnp.int32, (Q_TILE, ncmp), 1)
        dist = qt * Q_TILE + i - (n * CMP_STRIDE + CMP_LEN - 1)
        bc_ref[0, 0, rows, :] = _bias_of_dist(dist, (dist >= 0) & (n < ncmp - 1), tab_ref, head)

    @pl.when(qt == 0)
    def _():
        for r in range(HEADS_PER_KV):
            head = g * HEADS_PER_KV + r
            rows = slice(r * Q_TILE, (r + 1) * Q_TILE)
            i = lax.broadcasted_iota(jnp.int32, (Q_TILE, SEL_K_TILE), 0)
            j = lax.broadcasted_iota(jnp.int32, (Q_TILE, SEL_K_TILE), 1)
            for d in range(N_SEL_BIAS):
                dist = d * Q_TILE + i - j
                bs_ref[0, d, rows, :] = _bias_of_dist(dist, dist >= 0, tab_ref, head)
            i = lax.broadcasted_iota(jnp.int32, (Q_TILE, WIN_SPAN), 0)
            j = lax.broadcasted_iota(jnp.int32, (Q_TILE, WIN_SPAN), 1)
            dist = WINDOW + i - j
            bw_ref[0, rows, :] = _bias_of_dist(dist, (dist >= 0) & (dist < WINDOW), tab_ref, head)


def _bias_tiles(rel_bias, seq):
    nq = seq // Q_TILE
    ncmp = seq // CMP_STRIDE
    rows = HEADS_PER_KV * Q_TILE
    return pl.pallas_call(
        _bias_body,
        grid=(N_KV, nq),
        in_specs=[pl.BlockSpec(memory_space=pltpu.SMEM)],
        out_specs=[pl.BlockSpec((1, 1, rows, ncmp), lambda g, q: (g, q, 0, 0)),
                   pl.BlockSpec((1, N_SEL_BIAS, rows, SEL_K_TILE), lambda g, q: (g, 0, 0, 0)),
                   pl.BlockSpec((1, rows, WIN_SPAN), lambda g, q: (g, 0, 0))],
        out_shape=[jax.ShapeDtypeStruct((N_KV, nq, rows, ncmp), f32),
                   jax.ShapeDtypeStruct((N_KV, N_SEL_BIAS, rows, SEL_K_TILE), f32),
                   jax.ShapeDtypeStruct((N_KV, rows, WIN_SPAN), f32)],
        compiler_params=_params(("parallel", "arbitrary")),
        name="bias_tiles",
    )(rel_bias)


def _importance_matrix_t(seq):
    n_cmp = (seq - CMP_LEN) // CMP_STRIDE + 1
    n_sel = seq // SEL_BLOCK
    ratio_sel = SEL_BLOCK // CMP_STRIDE
    ratio_cmp = CMP_LEN // CMP_STRIDE
    jj = np.arange(n_sel)[:, None, None]
    ci = ratio_sel * jj + np.arange(ratio_sel)[None, :, None] - np.arange(ratio_cmp)[None, None, :]
    jb = np.broadcast_to(jj, ci.shape)
    ok = (ci >= 0) & (ci < n_cmp)
    m = np.zeros((n_sel, seq // CMP_STRIDE), np.float32)
    np.add.at(m, (jb[ok], ci[ok]), 1.0)
    return m


def _softmax_rows(s):
    m = jnp.max(s, axis=-1, keepdims=True)
    p = jnp.exp(s - m)
    return p, jnp.sum(p, axis=-1, keepdims=True)


def _nsa_body(q_ref, kc_ref, vc_ref, ks_ref, vs_ref, kw_ref, vw_ref, gt_ref, bc_ref, bs_ref, bw_ref, mt_ref,
              o_ref):
    qt = pl.program_id(2)
    rows = HEADS_PER_KV * Q_TILE
    n_sel = mt_ref.shape[0]
    q = q_ref[0, 0].reshape(rows, HEAD_DIM)

    s = _dot_nt(q, kc_ref[0, 0]) + bc_ref[0, 0]
    p, l = _softmax_rows(s)
    pos = qt * Q_TILE + (lax.broadcasted_iota(jnp.int32, (rows, 1), 0) & (Q_TILE - 1))
    p = p * jnp.where(pos >= CMP_LEN - 1, 1.0 / l, 0.0)
    o_c = _dot(p.astype(bf16), vc_ref[0, 0])

    psum = p[0:Q_TILE] + p[Q_TILE:2 * Q_TILE] + p[2 * Q_TILE:3 * Q_TILE] + p[3 * Q_TILE:4 * Q_TILE]
    p_hi = psum.astype(bf16)
    p_lo = (psum - p_hi.astype(f32)).astype(bf16)
    mt = mt_ref[...]
    imp = _dot_nt(mt, p_hi) + _dot_nt(mt, p_lo)
    jj = lax.broadcasted_iota(jnp.int32, (n_sel, Q_TILE), 0)
    qpos = qt * Q_TILE + lax.broadcasted_iota(jnp.int32, (n_sel, Q_TILE), 1)
    dblk = qpos // SEL_BLOCK - jj
    forced = (jj == 0) | ((dblk >= 0) & (dblk < N_LOCAL_FORCED))
    imp = jnp.where(forced, FORCE_SCORE, jnp.where(dblk >= 0, imp, -FORCE_SCORE))
    rank = jnp.zeros((n_sel, Q_TILE), f32)
    for i in range(n_sel):
        row = imp[i:i + 1, :]
        ahead = (row > imp) | ((row == imp) & (jj > i))
        rank = rank + jnp.where(ahead, 1.0, 0.0)
    sel_t = jnp.where(rank < N_SELECT, 1.0, 0.0).astype(bf16)

    blk_of_key = lax.broadcasted_iota(jnp.int32, (n_sel, SEL_K_TILE), 1) // SEL_BLOCK
    blk_row = lax.broadcasted_iota(jnp.int32, (n_sel, SEL_K_TILE), 0)

    def sel_step(kt, carry):
        m_run, l_run, acc = carry
        off = pl.multiple_of(kt * SEL_K_TILE, SEL_K_TILE)
        k = ks_ref[0, 0, pl.ds(off, SEL_K_TILE), :]
        v = vs_ref[0, 0, pl.ds(off, SEL_K_TILE), :]
        tile = jnp.minimum(qt - kt * (SEL_K_TILE // Q_TILE), N_SEL_BIAS - 1)
        sc = _dot_nt(q, k) + bs_ref[0, tile]
        expand = jnp.where(blk_of_key + kt * (SEL_K_TILE // SEL_BLOCK) == blk_row, 1.0, 0.0).astype(bf16)
        chosen = _dot_tn(sel_t, expand) > 0.5
        sc = jnp.where(chosen[None], sc.reshape(HEADS_PER_KV, Q_TILE, SEL_K_TILE), NEG_INF)
        sc = sc.reshape(rows, SEL_K_TILE)
        m_new = jnp.maximum(m_run, jnp.max(sc, axis=-1, keepdims=True))
        alpha = jnp.exp(m_run - m_new)
        pe = jnp.exp(sc - m_new)
        l_new = alpha * l_run + jnp.sum(pe, axis=-1, keepdims=True)
        acc = alpha * acc + _dot(pe.astype(bf16), v)
        return m_new, l_new, acc

    n_kt = qt // (SEL_K_TILE // Q_TILE) + 1
    init = (jnp.full((rows, 1), NEG_INF, f32), jnp.zeros((rows, 1), f32), jnp.zeros((rows, HEAD_DIM), f32))
    _, l_s, acc_s = lax.fori_loop(0, n_kt, sel_step, init)
    o_s = acc_s / l_s

    off = pl.multiple_of(qt * Q_TILE, Q_TILE)
    kwin = kw_ref[0, 0, pl.ds(off, WIN_SPAN), :]
    vwin = vw_ref[0, 0, pl.ds(off, WIN_SPAN), :]
    sw = _dot_nt(q, kwin) + bw_ref[0]
    col = lax.broadcasted_iota(jnp.int32, (rows, WIN_SPAN), 1)
    sw = jnp.where(col >= WINDOW - qt * Q_TILE, sw, NEG_INF)
    pw, lw = _softmax_rows(sw)
    o_w = _dot(pw.astype(bf16), vwin) / lw

    gt = jax.nn.sigmoid(gt_ref[0, 0])
    outs = []
    for r in range(HEADS_PER_KV):
        sl = slice(r * Q_TILE, (r + 1) * Q_TILE)
        outs.append(gt[:, 3 * r:3 * r + 1] * o_c[sl] + gt[:, 3 * r + 1:3 * r + 2] * o_s[sl]
                    + gt[:, 3 * r + 2:3 * r + 3] * o_w[sl])
    o_ref[0] = jnp.concatenate(outs, axis=-1)


def _nsa(q, kc, vc, ks, vs, kw, vw, gt, bc, bs, bw, mt):
    batch, g, r, seq, _ = q.shape
    nq = seq // Q_TILE
    rows = r * Q_TILE
    ncmp = kc.shape[2]

    def per_bg(shape):
        return pl.BlockSpec((1, 1) + shape, lambda b, gi, qi: (b, gi) + (0,) * len(shape))

    return pl.pallas_call(
        _nsa_body,
        grid=(batch, g, nq),
        in_specs=[pl.BlockSpec((1, 1, r, Q_TILE, HEAD_DIM), lambda b, gi, qi: (b, gi, 0, qi, 0)),
                  per_bg((ncmp, HEAD_DIM)), per_bg((ncmp, HEAD_DIM)),
                  per_bg((seq, HEAD_DIM)), per_bg((seq, HEAD_DIM)),
                  per_bg((seq + WINDOW, HEAD_DIM)), per_bg((seq + WINDOW, HEAD_DIM)),
                  pl.BlockSpec((1, 1, Q_TILE, 3 * r), lambda b, gi, qi: (b, gi, qi, 0)),
                  pl.BlockSpec((1, 1, rows, ncmp), lambda b, gi, qi: (gi, qi, 0, 0)),
                  pl.BlockSpec((1, N_SEL_BIAS, rows, SEL_K_TILE), lambda b, gi, qi: (gi, 0, 0, 0)),
                  pl.BlockSpec((1, rows, WIN_SPAN), lambda b, gi, qi: (gi, 0, 0)),
                  _const_spec(mt.shape)],
        out_specs=pl.BlockSpec((1, Q_TILE, r * HEAD_DIM), lambda b, gi, qi: (b, qi, gi)),
        out_shape=jax.ShapeDtypeStruct((batch, seq, g * r * HEAD_DIM), f32),
        compiler_params=_params(("parallel", "parallel", "arbitrary")),
        name="nsa",
    )(q, kc, vc, ks, vs, kw, vw, gt, bc, bs, bw, mt)


def _out_proj_body(x_ref, yl_ref, yn_ref, gn_ref, w_ref, gp_ref, o_ref):
    yn = _rms(yn_ref[...], gn_ref[...]).astype(bf16)
    y = _dot(yl_ref[...], w_ref[0:LRU_WIDTH, :]) + _dot(yn, w_ref[LRU_WIDTH:, :])
    o_ref[...] = x_ref[...] + _rms(y, gp_ref[...])


def _out_proj(x2, yl, yn, gn, w, gp, tm=512):
    m = x2.shape[0]
    return pl.pallas_call(
        _out_proj_body,
        grid=(m // tm,),
        in_specs=[pl.BlockSpec((tm, D_MODEL), lambda i: (i, 0)),
                  pl.BlockSpec((tm, LRU_WIDTH), lambda i: (i, 0)),
                  pl.BlockSpec((tm, D_MODEL - LRU_WIDTH), lambda i: (i, 0)),
                  _const_spec((1, D_MODEL - LRU_WIDTH)),
                  _const_spec((D_MODEL, D_MODEL)),
                  _const_spec((1, D_MODEL))],
        out_specs=pl.BlockSpec((tm, D_MODEL), lambda i: (i, 0)),
        out_shape=jax.ShapeDtypeStruct((m, D_MODEL), f32),
        compiler_params=_params(("parallel",)),
        name="out_proj",
    )(x2, yl, yn, gn, w, gp)


def _mlp_body(h_ref, g1_ref, w1_ref, w2_ref, g2_ref, o_ref, a_sc, acc_sc):
    j = pl.program_id(1)

    @pl.when(j == 0)
    def _():
        a_sc[...] = _rms(h_ref[...], g1_ref[...]).astype(bf16)
        acc_sc[...] = jnp.zeros_like(acc_sc)

    hid = jnp.maximum(_dot(a_sc[...], w1_ref[...]), 0.0)
    acc_sc[...] += _dot((hid * hid).astype(bf16), w2_ref[...])

    @pl.when(j == pl.num_programs(1) - 1)
    def _():
        o_ref[...] = h_ref[...] + _rms(acc_sc[...], g2_ref[...])


def _mlp(h, g1, w1, w2, g2, tm=512, tf=512):
    m = h.shape[0]
    return pl.pallas_call(
        _mlp_body,
        grid=(m // tm, D_FF // tf),
        in_specs=[pl.BlockSpec((tm, D_MODEL), lambda i, j: (i, 0)),
                  _const_spec((1, D_MODEL)),
                  pl.BlockSpec((D_MODEL, tf), lambda i, j: (0, j)),
                  pl.BlockSpec((tf, D_MODEL), lambda i, j: (j, 0)),
                  _const_spec((1, D_MODEL))],
        out_specs=pl.BlockSpec((tm, D_MODEL), lambda i, j: (i, 0)),
        out_shape=jax.ShapeDtypeStruct((m, D_MODEL), f32),
        scratch_shapes=[pltpu.VMEM((tm, D_MODEL), bf16), pltpu.VMEM((tm, D_MODEL), f32)],
        compiler_params=_params(("parallel", "arbitrary")),
        name="mlp",
    )(h, g1, w1, w2, g2)


def _ple_body(h_ref, p_ref, wg_ref, wp_ref, o_ref):
    h = h_ref[...]
    gate = jax.nn.sigmoid(_dot(h.astype(bf16), wg_ref[...]))
    o_ref[...] = h + gate * _dot(p_ref[...].astype(bf16), wp_ref[...])


def _ple(h, p2, wg, wp, tm=512):
    m = h.shape[0]
    return pl.pallas_call(
        _ple_body,
        grid=(m // tm,),
        in_specs=[pl.BlockSpec((tm, D_MODEL), lambda i: (i, 0)),
                  pl.BlockSpec((tm, PLE_DIM), lambda i: (i, 0)),
                  _const_spec((D_MODEL, D_MODEL)),
                  _const_spec((PLE_DIM, D_MODEL))],
        out_specs=pl.BlockSpec((tm, D_MODEL), lambda i: (i, 0)),
        out_shape=jax.ShapeDtypeStruct((m, D_MODEL), f32),
        compiler_params=_params(("parallel",)),
        name="ple",
    )(h, p2, wg, wp)


def _block_diag_chunks(w):
    per = 256 // LRU_BLOCK_DIM
    w = w.reshape(LRU_BLOCKS // per, per, LRU_BLOCK_DIM, LRU_BLOCK_DIM)
    eye = jnp.eye(per, dtype=w.dtype)
    return jnp.einsum('cpij,pq->cpiqj', w, eye).reshape(LRU_BLOCKS // per, 256, 256)


def _layer(h, p_i, i, prm, bias):
    batch, seq, _ = h.shape
    m = batch * seq
    x2 = h.reshape(m, D_MODEL)
    row = lambda v: v.reshape(1, -1)

    w_in = jnp.pad(prm["w_in"][i], ((0, 0), (0, IN_PAD - IN_DIM))).astype(bf16)
    u, gl, q, kc, vc, ks, vs, kw, vw, gts = _in_proj(x2, row(prm["norm_mix_pre"][i]), w_in)

    y_lru = _rglru(u, gl, prm["conv_w"][i], row(prm["conv_b"][i]),
                   _block_diag_chunks(prm["lru_wa"][i]).astype(bf16), row(prm["lru_ba"][i]),
                   _block_diag_chunks(prm["lru_wx"][i]).astype(bf16), row(prm["lru_bx"][i]),
                   row(prm["lru_lambda"][i]), row(prm["gnorm_lru"][i]), batch, seq)

    def heads(t):
        return t.reshape(batch, seq, N_KV, HEAD_DIM).transpose(0, 2, 1, 3)

    def chunks(t):
        t = t.reshape(batch, seq // CMP_STRIDE, CMP_STRIDE, N_KV, HEAD_DIM).transpose(0, 3, 1, 2, 4)
        return t.reshape(batch, N_KV, seq // CMP_STRIDE, CMP_STRIDE * HEAD_DIM)

    kcc, vcc = _compress(chunks(kc), chunks(vc),
                         prm["cmp_pe_k"][i].reshape(1, -1), prm["cmp_w1_k"][i].astype(bf16),
                         prm["cmp_w2_k"][i].astype(bf16),
                         prm["cmp_pe_v"][i].reshape(1, -1), prm["cmp_w1_v"][i].astype(bf16),
                         prm["cmp_w2_v"][i].astype(bf16))

    qh = q.reshape(batch, seq, N_KV, HEADS_PER_KV, HEAD_DIM).transpose(0, 2, 3, 1, 4)
    front = ((0, 0), (0, 0), (WINDOW, 0), (0, 0))
    gth = gts[:, :3 * N_HEADS].reshape(batch, seq, N_KV, 3 * HEADS_PER_KV).transpose(0, 2, 1, 3)
    bc, bs, bw = bias
    y_nsa = _nsa(qh, kcc, vcc, heads(ks), heads(vs), jnp.pad(heads(kw), front), jnp.pad(heads(vw), front),
                 gth, bc, bs, bw, jnp.asarray(_importance_matrix_t(seq)).astype(bf16))

    h1 = _out_proj(x2, y_lru, y_nsa.reshape(m, -1), row(prm["gnorm_nsa"][i]),
                   prm["w_out"][i].astype(bf16), row(prm["norm_mix_post"][i]))
    h2 = _mlp(h1, row(prm["norm_mlp_pre"][i]), prm["mlp_w1"][i].astype(bf16),
              prm["mlp_w2"][i].astype(bf16), row(prm["norm_mlp_post"][i]))
    h3 = _ple(h2, p_i.reshape(m, PLE_DIM), prm["ple_gate"][i].astype(bf16), prm["ple_proj"][i].astype(bf16))
    return h3.reshape(batch, seq, D_MODEL)


def kernel(x, p, norm_mix_pre, norm_mix_post, norm_mlp_pre, norm_mlp_post, w_in, conv_w, conv_b, lru_wa, lru_ba, lru_wx, lru_bx, lru_lambda, cmp_pe_k, cmp_w1_k, cmp_w2_k, cmp_pe_v, cmp_w1_v, cmp_w2_v, rel_bias, gnorm_lru, gnorm_nsa, w_out, mlp_w1, mlp_w2, ple_gate, ple_proj):
    prm = dict(norm_mix_pre=norm_mix_pre, norm_mix_post=norm_mix_post, norm_mlp_pre=norm_mlp_pre,
               norm_mlp_post=norm_mlp_post, w_in=w_in, conv_w=conv_w, conv_b=conv_b, lru_wa=lru_wa,
               lru_ba=lru_ba, lru_wx=lru_wx, lru_bx=lru_bx, lru_lambda=lru_lambda, cmp_pe_k=cmp_pe_k,
               cmp_w1_k=cmp_w1_k, cmp_w2_k=cmp_w2_k, cmp_pe_v=cmp_pe_v, cmp_w1_v=cmp_w1_v,
               cmp_w2_v=cmp_w2_v, gnorm_lru=gnorm_lru, gnorm_nsa=gnorm_nsa, w_out=w_out, mlp_w1=mlp_w1,
               mlp_w2=mlp_w2, ple_gate=ple_gate, ple_proj=ple_proj)
    bias = _bias_tiles(rel_bias, x.shape[1])
    h = x
    for i in range(w_in.shape[0]):
        h = _layer(h, p[i], i, prm, bias)
    return h
```

```python
import functools
import math

import numpy as np
import jax
import jax.numpy as jnp
from jax import lax
from jax.experimental import pallas as pl
from jax.experimental.pallas import tpu as pltpu

D_MODEL = 2048
PLE_DIM = 256
LRU_WIDTH = 1024
LRU_BLOCKS = 16
LRU_BLOCK_DIM = 64
CONV_WIDTH = 4
LRU_C = 8.0
HEAD_DIM = 64
N_HEADS = 16
N_KV = 4
HEADS_PER_KV = 4
CMP_LEN = 32
CMP_STRIDE = 16
CMP_HIDDEN = 256
SEL_BLOCK = 64
N_SELECT = 16
N_LOCAL_FORCED = 2
WINDOW = 512
N_BUCKETS = 32
MAX_DISTANCE = 128
D_FF = 4 * D_MODEL
NORM_EPS = 1e-6
NEG_INF = -1e30
FORCE_SCORE = 1e4
IN_DIM = 4656
IN_PAD = 4736
GATE_LO = 4608

LANES = 128
MASK_LO = HEAD_DIM
Q_TILE = 128
K_TILE = 256
TILE_RATIO = K_TILE // Q_TILE
N_SEL_BIAS = 3
N_WIN_BIAS = (WINDOW + Q_TILE) // Q_TILE + 1
LOG2E = math.log2(math.e)

VMEM_LIMIT = 56 * 1024 * 1024

f32 = jnp.float32
bf16 = jnp.bfloat16


def _rms(x, g):
    return x * lax.rsqrt(jnp.mean(x * x, axis=-1, keepdims=True) + NORM_EPS) * g


def _dot(a, b):
    return jnp.dot(a, b, preferred_element_type=f32)


def _dot_nt(a, b):
    return lax.dot_general(a, b, (((1,), (1,)), ((), ())), preferred_element_type=f32)


def _dot_tn(a, b):
    return lax.dot_general(a, b, (((0,), (0,)), ((), ())), preferred_element_type=f32)


def _const_spec(shape):
    nd = len(shape)
    return pl.BlockSpec(shape, lambda *_: (0,) * nd)


def _params(sem):
    return pltpu.CompilerParams(dimension_semantics=sem, vmem_limit_bytes=VMEM_LIMIT)


def _spread_groups(z, fill):
    lane = lax.broadcasted_iota(jnp.int32, (z.shape[0], LANES), 1)
    parts = []
    for g in range(N_KV):
        pair = z[:, (g // 2) * LANES:(g // 2 + 1) * LANES]
        if g % 2:
            pair = pltpu.roll(pair, HEAD_DIM, 1)
        parts.append(jnp.where(lane < HEAD_DIM, pair, fill))
    return jnp.concatenate(parts, axis=1)


def _in_proj_body(seq, x_ref, g_ref, w_ref, u_ref, gl_ref, q_ref, kc_ref, vc_ref,
                  ks_ref, vs_ref, kw_ref, vw_ref, gt_ref):
    tm = x_ref.shape[0]
    a = _rms(x_ref[...], g_ref[...]).astype(bf16)

    def proj(lo, hi):
        return _dot(a, w_ref[:, lo:hi])

    u_ref[...] = proj(0, 1024)
    gl_ref[...] = proj(1024, 2048)
    q_ref[...] = (proj(2048, 3072) * (LOG2E * HEAD_DIM ** -0.5)).astype(bf16)
    kc_ref[...] = proj(3072, 3328)
    vc_ref[...] = proj(3328, 3584)

    lane = lax.broadcasted_iota(jnp.int32, (tm, LANES), 1)
    pos = (pl.program_id(0) * tm) % seq + lax.broadcasted_iota(jnp.int32, (tm, LANES), 0)
    block_onehot = jnp.where(lane - MASK_LO == pos // SEL_BLOCK, 1.0, 0.0)
    ones_lane = jnp.where(lane == MASK_LO, 1.0, 0.0)
    ks_ref[...] = _spread_groups(proj(3584, 3840), block_onehot).astype(bf16)
    vs_ref[...] = _spread_groups(proj(3840, 4096), ones_lane).astype(bf16)
    kw_ref[...] = _spread_groups(proj(4096, 4352), 0.0).astype(bf16)
    vw_ref[...] = _spread_groups(proj(4352, 4608), ones_lane).astype(bf16)
    gt = proj(GATE_LO, IN_PAD)
    per = 3 * HEADS_PER_KV
    gt_ref[...] = jnp.concatenate(
        [gt if g == 0 else pltpu.roll(gt, LANES - per * g, 1) for g in range(N_KV)], axis=1)


def _in_proj(x2, g, w, seq, tm=512):
    m = x2.shape[0]
    wide = N_KV * LANES
    widths = [(1024, f32), (1024, f32), (1024, bf16), (256, f32), (256, f32),
              (wide, bf16), (wide, bf16), (wide, bf16), (wide, bf16), (wide, f32)]
    body = lambda *refs: _in_proj_body(seq, *refs)
    return pl.pallas_call(
        body,
        grid=(m // tm,),
        in_specs=[pl.BlockSpec((tm, D_MODEL), lambda i: (i, 0)),
                  _const_spec((1, D_MODEL)),
                  pl.BlockSpec((D_MODEL, IN_PAD), lambda i: (0, 0), pipeline_mode=pl.Buffered(1))],
        out_specs=[pl.BlockSpec((tm, n), lambda i: (i, 0)) for n, _ in widths],
        out_shape=[jax.ShapeDtypeStruct((m, n), dt) for n, dt in widths],
        compiler_params=_params(("parallel",)),
        name="in_proj",
    )(x2, g, w)


def _rglru_body(u_ref, gl_ref, cw_ref, cb_ref, wa_ref, ba_ref, wx_ref, bx_ref, lam_ref, gn_ref,
                o_ref, ubuf, a_sc, b_sc, h_sc, hc_sc):
    t = u_ref.shape[0]
    pad = 8

    @pl.when(pl.program_id(1) == 0)
    def _():
        ubuf[0:pad, :] = jnp.zeros((pad, LRU_WIDTH), f32)
        hc_sc[...] = jnp.zeros_like(hc_sc)

    u = u_ref[...]
    ubuf[pad:pad + t, :] = u
    cw = cw_ref[...]
    xc = (cb_ref[...] + ubuf[pad - 3:pad - 3 + t, :] * cw[0:1] + ubuf[pad - 2:pad - 2 + t, :] * cw[1:2]
          + ubuf[pad - 1:pad - 1 + t, :] * cw[2:3] + u * cw[3:4])
    ubuf[0:pad, :] = u[t - pad:t, :]

    xb = xc.astype(bf16)
    sp = jax.nn.softplus(-lam_ref[...])
    row8 = lax.broadcasted_iota(jnp.int32, (t, 256), 0) & 7
    for c in range(LRU_WIDTH // 256):
        sl = slice(c * 256, (c + 1) * 256)
        xcb = xb[:, sl]
        r = jax.nn.sigmoid(_dot(xcb, wa_ref[c]) + ba_ref[:, sl])
        ig = jax.nn.sigmoid(_dot(xcb, wx_ref[c]) + bx_ref[:, sl])
        log_a = (-LRU_C) * r * sp[:, sl]
        a = jnp.exp(log_a)
        b = jnp.sqrt(1.0 - jnp.exp(2.0 * log_a)) * (ig * xc[:, sl])
        for d in (1, 2, 4):
            keep = row8 >= d
            a_prev = pltpu.roll(a, d, 0)
            b_prev = pltpu.roll(b, d, 0)
            b = jnp.where(keep, a * b_prev + b, b)
            a = jnp.where(keep, a * a_prev, a)
        a_sc[:, sl] = a
        b_sc[:, sl] = b

    def group(gi, h):
        off = pl.multiple_of(gi * 8, 8)
        hg = b_sc[pl.ds(off, 8), :] + a_sc[pl.ds(off, 8), :] * h
        h_sc[pl.ds(off, 8), :] = hg
        return hg[7:8, :]

    h_last = lax.fori_loop(0, t // 8, group, hc_sc[0:1, :])
    hc_sc[0:1, :] = h_last
    y = h_sc[...] * jax.nn.gelu(gl_ref[...])
    o_ref[...] = _rms(y, gn_ref[...]).astype(o_ref.dtype)


def _rglru(u, gl, cw, cb, wa, ba, wx, bx, lam, gn, batch, seq, t=256):
    ns = seq // t
    row = pl.BlockSpec((t, LRU_WIDTH), lambda b, s: (b * ns + s, 0))
    vec = _const_spec((1, LRU_WIDTH))
    wspec = _const_spec((LRU_WIDTH // 256, 256, 256))
    return pl.pallas_call(
        _rglru_body,
        grid=(batch, ns),
        in_specs=[row, row, _const_spec((CONV_WIDTH, LRU_WIDTH)), vec, wspec, vec, wspec, vec, vec, vec],
        out_specs=row,
        out_shape=jax.ShapeDtypeStruct((batch * seq, LRU_WIDTH), bf16),
        scratch_shapes=[pltpu.VMEM((t + 8, LRU_WIDTH), f32), pltpu.VMEM((t, LRU_WIDTH), f32),
                        pltpu.VMEM((t, LRU_WIDTH), f32), pltpu.VMEM((t, LRU_WIDTH), f32),
                        pltpu.VMEM((8, LRU_WIDTH), f32)],
        compiler_params=_params(("parallel", "arbitrary")),
        name="rglru",
    )(u, gl, cw, cb, wa, ba, wx, bx, lam, gn)


def _compress_body(kc_ref, vc_ref, pek_ref, w1k_ref, w2k_ref, pev_ref, w1v_ref, w2v_ref, ko_ref, vo_ref):
    nrow = ko_ref.shape[2]
    rows = lax.broadcasted_iota(jnp.int32, (nrow, LANES), 0)

    def one(t_ref, pe_ref, w1_ref, w2_ref, o_ref):
        pe = jnp.broadcast_to(pe_ref[...], (8, CMP_LEN * HEAD_DIM)).astype(bf16)
        c0 = _dot(pe, w1_ref[...])[0:1, :]
        toks = [t_ref[0, pl.ds(j, nrow, stride=CMP_STRIDE), :] for j in range(CMP_STRIDE)]
        for g in range(LANES // HEAD_DIM):
            lo = jnp.zeros((nrow, CMP_HIDDEN), f32)
            hi = jnp.zeros((nrow, CMP_HIDDEN), f32)
            for j in range(CMP_STRIDE):
                tj = toks[j][:, g * HEAD_DIM:(g + 1) * HEAD_DIM].astype(bf16)
                lo = lo + _dot(tj, w1_ref[j * HEAD_DIM:(j + 1) * HEAD_DIM, :])
                hi = hi + _dot(tj, w1_ref[(CMP_STRIDE + j) * HEAD_DIM:(CMP_STRIDE + j + 1) * HEAD_DIM, :])
            hid = jax.nn.gelu(lo + pltpu.roll(hi, nrow - 1, 0) + c0)
            out = _dot(hid.astype(bf16), w2_ref[...])
            o_ref[0, g] = jnp.where(rows < nrow - 1, out, 0.0).astype(o_ref.dtype)

    one(kc_ref, pek_ref, w1k_ref, w2k_ref, ko_ref)
    one(vc_ref, pev_ref, w1v_ref, w2v_ref, vo_ref)


def _compress(kc, vc, pek, w1k, w2k, pev, w1v, w2v):
    batch, seq, width = kc.shape
    nrow = seq // CMP_STRIDE
    pair = LANES // HEAD_DIM
    tspec = pl.BlockSpec((1, seq, LANES), lambda b, h: (b, 0, h))
    ospec = pl.BlockSpec((1, pair, nrow, LANES), lambda b, h: (b, h, 0, 0))
    wts = [_const_spec((1, CMP_LEN * HEAD_DIM)), _const_spec((CMP_LEN * HEAD_DIM, CMP_HIDDEN)),
           _const_spec((CMP_HIDDEN, LANES))]
    return pl.pallas_call(
        _compress_body,
        grid=(batch, width // LANES),
        in_specs=[tspec, tspec] + wts + wts,
        out_specs=[ospec, ospec],
        out_shape=[jax.ShapeDtypeStruct((batch, N_KV, nrow, LANES), bf16)] * 2,
        compiler_params=_params(("parallel", "parallel")),
        name="compress",
    )(kc, vc, pek, w1k, w2k, pev, w1v, w2v)


def _bucket_thresholds():
    n = np.arange(0, 4096)
    max_exact = N_BUCKETS // 2
    nf = np.maximum(n, 1).astype(np.float32)
    large = max_exact + (np.log(nf / np.float32(max_exact)) / np.float32(math.log(MAX_DISTANCE / max_exact))
                         * np.float32(N_BUCKETS - max_exact)).astype(np.int32)
    large = np.minimum(large, N_BUCKETS - 1)
    bucket = np.where(n < max_exact, n, large)
    assert np.all(np.diff(bucket) >= 0) and bucket[0] == 0 and bucket[-1] == N_BUCKETS - 1
    return [int(np.argmax(bucket >= k)) for k in range(N_BUCKETS)]


_BUCKET_THR = _bucket_thresholds()
assert _BUCKET_THR[-1] <= N_SEL_BIAS * Q_TILE - K_TILE + 1


def _bias_of_dist(dist, ok, tab_ref, head):
    last = tab_ref[N_BUCKETS - 1, head]
    val = jnp.full(dist.shape, (tab_ref[0, head] - last) * LOG2E, f32)
    for k in range(1, N_BUCKETS - 1):
        val = jnp.where(dist >= _BUCKET_THR[k], (tab_ref[k, head] - last) * LOG2E, val)
    val = jnp.where(dist >= _BUCKET_THR[N_BUCKETS - 1], 0.0, val)
    return jnp.where(ok, val, NEG_INF)


def _bias_body(tab_ref, bc_ref, bs_ref, bw_ref):
    g = pl.program_id(0)
    qt = pl.program_id(1)
    ncmp = bc_ref.shape[3]
    for r in range(HEADS_PER_KV):
        head = g * HEADS_PER_KV + r
        rows = slice(r * Q_TILE, (r + 1) * Q_TILE)
        i = lax.broadcasted_iota(jnp.int32, (Q_TILE, ncmp), 0)
        n = lax.broadcasted_iota(jnp.int32, (Q_TILE, ncmp), 1)
        dist = qt * Q_TILE + i - (n * CMP_STRIDE + CMP_LEN - 1)
        bc_ref[0, 0, rows, :] = _bias_of_dist(dist, (dist >= 0) & (n < ncmp - 1), tab_ref, head)

    @pl.when(qt == 0)
    def _():
        i = lax.broadcasted_iota(jnp.int32, (Q_TILE, K_TILE), 0)
        j = lax.broadcasted_iota(jnp.int32, (Q_TILE, K_TILE), 1)
        for r in range(HEADS_PER_KV):
            head = g * HEADS_PER_KV + r
            rows = slice(r * Q_TILE, (r + 1) * Q_TILE)
            for d in range(N_WIN_BIAS):
                dist = d * Q_TILE + i - j
                if d < N_SEL_BIAS:
                    bs_ref[0, d, rows, :] = _bias_of_dist(dist, dist >= 0, tab_ref, head)
                bw_ref[0, d, rows, :] = _bias_of_dist(dist, (dist >= 0) & (dist < WINDOW), tab_ref, head)


def _bias_tiles(rel_bias, seq):
    nq = seq // Q_TILE
    ncmp = seq // CMP_STRIDE
    rows = HEADS_PER_KV * Q_TILE
    return pl.pallas_call(
        _bias_body,
        grid=(N_KV, nq),
        in_specs=[pl.BlockSpec(memory_space=pltpu.SMEM)],
        out_specs=[pl.BlockSpec((1, 1, rows, ncmp), lambda g, q: (g, q, 0, 0)),
                   pl.BlockSpec((1, N_SEL_BIAS, rows, K_TILE), lambda g, q: (g, 0, 0, 0)),
                   pl.BlockSpec((1, N_WIN_BIAS, rows, K_TILE), lambda g, q: (g, 0, 0, 0))],
        out_shape=[jax.ShapeDtypeStruct((N_KV, nq, rows, ncmp), f32),
                   jax.ShapeDtypeStruct((N_KV, N_SEL_BIAS, rows, K_TILE), f32),
                   jax.ShapeDtypeStruct((N_KV, N_WIN_BIAS, rows, K_TILE), f32)],
        compiler_params=_params(("parallel", "arbitrary")),
        name="bias_tiles",
    )(rel_bias)


def _importance_matrix_t(seq):
    n_cmp = (seq - CMP_LEN) // CMP_STRIDE + 1
    n_sel = seq // SEL_BLOCK
    ratio_sel = SEL_BLOCK // CMP_STRIDE
    ratio_cmp = CMP_LEN // CMP_STRIDE
    jj = np.arange(n_sel)[:, None, None]
    ci = ratio_sel * jj + np.arange(ratio_sel)[None, :, None] - np.arange(ratio_cmp)[None, None, :]
    jb = np.broadcast_to(jj, ci.shape)
    ok = (ci >= 0) & (ci < n_cmp)
    m = np.zeros((n_sel, seq // CMP_STRIDE), np.float32)
    np.add.at(m, (jb[ok], ci[ok]), 1.0)
    return m


def _mask_placement(n_sel):
    p = np.zeros((n_sel, LANES), np.float32)
    p[np.arange(n_sel), MASK_LO + np.arange(n_sel)] = NEG_INF
    return p


def _attend(q, k_ref, v_ref, t, first_tile, bias_ref):
    last_tile = t // TILE_RATIO
    lo = first_tile * K_TILE
    hi = (last_tile + 1) * K_TILE
    s = _dot_nt(q, k_ref[0, lo:hi, :])
    pieces = []
    for kt in range(first_tile, last_tile + 1):
        piece = s[:, (kt - first_tile) * K_TILE:(kt - first_tile + 1) * K_TILE]
        idx = t - kt * TILE_RATIO
        if idx < bias_ref.shape[1]:
            piece = piece + bias_ref[0, idx]
        pieces.append(piece)
    m = functools.reduce(jnp.maximum, [jnp.max(pc, axis=-1, keepdims=True) for pc in pieces])
    p = jnp.concatenate([jnp.exp2(pc - m).astype(bf16) for pc in pieces], axis=-1)
    acc = _dot(p, v_ref[0, lo:hi, :])
    return acc * (1.0 / acc[:, MASK_LO:MASK_LO + 1])


def _nsa_body(q_ref, kc_ref, vc_ref, ks_ref, vs_ref, kw_ref, vw_ref, gt_ref, bc_ref, bs_ref, bw_ref,
              mt_ref, place_ref, o_ref):
    n_sel = mt_ref.shape[0]
    n_tiles = q_ref.shape[1] // Q_TILE
    rows = HEADS_PER_KV * Q_TILE
    lane = lax.broadcasted_iota(jnp.int32, (Q_TILE, LANES), 1)
    lane_rows = lax.broadcasted_iota(jnp.int32, (rows, LANES), 1)
    head_rows = [slice(r * Q_TILE, (r + 1) * Q_TILE) for r in range(HEADS_PER_KV)]
    kc = kc_ref[0, 0]
    vc = vc_ref[0, 0]
    mt = mt_ref[...]
    jj = lax.broadcasted_iota(jnp.int32, (n_sel, Q_TILE), 0)
    qcol = lax.broadcasted_iota(jnp.int32, (n_sel, Q_TILE), 1)

    def select_stage(t):
        tok = slice(t * Q_TILE, (t + 1) * Q_TILE)
        qf = q_ref[0, tok, :].astype(f32)
        parts = []
        for r in range(HEADS_PER_KV):
            pair = qf[:, (r // 2) * LANES:(r // 2 + 1) * LANES]
            parts.append(pltpu.roll(pair, HEAD_DIM, 1) if r % 2 else pair)
        qh = jnp.concatenate(parts, axis=0)
        q_plain = qh.astype(bf16)

        s = _dot_nt(q_plain, kc) + bc_ref[0, t]
        p = jnp.exp2(s - jnp.max(s, axis=-1, keepdims=True))
        norm = 1.0 / jnp.sum(p, axis=-1, keepdims=True)
        if (t + 1) * Q_TILE > CMP_LEN - 1 >= t * Q_TILE:
            pos = t * Q_TILE + (lax.broadcasted_iota(jnp.int32, (rows, 1), 0) & (Q_TILE - 1))
            norm = jnp.where(pos >= CMP_LEN - 1, norm, 0.0)
        p = p * norm
        o_c = _dot(p.astype(bf16), vc)

        n_causal = ((t + 1) * Q_TILE - 1) // SEL_BLOCK + 1
        if n_causal <= N_SELECT:
            return q_plain, jnp.where(lane_rows < HEAD_DIM, qh, 0.0).astype(bf16), o_c

        psum = p[head_rows[0]] + p[head_rows[1]] + p[head_rows[2]] + p[head_rows[3]]
        p_hi = psum.astype(bf16)
        p_lo = (psum - p_hi.astype(f32)).astype(bf16)
        imp = _dot_nt(mt, p_hi) + _dot_nt(mt, p_lo)
        dblk = (t * Q_TILE + qcol) // SEL_BLOCK - jj
        forced = (jj == 0) | ((dblk >= 0) & (dblk < N_LOCAL_FORCED))
        imp = jnp.where(forced, FORCE_SCORE, jnp.where(dblk >= 0, imp, -FORCE_SCORE))
        rank = jnp.zeros((n_sel, Q_TILE), f32)
        for i in range(n_causal):
            row = imp[i:i + 1, :]
            ahead = (row > imp) | ((row == imp) & (jj > i))
            rank = rank + jnp.where(ahead, 1.0, 0.0)
        not_sel_t = jnp.where(rank < N_SELECT, 0.0, 1.0).astype(bf16)
        mask_lanes = _dot_tn(not_sel_t, place_ref[...])
        mask_rows = jnp.concatenate([mask_lanes] * HEADS_PER_KV, axis=0)
        q_masked = jnp.where(lane_rows < HEAD_DIM, qh, mask_rows).astype(bf16)
        return q_plain, q_masked, o_c

    def attend_stage(t, q_plain, q_masked, o_c):
        tok = slice(t * Q_TILE, (t + 1) * Q_TILE)
        o_s = _attend(q_masked, ks_ref, vs_ref, t, 0, bs_ref)
        first_win = max((t - WINDOW // Q_TILE) // TILE_RATIO, 0)
        o_w = _attend(q_plain, kw_ref, vw_ref, t, first_win, bw_ref)
        gt = jax.nn.sigmoid(gt_ref[0, tok, :])
        outs = [gt[:, 3 * r:3 * r + 1] * o_c[head_rows[r]] + gt[:, 3 * r + 1:3 * r + 2] * o_s[head_rows[r]]
                + gt[:, 3 * r + 2:3 * r + 3] * o_w[head_rows[r]] for r in range(HEADS_PER_KV)]
        pairs = [jnp.where(lane < HEAD_DIM, outs[2 * h], pltpu.roll(outs[2 * h + 1], HEAD_DIM, 1))
                 for h in range(HEADS_PER_KV // 2)]
        o_ref[0, tok, :] = jnp.concatenate(pairs, axis=-1)

    staged = select_stage(0)
    for t in range(n_tiles):
        nxt = select_stage(t + 1) if t + 1 < n_tiles else None
        attend_stage(t, *staged)
        staged = nxt


def _nsa(q, kc, vc, ks, vs, kw, vw, gt, bc, bs, bw, mt, place):
    batch, seq, _ = q.shape
    nq = seq // Q_TILE
    rows = HEADS_PER_KV * Q_TILE
    ncmp = kc.shape[2]
    width = HEADS_PER_KV * HEAD_DIM
    kv_spec = pl.BlockSpec((1, seq, LANES), lambda b, gi: (b, 0, gi))
    cmp_spec = pl.BlockSpec((1, 1, ncmp, LANES), lambda b, gi: (b, gi, 0, 0))
    return pl.pallas_call(
        _nsa_body,
        grid=(batch, N_KV),
        in_specs=[pl.BlockSpec((1, seq, width), lambda b, gi: (b, 0, gi)),
                  cmp_spec, cmp_spec, kv_spec, kv_spec, kv_spec, kv_spec,
                  pl.BlockSpec((1, seq, LANES), lambda b, gi: (b, 0, gi)),
                  pl.BlockSpec((1, nq, rows, ncmp), lambda b, gi: (gi, 0, 0, 0)),
                  pl.BlockSpec((1, N_SEL_BIAS, rows, K_TILE), lambda b, gi: (gi, 0, 0, 0)),
                  pl.BlockSpec((1, N_WIN_BIAS, rows, K_TILE), lambda b, gi: (gi, 0, 0, 0)),
                  _const_spec(mt.shape), _const_spec(place.shape)],
        out_specs=pl.BlockSpec((1, seq, width), lambda b, gi: (b, 0, gi)),
        out_shape=jax.ShapeDtypeStruct((batch, seq, N_KV * width), f32),
        compiler_params=_params(("parallel", "arbitrary")),
        name="nsa",
    )(q, kc, vc, ks, vs, kw, vw, gt, bc, bs, bw, mt, place)


def _out_proj_body(x_ref, yl_ref, yn_ref, gn_ref, w_ref, gp_ref, o_ref):
    yn = _rms(yn_ref[...], gn_ref[...]).astype(bf16)
    y = _dot(yl_ref[...], w_ref[0:LRU_WIDTH, :]) + _dot(yn, w_ref[LRU_WIDTH:, :])
    o_ref[...] = x_ref[...] + _rms(y, gp_ref[...])


def _out_proj(x2, yl, yn, gn, w, gp, tm=512):
    m = x2.shape[0]
    return pl.pallas_call(
        _out_proj_body,
        grid=(m // tm,),
        in_specs=[pl.BlockSpec((tm, D_MODEL), lambda i: (i, 0)),
                  pl.BlockSpec((tm, LRU_WIDTH), lambda i: (i, 0)),
                  pl.BlockSpec((tm, D_MODEL - LRU_WIDTH), lambda i: (i, 0)),
                  _const_spec((1, D_MODEL - LRU_WIDTH)),
                  _const_spec((D_MODEL, D_MODEL)),
                  _const_spec((1, D_MODEL))],
        out_specs=pl.BlockSpec((tm, D_MODEL), lambda i: (i, 0)),
        out_shape=jax.ShapeDtypeStruct((m, D_MODEL), f32),
        compiler_params=_params(("parallel",)),
        name="out_proj",
    )(x2, yl, yn, gn, w, gp)


def _mlp_body(h_ref, g1_ref, w1_ref, w2_ref, g2_ref, o_ref, a_sc, acc_sc):
    j = pl.program_id(1)

    @pl.when(j == 0)
    def _():
        a_sc[...] = _rms(h_ref[...], g1_ref[...]).astype(bf16)
        acc_sc[...] = jnp.zeros_like(acc_sc)

    hid = jnp.maximum(_dot(a_sc[...], w1_ref[...]), 0.0)
    acc_sc[...] += _dot((hid * hid).astype(bf16), w2_ref[...])

    @pl.when(j == pl.num_programs(1) - 1)
    def _():
        o_ref[...] = h_ref[...] + _rms(acc_sc[...], g2_ref[...])


def _mlp(h, g1, w1, w2, g2, tm=512, tf=512):
    m = h.shape[0]
    return pl.pallas_call(
        _mlp_body,
        grid=(m // tm, D_FF // tf),
        in_specs=[pl.BlockSpec((tm, D_MODEL), lambda i, j: (i, 0)),
                  _const_spec((1, D_MODEL)),
                  pl.BlockSpec((D_MODEL, tf), lambda i, j: (0, j)),
                  pl.BlockSpec((tf, D_MODEL), lambda i, j: (j, 0)),
                  _const_spec((1, D_MODEL))],
        out_specs=pl.BlockSpec((tm, D_MODEL), lambda i, j: (i, 0)),
        out_shape=jax.ShapeDtypeStruct((m, D_MODEL), f32),
        scratch_shapes=[pltpu.VMEM((tm, D_MODEL), bf16), pltpu.VMEM((tm, D_MODEL), f32)],
        compiler_params=_params(("parallel", "arbitrary")),
        name="mlp",
    )(h, g1, w1, w2, g2)


def _ple_body(h_ref, p_ref, wg_ref, wp_ref, o_ref):
    h = h_ref[...]
    gate = jax.nn.sigmoid(_dot(h.astype(bf16), wg_ref[...]))
    o_ref[...] = h + gate * _dot(p_ref[...].astype(bf16), wp_ref[...])


def _ple(h, p2, wg, wp, tm=512):
    m = h.shape[0]
    return pl.pallas_call(
        _ple_body,
        grid=(m // tm,),
        in_specs=[pl.BlockSpec((tm, D_MODEL), lambda i: (i, 0)),
                  pl.BlockSpec((tm, PLE_DIM), lambda i: (i, 0)),
                  _const_spec((D_MODEL, D_MODEL)),
                  _const_spec((PLE_DIM, D_MODEL))],
        out_specs=pl.BlockSpec((tm, D_MODEL), lambda i: (i, 0)),
        out_shape=jax.ShapeDtypeStruct((m, D_MODEL), f32),
        compiler_params=_params(("parallel",)),
        name="ple",
    )(h, p2, wg, wp)


def _block_diag_chunks(w):
    per = 256 // LRU_BLOCK_DIM
    w = w.reshape(LRU_BLOCKS // per, per, LRU_BLOCK_DIM, LRU_BLOCK_DIM)
    eye = jnp.eye(per, dtype=w.dtype)
    return jnp.einsum('cpij,pq->cpiqj', w, eye).reshape(LRU_BLOCKS // per, 256, 256)


def _layer(h, p_i, i, prm, bias):
    batch, seq, _ = h.shape
    m = batch * seq
    x2 = h.reshape(m, D_MODEL)
    row = lambda v: v.reshape(1, -1)
    per_batch = lambda t: t.reshape(batch, seq, t.shape[-1])

    w_in = jnp.pad(prm["w_in"][i], ((0, 0), (0, IN_PAD - IN_DIM))).astype(bf16)
    u, gl, q, kc, vc, ks, vs, kw, vw, gts = _in_proj(x2, row(prm["norm_mix_pre"][i]), w_in, seq)

    y_lru = _rglru(u, gl, prm["conv_w"][i], row(prm["conv_b"][i]),
                   _block_diag_chunks(prm["lru_wa"][i]).astype(bf16), row(prm["lru_ba"][i]),
                   _block_diag_chunks(prm["lru_wx"][i]).astype(bf16), row(prm["lru_bx"][i]),
                   row(prm["lru_lambda"][i]), row(prm["gnorm_lru"][i]), batch, seq)

    lane_pad = lambda w: jnp.pad(w, ((0, 0), (0, LANES - HEAD_DIM))).astype(bf16)
    kcc, vcc = _compress(per_batch(kc), per_batch(vc),
                         prm["cmp_pe_k"][i].reshape(1, -1), prm["cmp_w1_k"][i].astype(bf16),
                         lane_pad(prm["cmp_w2_k"][i]),
                         prm["cmp_pe_v"][i].reshape(1, -1), prm["cmp_w1_v"][i].astype(bf16),
                         lane_pad(prm["cmp_w2_v"][i]))

    bc, bs, bw = bias
    n_sel = seq // SEL_BLOCK
    y_nsa = _nsa(per_batch(q), kcc, vcc, per_batch(ks), per_batch(vs), per_batch(kw), per_batch(vw),
                 per_batch(gts), bc, bs, bw,
                 jnp.asarray(_importance_matrix_t(seq)).astype(bf16),
                 jnp.asarray(_mask_placement(n_sel)).astype(bf16))

    h1 = _out_proj(x2, y_lru, y_nsa.reshape(m, -1), row(prm["gnorm_nsa"][i]),
                   prm["w_out"][i].astype(bf16), row(prm["norm_mix_post"][i]))
    h2 = _mlp(h1, row(prm["norm_mlp_pre"][i]), prm["mlp_w1"][i].astype(bf16),
              prm["mlp_w2"][i].astype(bf16), row(prm["norm_mlp_post"][i]))
    h3 = _ple(h2, p_i.reshape(m, PLE_DIM), prm["ple_gate"][i].astype(bf16), prm["ple_proj"][i].astype(bf16))
    return h3.reshape(batch, seq, D_MODEL)


def kernel(x, p, norm_mix_pre, norm_mix_post, norm_mlp_pre, norm_mlp_post, w_in, conv_w, conv_b, lru_wa, lru_ba, lru_wx, lru_bx, lru_lambda, cmp_pe_k, cmp_w1_k, cmp_w2_k, cmp_pe_v, cmp_w1_v, cmp_w2_v, rel_bias, gnorm_lru, gnorm_nsa, w_out, mlp_w1, mlp_w2, ple_gate, ple_proj):
    prm = dict(norm_mix_pre=norm_mix_pre, norm_mix_post=norm_mix_post, norm_mlp_pre=norm_mlp_pre,
               norm_mlp_post=norm_mlp_post, w_in=w_in, conv_w=conv_w, conv_b=conv_b, lru_wa=lru_wa,
               lru_ba=lru_ba, lru_wx=lru_wx, lru_bx=lru_bx, lru_lambda=lru_lambda, cmp_pe_k=cmp_pe_k,
               cmp_w1_k=cmp_w1_k, cmp_w2_k=cmp_w2_k, cmp_pe_v=cmp_pe_v, cmp_w1_v=cmp_w1_v,
               cmp_w2_v=cmp_w2_v, gnorm_lru=gnorm_lru, gnorm_nsa=gnorm_nsa, w_out=w_out, mlp_w1=mlp_w1,
               mlp_w2=mlp_w2, ple_gate=ple_gate, ple_proj=ple_proj)
    bias = _bias_tiles(rel_bias, x.shape[1])
    h = x
    for i in range(w_in.shape[0]):
        h = _layer(h, p[i], i, prm, bias)
    return h
```

```python
import functools
import math

import numpy as np
import jax
import jax.numpy as jnp
from jax import lax
from jax.experimental import pallas as pl
from jax.experimental.pallas import tpu as pltpu

D_MODEL = 2048
PLE_DIM = 256
LRU_WIDTH = 1024
LRU_BLOCKS = 16
LRU_BLOCK_DIM = 64
CONV_WIDTH = 4
LRU_C = 8.0
HEAD_DIM = 64
N_HEADS = 16
N_KV = 4
HEADS_PER_KV = 4
CMP_LEN = 32
CMP_STRIDE = 16
CMP_HIDDEN = 256
SEL_BLOCK = 64
N_SELECT = 16
N_LOCAL_FORCED = 2
WINDOW = 512
N_BUCKETS = 32
MAX_DISTANCE = 128
D_FF = 4 * D_MODEL
NORM_EPS = 1e-6
NEG_INF = -1e30
FORCE_SCORE = 1e4
IN_DIM = 4656
IN_PAD = 4736
GATE_LO = 4608

LANES = 128
MASK_LO = HEAD_DIM
Q_TILE = 128
K_TILE = 256
TILE_RATIO = K_TILE // Q_TILE
N_SEL_BIAS = 3
N_WIN_BIAS = (WINDOW + Q_TILE) // Q_TILE + 1
LOG2E = math.log2(math.e)

VMEM_LIMIT = 56 * 1024 * 1024

f32 = jnp.float32
bf16 = jnp.bfloat16


def _rms(x, g):
    return x * lax.rsqrt(jnp.mean(x * x, axis=-1, keepdims=True) + NORM_EPS) * g


def _dot(a, b):
    return jnp.dot(a, b, preferred_element_type=f32)


def _dot_nt(a, b):
    return lax.dot_general(a, b, (((1,), (1,)), ((), ())), preferred_element_type=f32)


def _dot_tn(a, b):
    return lax.dot_general(a, b, (((0,), (0,)), ((), ())), preferred_element_type=f32)


def _const_spec(shape):
    nd = len(shape)
    return pl.BlockSpec(shape, lambda *_: (0,) * nd)


def _params(sem):
    return pltpu.CompilerParams(dimension_semantics=sem, vmem_limit_bytes=VMEM_LIMIT)


def _spread_groups(z, fill):
    lane = lax.broadcasted_iota(jnp.int32, (z.shape[0], LANES), 1)
    parts = []
    for g in range(N_KV):
        pair = z[:, (g // 2) * LANES:(g // 2 + 1) * LANES]
        if g % 2:
            pair = pltpu.roll(pair, HEAD_DIM, 1)
        parts.append(jnp.where(lane < HEAD_DIM, pair, fill))
    return jnp.concatenate(parts, axis=1)


def _in_proj_body(seq, x_ref, g_ref, w_ref, u_ref, gl_ref, q_ref, kc_ref, vc_ref,
                  ks_ref, vs_ref, kw_ref, vw_ref, gt_ref):
    tm = x_ref.shape[0]
    a = _rms(x_ref[...], g_ref[...]).astype(bf16)

    def proj(lo, hi):
        return _dot(a, w_ref[:, lo:hi])

    u_ref[...] = proj(0, 1024)
    gl_ref[...] = proj(1024, 2048)
    q_ref[...] = (proj(2048, 3072) * (LOG2E * HEAD_DIM ** -0.5)).astype(bf16)
    kc_ref[...] = proj(3072, 3328)
    vc_ref[...] = proj(3328, 3584)

    lane = lax.broadcasted_iota(jnp.int32, (tm, LANES), 1)
    pos = (pl.program_id(0) * tm) % seq + lax.broadcasted_iota(jnp.int32, (tm, LANES), 0)
    block_onehot = jnp.where(lane - MASK_LO == pos // SEL_BLOCK, 1.0, 0.0)
    ones_lane = jnp.where(lane == MASK_LO, 1.0, 0.0)
    ks_ref[...] = _spread_groups(proj(3584, 3840), block_onehot).astype(bf16)
    vs_ref[...] = _spread_groups(proj(3840, 4096), ones_lane).astype(bf16)
    kw_ref[...] = _spread_groups(proj(4096, 4352), 0.0).astype(bf16)
    vw_ref[...] = _spread_groups(proj(4352, 4608), ones_lane).astype(bf16)
    gt = proj(GATE_LO, IN_PAD)
    per = 3 * HEADS_PER_KV
    gt_ref[...] = jnp.concatenate(
        [gt if g == 0 else pltpu.roll(gt, LANES - per * g, 1) for g in range(N_KV)], axis=1)


def _in_proj(x2, g, w, seq, tm=512):
    m = x2.shape[0]
    wide = N_KV * LANES
    widths = [(1024, f32), (1024, f32), (1024, bf16), (256, f32), (256, f32),
              (wide, bf16), (wide, bf16), (wide, bf16), (wide, bf16), (wide, f32)]
    body = lambda *refs: _in_proj_body(seq, *refs)
    return pl.pallas_call(
        body,
        grid=(m // tm,),
        in_specs=[pl.BlockSpec((tm, D_MODEL), lambda i: (i, 0)),
                  _const_spec((1, D_MODEL)),
                  pl.BlockSpec((D_MODEL, IN_PAD), lambda i: (0, 0), pipeline_mode=pl.Buffered(1))],
        out_specs=[pl.BlockSpec((tm, n), lambda i: (i, 0)) for n, _ in widths],
        out_shape=[jax.ShapeDtypeStruct((m, n), dt) for n, dt in widths],
        compiler_params=_params(("parallel",)),
        name="in_proj",
    )(x2, g, w)


def _rglru_body(u_ref, gl_ref, cw_ref, cb_ref, wa_ref, ba_ref, wx_ref, bx_ref, lam_ref, gn_ref,
                o_ref, ubuf, a_sc, b_sc, hc_sc):
    t = u_ref.shape[0]

    @pl.when(pl.program_id(1) == 0)
    def _():
        ubuf[...] = jnp.zeros_like(ubuf)
        hc_sc[...] = jnp.zeros_like(hc_sc)

    u = u_ref[...]
    u3 = u.reshape(t // 8, 8, LRU_WIDTH)
    tail = ubuf[...]
    cw = cw_ref[...]
    row8w = lax.broadcasted_iota(jnp.int32, (t // 8, 8, LRU_WIDTH), 1)
    xc3 = cb_ref[...] + u3 * cw[CONV_WIDTH - 1]
    for d in range(1, CONV_WIDTH):
        cur = pltpu.roll(u3, d, 1)
        prev = jnp.concatenate([pltpu.roll(tail, d, 0)[None], cur[:-1]], axis=0)
        xc3 = xc3 + jnp.where(row8w >= d, cur, prev) * cw[CONV_WIDTH - 1 - d]
    ubuf[...] = u[t - 8:t, :]
    xc = xc3.reshape(t, LRU_WIDTH)

    xb = xc.astype(bf16)
    sp = jax.nn.softplus(-lam_ref[...])
    row8 = lax.broadcasted_iota(jnp.int32, (t // 8, 8, 256), 1)
    for c in range(LRU_WIDTH // 256):
        sl = slice(c * 256, (c + 1) * 256)
        xcb = xb[:, sl]
        r = jax.nn.sigmoid(_dot(xcb, wa_ref[c]) + ba_ref[:, sl])
        ig = jax.nn.sigmoid(_dot(xcb, wx_ref[c]) + bx_ref[:, sl])
        log_a = (-LRU_C) * r * sp[:, sl]
        a = jnp.exp(log_a)
        b = jnp.sqrt(1.0 - a * a) * (ig * xc[:, sl])
        a = a.reshape(t // 8, 8, 256)
        b = b.reshape(t // 8, 8, 256)
        for d in (1, 2, 4):
            keep = row8 >= d
            a_prev = pltpu.roll(a, d, 1)
            b_prev = pltpu.roll(b, d, 1)
            b = jnp.where(keep, a * b_prev + b, b)
            a = jnp.where(keep, a * a_prev, a)
        a_sc[:, sl] = a.reshape(t, 256)
        b_sc[:, sl] = b.reshape(t, 256)

    def group(gi, h):
        off = pl.multiple_of(gi * 8, 8)
        hg = b_sc[pl.ds(off, 8), :] + a_sc[pl.ds(off, 8), :] * h
        b_sc[pl.ds(off, 8), :] = hg
        return hg[7:8, :]

    h_last = lax.fori_loop(0, t // 8, group, hc_sc[0:1, :])
    hc_sc[0:1, :] = h_last
    y = b_sc[...] * jax.nn.gelu(gl_ref[...])
    o_ref[...] = _rms(y, gn_ref[...]).astype(o_ref.dtype)


def _rglru(u, gl, cw, cb, wa, ba, wx, bx, lam, gn, batch, seq, t=256):
    ns = seq // t
    row = pl.BlockSpec((t, LRU_WIDTH), lambda b, s: (b * ns + s, 0))
    vec = _const_spec((1, LRU_WIDTH))
    wspec = _const_spec((LRU_WIDTH // 256, 256, 256))
    return pl.pallas_call(
        _rglru_body,
        grid=(batch, ns),
        in_specs=[row, row, _const_spec((CONV_WIDTH, LRU_WIDTH)), vec, wspec, vec, wspec, vec, vec, vec],
        out_specs=row,
        out_shape=jax.ShapeDtypeStruct((batch * seq, LRU_WIDTH), bf16),
        scratch_shapes=[pltpu.VMEM((8, LRU_WIDTH), f32), pltpu.VMEM((t, LRU_WIDTH), f32),
                        pltpu.VMEM((t, LRU_WIDTH), f32), pltpu.VMEM((8, LRU_WIDTH), f32)],
        compiler_params=_params(("parallel", "arbitrary")),
        name="rglru",
    )(u, gl, cw, cb, wa, ba, wx, bx, lam, gn)


def _compress_body(kc_ref, vc_ref, pek_ref, w1k_ref, w2k_ref, pev_ref, w1v_ref, w2v_ref, ko_ref, vo_ref):
    nrow = ko_ref.shape[2]
    rows = lax.broadcasted_iota(jnp.int32, (nrow, LANES), 0)

    def one(t_ref, pe_ref, w1_ref, w2_ref, o_ref):
        pe = jnp.broadcast_to(pe_ref[...], (8, CMP_LEN * HEAD_DIM)).astype(bf16)
        c0 = _dot(pe, w1_ref[...])[0:1, :]
        toks = [t_ref[0, pl.ds(j, nrow, stride=CMP_STRIDE), :] for j in range(CMP_STRIDE)]
        for g in range(LANES // HEAD_DIM):
            lo = jnp.zeros((nrow, CMP_HIDDEN), f32)
            hi = jnp.zeros((nrow, CMP_HIDDEN), f32)
            for j in range(CMP_STRIDE):
                tj = toks[j][:, g * HEAD_DIM:(g + 1) * HEAD_DIM].astype(bf16)
                lo = lo + _dot(tj, w1_ref[j * HEAD_DIM:(j + 1) * HEAD_DIM, :])
                hi = hi + _dot(tj, w1_ref[(CMP_STRIDE + j) * HEAD_DIM:(CMP_STRIDE + j + 1) * HEAD_DIM, :])
            hid = jax.nn.gelu(lo + pltpu.roll(hi, nrow - 1, 0) + c0)
            out = _dot(hid.astype(bf16), w2_ref[...])
            o_ref[0, g] = jnp.where(rows < nrow - 1, out, 0.0).astype(o_ref.dtype)

    one(kc_ref, pek_ref, w1k_ref, w2k_ref, ko_ref)
    one(vc_ref, pev_ref, w1v_ref, w2v_ref, vo_ref)


def _compress(kc, vc, pek, w1k, w2k, pev, w1v, w2v):
    batch, seq, width = kc.shape
    nrow = seq // CMP_STRIDE
    pair = LANES // HEAD_DIM
    tspec = pl.BlockSpec((1, seq, LANES), lambda b, h: (b, 0, h))
    ospec = pl.BlockSpec((1, pair, nrow, LANES), lambda b, h: (b, h, 0, 0))
    wts = [_const_spec((1, CMP_LEN * HEAD_DIM)), _const_spec((CMP_LEN * HEAD_DIM, CMP_HIDDEN)),
           _const_spec((CMP_HIDDEN, LANES))]
    return pl.pallas_call(
        _compress_body,
        grid=(batch, width // LANES),
        in_specs=[tspec, tspec] + wts + wts,
        out_specs=[ospec, ospec],
        out_shape=[jax.ShapeDtypeStruct((batch, N_KV, nrow, LANES), bf16)] * 2,
        compiler_params=_params(("parallel", "parallel")),
        name="compress",
    )(kc, vc, pek, w1k, w2k, pev, w1v, w2v)


def _bucket_thresholds():
    n = np.arange(0, 4096)
    max_exact = N_BUCKETS // 2
    nf = np.maximum(n, 1).astype(np.float32)
    large = max_exact + (np.log(nf / np.float32(max_exact)) / np.float32(math.log(MAX_DISTANCE / max_exact))
                         * np.float32(N_BUCKETS - max_exact)).astype(np.int32)
    large = np.minimum(large, N_BUCKETS - 1)
    bucket = np.where(n < max_exact, n, large)
    assert np.all(np.diff(bucket) >= 0) and bucket[0] == 0 and bucket[-1] == N_BUCKETS - 1
    return [int(np.argmax(bucket >= k)) for k in range(N_BUCKETS)]


_BUCKET_THR = _bucket_thresholds()
assert _BUCKET_THR[-1] <= N_SEL_BIAS * Q_TILE - K_TILE + 1


def _bias_of_dist(dist, ok, tab_ref, head):
    last = tab_ref[N_BUCKETS - 1, head]
    val = jnp.full(dist.shape, (tab_ref[0, head] - last) * LOG2E, f32)
    for k in range(1, N_BUCKETS - 1):
        val = jnp.where(dist >= _BUCKET_THR[k], (tab_ref[k, head] - last) * LOG2E, val)
    val = jnp.where(dist >= _BUCKET_THR[N_BUCKETS - 1], 0.0, val)
    return jnp.where(ok, val, NEG_INF)


def _bias_body(tab_ref, bc_ref, bs_ref, bw_ref):
    g = pl.program_id(0)
    qt = pl.program_id(1)
    ncmp = bc_ref.shape[3]
    for r in range(HEADS_PER_KV):
        head = g * HEADS_PER_KV + r
        rows = slice(r * Q_TILE, (r + 1) * Q_TILE)
        i = lax.broadcasted_iota(jnp.int32, (Q_TILE, ncmp), 0)
        n = lax.broadcasted_iota(jnp.int32, (Q_TILE, ncmp), 1)
        dist = qt * Q_TILE + i - (n * CMP_STRIDE + CMP_LEN - 1)
        bc_ref[0, 0, rows, :] = _bias_of_dist(dist, (dist >= 0) & (n < ncmp - 1), tab_ref, head)

    @pl.when(qt == 0)
    def _():
        i = lax.broadcasted_iota(jnp.int32, (Q_TILE, K_TILE), 0)
        j = lax.broadcasted_iota(jnp.int32, (Q_TILE, K_TILE), 1)
        for r in range(HEADS_PER_KV):
            head = g * HEADS_PER_KV + r
            rows = slice(r * Q_TILE, (r + 1) * Q_TILE)
            for d in range(N_WIN_BIAS):
                dist = d * Q_TILE + i - j
                if d < N_SEL_BIAS:
                    bs_ref[0, d, rows, :] = _bias_of_dist(dist, dist >= 0, tab_ref, head)
                bw_ref[0, d, rows, :] = _bias_of_dist(dist, (dist >= 0) & (dist < WINDOW), tab_ref, head)


def _bias_tiles(rel_bias, seq):
    nq = seq // Q_TILE
    ncmp = seq // CMP_STRIDE
    rows = HEADS_PER_KV * Q_TILE
    return pl.pallas_call(
        _bias_body,
        grid=(N_KV, nq),
        in_specs=[pl.BlockSpec(memory_space=pltpu.SMEM)],
        out_specs=[pl.BlockSpec((1, 1, rows, ncmp), lambda g, q: (g, q, 0, 0)),
                   pl.BlockSpec((1, N_SEL_BIAS, rows, K_TILE), lambda g, q: (g, 0, 0, 0)),
                   pl.BlockSpec((1, N_WIN_BIAS, rows, K_TILE), lambda g, q: (g, 0, 0, 0))],
        out_shape=[jax.ShapeDtypeStruct((N_KV, nq, rows, ncmp), f32),
                   jax.ShapeDtypeStruct((N_KV, N_SEL_BIAS, rows, K_TILE), f32),
                   jax.ShapeDtypeStruct((N_KV, N_WIN_BIAS, rows, K_TILE), f32)],
        compiler_params=_params(("parallel", "arbitrary")),
        name="bias_tiles",
    )(rel_bias)


def _importance_matrix_t(seq):
    n_cmp = (seq - CMP_LEN) // CMP_STRIDE + 1
    n_sel = seq // SEL_BLOCK
    ratio_sel = SEL_BLOCK // CMP_STRIDE
    ratio_cmp = CMP_LEN // CMP_STRIDE
    jj = np.arange(n_sel)[:, None, None]
    ci = ratio_sel * jj + np.arange(ratio_sel)[None, :, None] - np.arange(ratio_cmp)[None, None, :]
    jb = np.broadcast_to(jj, ci.shape)
    ok = (ci >= 0) & (ci < n_cmp)
    m = np.zeros((n_sel, seq // CMP_STRIDE), np.float32)
    np.add.at(m, (jb[ok], ci[ok]), 1.0)
    return m


def _mask_placement(n_sel):
    p = np.zeros((n_sel, LANES), np.float32)
    p[np.arange(n_sel), MASK_LO + np.arange(n_sel)] = NEG_INF
    return p


def _attend(q, k_ref, v_ref, t, first_tile, bias_ref):
    last_tile = t // TILE_RATIO
    lo = first_tile * K_TILE
    hi = (last_tile + 1) * K_TILE
    s = _dot_nt(q, k_ref[0, lo:hi, :])
    pieces = []
    for kt in range(first_tile, last_tile + 1):
        piece = s[:, (kt - first_tile) * K_TILE:(kt - first_tile + 1) * K_TILE]
        idx = t - kt * TILE_RATIO
        if idx < bias_ref.shape[1]:
            piece = piece + bias_ref[0, idx]
        pieces.append(piece)
    m = jnp.max(functools.reduce(jnp.maximum, pieces), axis=-1, keepdims=True)
    p = jnp.concatenate([jnp.exp2(pc - m).astype(bf16) for pc in pieces], axis=-1)
    acc = _dot(p, v_ref[0, lo:hi, :])
    return acc * (1.0 / acc[:, MASK_LO:MASK_LO + 1])


def _nsa_body(q_ref, kc_ref, vc_ref, ks_ref, vs_ref, kw_ref, vw_ref, gt_ref, bc_ref, bs_ref, bw_ref,
              mt_ref, place_ref, o_ref):
    n_sel = mt_ref.shape[0]
    n_tiles = q_ref.shape[1] // Q_TILE
    rows = HEADS_PER_KV * Q_TILE
    lane = lax.broadcasted_iota(jnp.int32, (Q_TILE, LANES), 1)
    lane_rows = lax.broadcasted_iota(jnp.int32, (rows, LANES), 1)
    head_rows = [slice(r * Q_TILE, (r + 1) * Q_TILE) for r in range(HEADS_PER_KV)]
    kc = kc_ref[0, 0]
    vc = vc_ref[0, 0]
    mt = mt_ref[...]
    jj = lax.broadcasted_iota(jnp.int32, (n_sel, Q_TILE), 0)
    qcol = lax.broadcasted_iota(jnp.int32, (n_sel, Q_TILE), 1)

    def select_stage(t):
        tok = slice(t * Q_TILE, (t + 1) * Q_TILE)
        qf = q_ref[0, tok, :].astype(f32)
        parts = []
        for r in range(HEADS_PER_KV):
            pair = qf[:, (r // 2) * LANES:(r // 2 + 1) * LANES]
            parts.append(pltpu.roll(pair, HEAD_DIM, 1) if r % 2 else pair)
        qh = jnp.concatenate(parts, axis=0)
        q_plain = qh.astype(bf16)

        s = _dot_nt(q_plain, kc) + bc_ref[0, t]
        p = jnp.exp2(s - jnp.max(s, axis=-1, keepdims=True))
        norm = 1.0 / jnp.sum(p, axis=-1, keepdims=True)
        if (t + 1) * Q_TILE > CMP_LEN - 1 >= t * Q_TILE:
            pos = t * Q_TILE + (lax.broadcasted_iota(jnp.int32, (rows, 1), 0) & (Q_TILE - 1))
            norm = jnp.where(pos >= CMP_LEN - 1, norm, 0.0)
        p = p * norm
        o_c = _dot(p.astype(bf16), vc)

        n_causal = ((t + 1) * Q_TILE - 1) // SEL_BLOCK + 1
        if n_causal <= N_SELECT:
            return q_plain, jnp.where(lane_rows < HEAD_DIM, qh, 0.0).astype(bf16), o_c

        psum = p[head_rows[0]] + p[head_rows[1]] + p[head_rows[2]] + p[head_rows[3]]
        p_hi = psum.astype(bf16)
        p_lo = (psum - p_hi.astype(f32)).astype(bf16)
        imp = _dot_nt(mt, p_hi) + _dot_nt(mt, p_lo)
        dblk = (t * Q_TILE + qcol) // SEL_BLOCK - jj
        forced = (jj == 0) | ((dblk >= 0) & (dblk < N_LOCAL_FORCED))
        imp = jnp.where(forced, FORCE_SCORE, jnp.where(dblk >= 0, imp, -FORCE_SCORE))
        rank = jnp.zeros((n_sel, Q_TILE), f32)
        for i in range(n_causal):
            row = imp[i:i + 1, :]
            ahead = (row > imp) | ((row == imp) & (jj > i))
            rank = rank + jnp.where(ahead, 1.0, 0.0)
        not_sel_t = jnp.where(rank < N_SELECT, 0.0, 1.0).astype(bf16)
        mask_lanes = _dot_tn(not_sel_t, place_ref[...])
        mask_rows = jnp.concatenate([mask_lanes] * HEADS_PER_KV, axis=0)
        q_masked = jnp.where(lane_rows < HEAD_DIM, qh, mask_rows).astype(bf16)
        return q_plain, q_masked, o_c

    def attend_stage(t, q_plain, q_masked, o_c):
        tok = slice(t * Q_TILE, (t + 1) * Q_TILE)
        o_s = _attend(q_masked, ks_ref, vs_ref, t, 0, bs_ref)
        first_win = max((t - WINDOW // Q_TILE) // TILE_RATIO, 0)
        o_w = _attend(q_plain, kw_ref, vw_ref, t, first_win, bw_ref)
        gt = jax.nn.sigmoid(gt_ref[0, tok, :])
        outs = [gt[:, 3 * r:3 * r + 1] * o_c[head_rows[r]] + gt[:, 3 * r + 1:3 * r + 2] * o_s[head_rows[r]]
                + gt[:, 3 * r + 2:3 * r + 3] * o_w[head_rows[r]] for r in range(HEADS_PER_KV)]
        pairs = [jnp.where(lane < HEAD_DIM, outs[2 * h], pltpu.roll(outs[2 * h + 1], HEAD_DIM, 1))
                 for h in range(HEADS_PER_KV // 2)]
        o_ref[0, tok, :] = jnp.concatenate(pairs, axis=-1)

    staged = select_stage(0)
    for t in range(n_tiles):
        nxt = select_stage(t + 1) if t + 1 < n_tiles else None
        attend_stage(t, *staged)
        staged = nxt


def _nsa(q, kc, vc, ks, vs, kw, vw, gt, bc, bs, bw, mt, place):
    batch, seq, _ = q.shape
    nq = seq // Q_TILE
    rows = HEADS_PER_KV * Q_TILE
    ncmp = kc.shape[2]
    width = HEADS_PER_KV * HEAD_DIM
    kv_spec = pl.BlockSpec((1, seq, LANES), lambda b, gi: (b, 0, gi))
    cmp_spec = pl.BlockSpec((1, 1, ncmp, LANES), lambda b, gi: (b, gi, 0, 0))
    return pl.pallas_call(
        _nsa_body,
        grid=(batch, N_KV),
        in_specs=[pl.BlockSpec((1, seq, width), lambda b, gi: (b, 0, gi)),
                  cmp_spec, cmp_spec, kv_spec, kv_spec, kv_spec, kv_spec,
                  pl.BlockSpec((1, seq, LANES), lambda b, gi: (b, 0, gi)),
                  pl.BlockSpec((1, nq, rows, ncmp), lambda b, gi: (gi, 0, 0, 0)),
                  pl.BlockSpec((1, N_SEL_BIAS, rows, K_TILE), lambda b, gi: (gi, 0, 0, 0)),
                  pl.BlockSpec((1, N_WIN_BIAS, rows, K_TILE), lambda b, gi: (gi, 0, 0, 0)),
                  _const_spec(mt.shape), _const_spec(place.shape)],
        out_specs=pl.BlockSpec((1, seq, width), lambda b, gi: (b, 0, gi)),
        out_shape=jax.ShapeDtypeStruct((batch, seq, N_KV * width), f32),
        compiler_params=_params(("parallel", "arbitrary")),
        name="nsa",
    )(q, kc, vc, ks, vs, kw, vw, gt, bc, bs, bw, mt, place)


def _out_proj_body(x_ref, yl_ref, yn_ref, gn_ref, w_ref, gp_ref, gm_ref, o_ref, a_ref):
    yn = _rms(yn_ref[...], gn_ref[...]).astype(bf16)
    y = _dot(yl_ref[...], w_ref[0:LRU_WIDTH, :]) + _dot(yn, w_ref[LRU_WIDTH:, :])
    h = x_ref[...] + _rms(y, gp_ref[...])
    o_ref[...] = h
    a_ref[...] = _rms(h, gm_ref[...]).astype(bf16)


def _out_proj(x2, yl, yn, gn, w, gp, gm, tm=512):
    m = x2.shape[0]
    row = pl.BlockSpec((tm, D_MODEL), lambda i: (i, 0))
    return pl.pallas_call(
        _out_proj_body,
        grid=(m // tm,),
        in_specs=[row,
                  pl.BlockSpec((tm, LRU_WIDTH), lambda i: (i, 0)),
                  pl.BlockSpec((tm, D_MODEL - LRU_WIDTH), lambda i: (i, 0)),
                  _const_spec((1, D_MODEL - LRU_WIDTH)),
                  _const_spec((D_MODEL, D_MODEL)),
                  _const_spec((1, D_MODEL)), _const_spec((1, D_MODEL))],
        out_specs=[row, row],
        out_shape=[jax.ShapeDtypeStruct((m, D_MODEL), f32), jax.ShapeDtypeStruct((m, D_MODEL), bf16)],
        compiler_params=_params(("parallel",)),
        name="out_proj",
    )(x2, yl, yn, gn, w, gp, gm)


def _mlp_body(a_ref, w1_ref, w2_ref, g2_ref, o_ref, acc_sc):
    j = pl.program_id(1)

    @pl.when(j == 0)
    def _():
        acc_sc[...] = jnp.zeros_like(acc_sc)

    hid = jnp.maximum(_dot(a_ref[...], w1_ref[...]), 0.0)
    acc_sc[...] += _dot((hid * hid).astype(bf16), w2_ref[...])

    @pl.when(j == pl.num_programs(1) - 1)
    def _():
        o_ref[...] = _rms(acc_sc[...], g2_ref[...])


def _mlp(a, w1, w2, g2, tm=1024, tf=512):
    m = a.shape[0]
    return pl.pallas_call(
        _mlp_body,
        grid=(m // tm, D_FF // tf),
        in_specs=[pl.BlockSpec((tm, D_MODEL), lambda i, j: (i, 0)),
                  pl.BlockSpec((D_MODEL, tf), lambda i, j: (0, j)),
                  pl.BlockSpec((tf, D_MODEL), lambda i, j: (j, 0)),
                  _const_spec((1, D_MODEL))],
        out_specs=pl.BlockSpec((tm, D_MODEL), lambda i, j: (i, 0)),
        out_shape=jax.ShapeDtypeStruct((m, D_MODEL), f32),
        scratch_shapes=[pltpu.VMEM((tm, D_MODEL), f32)],
        compiler_params=_params(("parallel", "arbitrary")),
        name="mlp",
    )(a, w1, w2, g2)


def _ple_body(h_ref, f_ref, p_ref, wg_ref, wp_ref, o_ref):
    h = h_ref[...] + f_ref[...]
    gate = jax.nn.sigmoid(_dot(h.astype(bf16), wg_ref[...]))
    o_ref[...] = h + gate * _dot(p_ref[...].astype(bf16), wp_ref[...])


def _ple(h, f, p2, wg, wp, tm=512):
    m = h.shape[0]
    row = pl.BlockSpec((tm, D_MODEL), lambda i: (i, 0))
    return pl.pallas_call(
        _ple_body,
        grid=(m // tm,),
        in_specs=[row, row,
                  pl.BlockSpec((tm, PLE_DIM), lambda i: (i, 0)),
                  _const_spec((D_MODEL, D_MODEL)),
                  _const_spec((PLE_DIM, D_MODEL))],
        out_specs=row,
        out_shape=jax.ShapeDtypeStruct((m, D_MODEL), f32),
        compiler_params=_params(("parallel",)),
        name="ple",
    )(h, f, p2, wg, wp)


def _block_diag_chunks(w):
    per = 256 // LRU_BLOCK_DIM
    w = w.reshape(LRU_BLOCKS // per, per, LRU_BLOCK_DIM, LRU_BLOCK_DIM)
    eye = jnp.eye(per, dtype=w.dtype)
    return jnp.einsum('cpij,pq->cpiqj', w, eye).reshape(LRU_BLOCKS // per, 256, 256)


def _layer(h, p_i, i, prm, bias):
    batch, seq, _ = h.shape
    m = batch * seq
    x2 = h.reshape(m, D_MODEL)
    row = lambda v: v.reshape(1, -1)
    per_batch = lambda t: t.reshape(batch, seq, t.shape[-1])

    w_in = jnp.pad(prm["w_in"][i], ((0, 0), (0, IN_PAD - IN_DIM))).astype(bf16)
    u, gl, q, kc, vc, ks, vs, kw, vw, gts = _in_proj(x2, row(prm["norm_mix_pre"][i]), w_in, seq)

    y_lru = _rglru(u, gl, prm["conv_w"][i], row(prm["conv_b"][i]),
                   _block_diag_chunks(prm["lru_wa"][i]).astype(bf16), row(prm["lru_ba"][i]),
                   _block_diag_chunks(prm["lru_wx"][i]).astype(bf16), row(prm["lru_bx"][i]),
                   row(prm["lru_lambda"][i]), row(prm["gnorm_lru"][i]), batch, seq)

    lane_pad = lambda w: jnp.pad(w, ((0, 0), (0, LANES - HEAD_DIM))).astype(bf16)
    kcc, vcc = _compress(per_batch(kc), per_batch(vc),
                         prm["cmp_pe_k"][i].reshape(1, -1), prm["cmp_w1_k"][i].astype(bf16),
                         lane_pad(prm["cmp_w2_k"][i]),
                         prm["cmp_pe_v"][i].reshape(1, -1), prm["cmp_w1_v"][i].astype(bf16),
                         lane_pad(prm["cmp_w2_v"][i]))

    bc, bs, bw = bias
    n_sel = seq // SEL_BLOCK
    y_nsa = _nsa(per_batch(q), kcc, vcc, per_batch(ks), per_batch(vs), per_batch(kw), per_batch(vw),
                 per_batch(gts), bc, bs, bw,
                 jnp.asarray(_importance_matrix_t(seq)).astype(bf16),
                 jnp.asarray(_mask_placement(n_sel)).astype(bf16))

    h1, a1 = _out_proj(x2, y_lru, y_nsa.reshape(m, -1), row(prm["gnorm_nsa"][i]),
                       prm["w_out"][i].astype(bf16), row(prm["norm_mix_post"][i]), row(prm["norm_mlp_pre"][i]))
    f = _mlp(a1, prm["mlp_w1"][i].astype(bf16), prm["mlp_w2"][i].astype(bf16), row(prm["norm_mlp_post"][i]))
    h3 = _ple(h1, f, p_i.reshape(m, PLE_DIM), prm["ple_gate"][i].astype(bf16), prm["ple_proj"][i].astype(bf16))
    return h3.reshape(batch, seq, D_MODEL)


def kernel(x, p, norm_mix_pre, norm_mix_post, norm_mlp_pre, norm_mlp_post, w_in, conv_w, conv_b, lru_wa, lru_ba, lru_wx, lru_bx, lru_lambda, cmp_pe_k, cmp_w1_k, cmp_w2_k, cmp_pe_v, cmp_w1_v, cmp_w2_v, rel_bias, gnorm_lru, gnorm_nsa, w_out, mlp_w1, mlp_w2, ple_gate, ple_proj):
    prm = dict(norm_mix_pre=norm_mix_pre, norm_mix_post=norm_mix_post, norm_mlp_pre=norm_mlp_pre,
               norm_mlp_post=norm_mlp_post, w_in=w_in, conv_w=conv_w, conv_b=conv_b, lru_wa=lru_wa,
               lru_ba=lru_ba, lru_wx=lru_wx, lru_bx=lru_bx, lru_lambda=lru_lambda, cmp_pe_k=cmp_pe_k,
               cmp_w1_k=cmp_w1_k, cmp_w2_k=cmp_w2_k, cmp_pe_v=cmp_pe_v, cmp_w1_v=cmp_w1_v,
               cmp_w2_v=cmp_w2_v, gnorm_lru=gnorm_lru, gnorm_nsa=gnorm_nsa, w_out=w_out, mlp_w1=mlp_w1,
               mlp_w2=mlp_w2, ple_gate=ple_gate, ple_proj=ple_proj)
    bias = _bias_tiles(rel_bias, x.shape[1])
    h = x
    for i in range(w_in.shape[0]):
        h = _layer(h, p[i], i, prm, bias)
    return h
```

```python
import functools
import math

import numpy as np
import jax
import jax.numpy as jnp
from jax import lax
from jax.experimental import pallas as pl
from jax.experimental.pallas import tpu as pltpu

D_MODEL = 2048
PLE_DIM = 256
LRU_WIDTH = 1024
LRU_BLOCKS = 16
LRU_BLOCK_DIM = 64
CONV_WIDTH = 4
LRU_C = 8.0
HEAD_DIM = 64
N_HEADS = 16
N_KV = 4
HEADS_PER_KV = 4
CMP_LEN = 32
CMP_STRIDE = 16
CMP_HIDDEN = 256
SEL_BLOCK = 64
N_SELECT = 16
N_LOCAL_FORCED = 2
WINDOW = 512
N_BUCKETS = 32
MAX_DISTANCE = 128
D_FF = 4 * D_MODEL
NORM_EPS = 1e-6
NEG_INF = -1e30
FORCE_SCORE = 1e4
IN_DIM = 4656
IN_PAD = 4736
GATE_LO = 4608

LANES = 128
MASK_LO = HEAD_DIM
Q_TILE = 128
K_TILE = 256
TILE_RATIO = K_TILE // Q_TILE
N_SEL_BIAS = 3
N_WIN_BIAS = (WINDOW + Q_TILE) // Q_TILE + 1
LOG2E = math.log2(math.e)

VMEM_LIMIT = 56 * 1024 * 1024

f32 = jnp.float32
bf16 = jnp.bfloat16


def _rms(x, g):
    return x * lax.rsqrt(jnp.mean(x * x, axis=-1, keepdims=True) + NORM_EPS) * g


def _dot(a, b):
    return jnp.dot(a, b, preferred_element_type=f32)


def _dot_nt(a, b):
    return lax.dot_general(a, b, (((1,), (1,)), ((), ())), preferred_element_type=f32)


def _dot_tn(a, b):
    return lax.dot_general(a, b, (((0,), (0,)), ((), ())), preferred_element_type=f32)


def _const_spec(shape):
    nd = len(shape)
    return pl.BlockSpec(shape, lambda *_: (0,) * nd)


def _params(sem):
    return pltpu.CompilerParams(dimension_semantics=sem, vmem_limit_bytes=VMEM_LIMIT)


def _spread_groups(z, fill):
    lane = lax.broadcasted_iota(jnp.int32, (z.shape[0], LANES), 1)
    parts = []
    for g in range(N_KV):
        pair = z[:, (g // 2) * LANES:(g // 2 + 1) * LANES]
        if g % 2:
            pair = pltpu.roll(pair, HEAD_DIM, 1)
        parts.append(jnp.where(lane < HEAD_DIM, pair, fill))
    return jnp.concatenate(parts, axis=1)


def _value_rows(z):
    tm = z.shape[1]
    ones_row = jnp.where(lax.broadcasted_iota(jnp.int32, (LANES - HEAD_DIM, tm), 0) == 0, 1.0, 0.0)
    parts = []
    for g in range(N_KV):
        parts += [z[g * HEAD_DIM:(g + 1) * HEAD_DIM, :], ones_row]
    return jnp.concatenate(parts, axis=0)


TOK_U, TOK_GL, TOK_KC, TOK_VC, TOK_KS, TOK_KW, TOK_END = 0, 1024, 2048, 2304, 2560, 2816, 3072
FEAT_Q, FEAT_VS, FEAT_VW, FEAT_GT, FEAT_END = 0, 1024, 1280, 1536, 1600
GATE_ROWS = 16


def _in_proj_body(seq, x_ref, g_ref, w_ref, wt_ref, u_ref, gl_ref, kc_ref, vc_ref, ks_ref, kw_ref,
                  qt_ref, vst_ref, vwt_ref, gtt_ref):
    tm = x_ref.shape[0]
    a = _rms(x_ref[...], g_ref[...]).astype(bf16)

    def tok(lo, hi):
        return _dot(a, w_ref[:, lo:hi])

    def feat(lo, hi):
        return _dot_nt(wt_ref[lo:hi, :], a)

    u_ref[...] = tok(TOK_U, TOK_GL)
    gl_ref[...] = tok(TOK_GL, TOK_KC)
    kc_ref[...] = tok(TOK_KC, TOK_VC)
    vc_ref[...] = tok(TOK_VC, TOK_KS)
    lane = lax.broadcasted_iota(jnp.int32, (tm, LANES), 1)
    pos = (pl.program_id(0) * tm) % seq + lax.broadcasted_iota(jnp.int32, (tm, LANES), 0)
    block_onehot = jnp.where(lane - MASK_LO == pos // SEL_BLOCK, 1.0, 0.0)
    ks_ref[...] = _spread_groups(tok(TOK_KS, TOK_KW), block_onehot).astype(bf16)
    kw_ref[...] = _spread_groups(tok(TOK_KW, TOK_END), 0.0).astype(bf16)

    qt_ref[...] = (feat(FEAT_Q, FEAT_VS) * (LOG2E * HEAD_DIM ** -0.5)).astype(bf16)
    vst_ref[...] = _value_rows(feat(FEAT_VS, FEAT_VW)).astype(bf16)
    vwt_ref[...] = _value_rows(feat(FEAT_VW, FEAT_GT)).astype(bf16)
    gtt_ref[...] = feat(FEAT_GT, FEAT_END)


def _in_proj(x2, g, w, wt, seq, tm=512):
    m = x2.shape[0]
    wide = N_KV * LANES
    tok_out = [(1024, f32), (1024, f32), (256, f32), (256, f32), (wide, bf16), (wide, bf16)]
    feat_out = [(N_HEADS * HEAD_DIM, bf16), (wide, bf16), (wide, bf16), (N_KV * GATE_ROWS, f32)]
    body = lambda *refs: _in_proj_body(seq, *refs)
    return pl.pallas_call(
        body,
        grid=(m // tm,),
        in_specs=[pl.BlockSpec((tm, D_MODEL), lambda i: (i, 0)),
                  _const_spec((1, D_MODEL)),
                  pl.BlockSpec((D_MODEL, TOK_END), lambda i: (0, 0), pipeline_mode=pl.Buffered(1)),
                  pl.BlockSpec((FEAT_END, D_MODEL), lambda i: (0, 0), pipeline_mode=pl.Buffered(1))],
        out_specs=([pl.BlockSpec((tm, n), lambda i: (i, 0)) for n, _ in tok_out]
                   + [pl.BlockSpec((n, tm), lambda i: (0, i)) for n, _ in feat_out]),
        out_shape=([jax.ShapeDtypeStruct((m, n), dt) for n, dt in tok_out]
                   + [jax.ShapeDtypeStruct((n, m), dt) for n, dt in feat_out]),
        compiler_params=_params(("parallel",)),
        name="in_proj",
    )(x2, g, w, wt)


def _rglru_body(u_ref, gl_ref, cw_ref, cb_ref, wa_ref, ba_ref, wx_ref, bx_ref, lam_ref, gn_ref,
                o_ref, ubuf, a_sc, b_sc, hc_sc):
    t = u_ref.shape[0]

    @pl.when(pl.program_id(1) == 0)
    def _():
        ubuf[...] = jnp.zeros_like(ubuf)
        hc_sc[...] = jnp.zeros_like(hc_sc)

    u = u_ref[...]
    u3 = u.reshape(t // 8, 8, LRU_WIDTH)
    tail = ubuf[...]
    cw = cw_ref[...]
    row8w = lax.broadcasted_iota(jnp.int32, (t // 8, 8, LRU_WIDTH), 1)
    xc3 = cb_ref[...] + u3 * cw[CONV_WIDTH - 1]
    for d in range(1, CONV_WIDTH):
        cur = pltpu.roll(u3, d, 1)
        prev = jnp.concatenate([pltpu.roll(tail, d, 0)[None], cur[:-1]], axis=0)
        xc3 = xc3 + jnp.where(row8w >= d, cur, prev) * cw[CONV_WIDTH - 1 - d]
    ubuf[...] = u[t - 8:t, :]
    xc = xc3.reshape(t, LRU_WIDTH)

    xb = xc.astype(bf16)
    sp = jax.nn.softplus(-lam_ref[...])
    row8 = lax.broadcasted_iota(jnp.int32, (t // 8, 8, 256), 1)
    for c in range(LRU_WIDTH // 256):
        sl = slice(c * 256, (c + 1) * 256)
        xcb = xb[:, sl]
        r = jax.nn.sigmoid(_dot(xcb, wa_ref[c]) + ba_ref[:, sl])
        ig = jax.nn.sigmoid(_dot(xcb, wx_ref[c]) + bx_ref[:, sl])
        log_a = (-LRU_C) * r * sp[:, sl]
        a = jnp.exp(log_a)
        b = jnp.sqrt(1.0 - a * a) * (ig * xc[:, sl])
        a = a.reshape(t // 8, 8, 256)
        b = b.reshape(t // 8, 8, 256)
        for d in (1, 2, 4):
            keep = row8 >= d
            a_prev = pltpu.roll(a, d, 1)
            b_prev = pltpu.roll(b, d, 1)
            b = jnp.where(keep, a * b_prev + b, b)
            a = jnp.where(keep, a * a_prev, a)
        a_sc[:, sl] = a.reshape(t, 256)
        b_sc[:, sl] = b.reshape(t, 256)

    def group(gi, h):
        off = pl.multiple_of(gi * 8, 8)
        hg = b_sc[pl.ds(off, 8), :] + a_sc[pl.ds(off, 8), :] * h
        b_sc[pl.ds(off, 8), :] = hg
        return hg[7:8, :]

    h_last = lax.fori_loop(0, t // 8, group, hc_sc[0:1, :])
    hc_sc[0:1, :] = h_last
    y = b_sc[...] * jax.nn.gelu(gl_ref[...])
    o_ref[...] = _rms(y, gn_ref[...]).astype(o_ref.dtype)


def _rglru(u, gl, cw, cb, wa, ba, wx, bx, lam, gn, batch, seq, t=256):
    ns = seq // t
    row = pl.BlockSpec((t, LRU_WIDTH), lambda b, s: (b * ns + s, 0))
    vec = _const_spec((1, LRU_WIDTH))
    wspec = _const_spec((LRU_WIDTH // 256, 256, 256))
    return pl.pallas_call(
        _rglru_body,
        grid=(batch, ns),
        in_specs=[row, row, _const_spec((CONV_WIDTH, LRU_WIDTH)), vec, wspec, vec, wspec, vec, vec, vec],
        out_specs=row,
        out_shape=jax.ShapeDtypeStruct((batch * seq, LRU_WIDTH), bf16),
        scratch_shapes=[pltpu.VMEM((8, LRU_WIDTH), f32), pltpu.VMEM((t, LRU_WIDTH), f32),
                        pltpu.VMEM((t, LRU_WIDTH), f32), pltpu.VMEM((8, LRU_WIDTH), f32)],
        compiler_params=_params(("parallel", "arbitrary")),
        name="rglru",
    )(u, gl, cw, cb, wa, ba, wx, bx, lam, gn)


def _compress_body(kc_ref, vc_ref, pek_ref, w1k_ref, w2k_ref, pev_ref, w1v_ref, w2v_ref, ko_ref, vo_ref):
    nrow = ko_ref.shape[2]
    rows = lax.broadcasted_iota(jnp.int32, (nrow, LANES), 0)

    cols = lax.broadcasted_iota(jnp.int32, (LANES, nrow), 1)

    def one(t_ref, pe_ref, w1_ref, w2_ref, o_ref, transposed):
        pe = jnp.broadcast_to(pe_ref[...], (8, CMP_LEN * HEAD_DIM)).astype(bf16)
        c0 = _dot(pe, w1_ref[...])[0:1, :]
        toks = [t_ref[0, pl.ds(j, nrow, stride=CMP_STRIDE), :] for j in range(CMP_STRIDE)]
        for g in range(LANES // HEAD_DIM):
            lo = jnp.zeros((nrow, CMP_HIDDEN), f32)
            hi = jnp.zeros((nrow, CMP_HIDDEN), f32)
            for j in range(CMP_STRIDE):
                tj = toks[j][:, g * HEAD_DIM:(g + 1) * HEAD_DIM].astype(bf16)
                lo = lo + _dot(tj, w1_ref[j * HEAD_DIM:(j + 1) * HEAD_DIM, :])
                hi = hi + _dot(tj, w1_ref[(CMP_STRIDE + j) * HEAD_DIM:(CMP_STRIDE + j + 1) * HEAD_DIM, :])
            hid = jax.nn.gelu(lo + pltpu.roll(hi, nrow - 1, 0) + c0)
            if transposed:
                out = _dot_nt(w2_ref[...], hid.astype(bf16))
                o_ref[0, g] = jnp.where(cols < nrow - 1, out, 0.0).astype(o_ref.dtype)
            else:
                out = _dot(hid.astype(bf16), w2_ref[...])
                o_ref[0, g] = jnp.where(rows < nrow - 1, out, 0.0).astype(o_ref.dtype)

    one(kc_ref, pek_ref, w1k_ref, w2k_ref, ko_ref, False)
    one(vc_ref, pev_ref, w1v_ref, w2v_ref, vo_ref, True)


def _compress(kc, vc, pek, w1k, w2k, pev, w1v, w2v):
    batch, seq, width = kc.shape
    nrow = seq // CMP_STRIDE
    pair = LANES // HEAD_DIM
    tspec = pl.BlockSpec((1, seq, LANES), lambda b, h: (b, 0, h))
    ospec = pl.BlockSpec((1, pair, nrow, LANES), lambda b, h: (b, h, 0, 0))
    wts = [_const_spec((1, CMP_LEN * HEAD_DIM)), _const_spec((CMP_LEN * HEAD_DIM, CMP_HIDDEN))]
    tspec_v = pl.BlockSpec((1, pair, LANES, nrow), lambda b, h: (b, h, 0, 0))
    return pl.pallas_call(
        _compress_body,
        grid=(batch, width // LANES),
        in_specs=([tspec, tspec] + wts + [_const_spec((CMP_HIDDEN, LANES))]
                  + wts + [_const_spec((LANES, CMP_HIDDEN))]),
        out_specs=[ospec, tspec_v],
        out_shape=[jax.ShapeDtypeStruct((batch, N_KV, nrow, LANES), bf16),
                   jax.ShapeDtypeStruct((batch, N_KV, LANES, nrow), bf16)],
        compiler_params=_params(("parallel", "parallel")),
        name="compress",
    )(kc, vc, pek, w1k, w2k, pev, w1v, w2v)


def _bucket_thresholds():
    n = np.arange(0, 4096)
    max_exact = N_BUCKETS // 2
    nf = np.maximum(n, 1).astype(np.float32)
    large = max_exact + (np.log(nf / np.float32(max_exact)) / np.float32(math.log(MAX_DISTANCE / max_exact))
                         * np.float32(N_BUCKETS - max_exact)).astype(np.int32)
    large = np.minimum(large, N_BUCKETS - 1)
    bucket = np.where(n < max_exact, n, large)
    assert np.all(np.diff(bucket) >= 0) and bucket[0] == 0 and bucket[-1] == N_BUCKETS - 1
    return [int(np.argmax(bucket >= k)) for k in range(N_BUCKETS)]


_BUCKET_THR = _bucket_thresholds()
assert _BUCKET_THR[-1] <= N_SEL_BIAS * Q_TILE - K_TILE + 1


def _bias_of_dist(dist, ok, tab_ref, head):
    last = tab_ref[N_BUCKETS - 1, head]
    val = jnp.full(dist.shape, (tab_ref[0, head] - last) * LOG2E, f32)
    for k in range(1, N_BUCKETS - 1):
        val = jnp.where(dist >= _BUCKET_THR[k], (tab_ref[k, head] - last) * LOG2E, val)
    val = jnp.where(dist >= _BUCKET_THR[N_BUCKETS - 1], 0.0, val)
    return jnp.where(ok, val, NEG_INF)


def _bias_body(tab_ref, bc_ref, bs_ref, bw_ref):
    g = pl.program_id(0)
    qt = pl.program_id(1)
    ncmp = bc_ref.shape[2]
    for r in range(HEADS_PER_KV):
        head = g * HEADS_PER_KV + r
        cols = slice(r * Q_TILE, (r + 1) * Q_TILE)
        n = lax.broadcasted_iota(jnp.int32, (ncmp, Q_TILE), 0)
        i = lax.broadcasted_iota(jnp.int32, (ncmp, Q_TILE), 1)
        dist = qt * Q_TILE + i - (n * CMP_STRIDE + CMP_LEN - 1)
        bc_ref[0, 0, :, cols] = _bias_of_dist(dist, (dist >= 0) & (n < ncmp - 1), tab_ref, head)

    @pl.when(qt == 0)
    def _():
        j = lax.broadcasted_iota(jnp.int32, (K_TILE, Q_TILE), 0)
        i = lax.broadcasted_iota(jnp.int32, (K_TILE, Q_TILE), 1)
        for r in range(HEADS_PER_KV):
            head = g * HEADS_PER_KV + r
            cols = slice(r * Q_TILE, (r + 1) * Q_TILE)
            for d in range(N_WIN_BIAS):
                dist = d * Q_TILE + i - j
                if d < N_SEL_BIAS:
                    bs_ref[0, d, :, cols] = _bias_of_dist(dist, dist >= 0, tab_ref, head)
                bw_ref[0, d, :, cols] = _bias_of_dist(dist, (dist >= 0) & (dist < WINDOW), tab_ref, head)


def _bias_tiles(rel_bias, seq):
    nq = seq // Q_TILE
    ncmp = seq // CMP_STRIDE
    rows = HEADS_PER_KV * Q_TILE
    return pl.pallas_call(
        _bias_body,
        grid=(N_KV, nq),
        in_specs=[pl.BlockSpec(memory_space=pltpu.SMEM)],
        out_specs=[pl.BlockSpec((1, 1, ncmp, rows), lambda g, q: (g, q, 0, 0)),
                   pl.BlockSpec((1, N_SEL_BIAS, K_TILE, rows), lambda g, q: (g, 0, 0, 0)),
                   pl.BlockSpec((1, N_WIN_BIAS, K_TILE, rows), lambda g, q: (g, 0, 0, 0))],
        out_shape=[jax.ShapeDtypeStruct((N_KV, nq, ncmp, rows), f32),
                   jax.ShapeDtypeStruct((N_KV, N_SEL_BIAS, K_TILE, rows), f32),
                   jax.ShapeDtypeStruct((N_KV, N_WIN_BIAS, K_TILE, rows), f32)],
        compiler_params=_params(("parallel", "arbitrary")),
        name="bias_tiles",
    )(rel_bias)


def _importance_matrix_t(seq):
    n_cmp = (seq - CMP_LEN) // CMP_STRIDE + 1
    n_sel = seq // SEL_BLOCK
    ratio_sel = SEL_BLOCK // CMP_STRIDE
    ratio_cmp = CMP_LEN // CMP_STRIDE
    jj = np.arange(n_sel)[:, None, None]
    ci = ratio_sel * jj + np.arange(ratio_sel)[None, :, None] - np.arange(ratio_cmp)[None, None, :]
    jb = np.broadcast_to(jj, ci.shape)
    ok = (ci >= 0) & (ci < n_cmp)
    m = np.zeros((n_sel, seq // CMP_STRIDE), np.float32)
    np.add.at(m, (jb[ok], ci[ok]), 1.0)
    return m


def _attend(qt, k_ref, vt_ref, t, first_tile, bias_ref):
    last_tile = t // TILE_RATIO
    lo = first_tile * K_TILE
    hi = (last_tile + 1) * K_TILE
    s = _dot(k_ref[0, lo:hi, :], qt)
    pieces = []
    for kt in range(first_tile, last_tile + 1):
        piece = s[(kt - first_tile) * K_TILE:(kt - first_tile + 1) * K_TILE, :]
        idx = t - kt * TILE_RATIO
        if idx < bias_ref.shape[1]:
            piece = piece + bias_ref[0, idx]
        pieces.append(piece)
    m = jnp.max(functools.reduce(jnp.maximum, pieces), axis=0, keepdims=True)
    p = jnp.concatenate([jnp.exp2(pc - m).astype(bf16) for pc in pieces], axis=0)
    acc = _dot(vt_ref[:, lo:hi], p)
    return acc * (1.0 / acc[MASK_LO:MASK_LO + 1, :])


def _nsa_body(qt_ref, kc_ref, vct_ref, ks_ref, vst_ref, kw_ref, vwt_ref, gtt_ref, bc_ref, bs_ref, bw_ref,
              mt_ref, o_ref):
    n_sel = mt_ref.shape[0]
    n_tiles = qt_ref.shape[1] // Q_TILE
    cols = HEADS_PER_KV * Q_TILE
    lane = lax.broadcasted_iota(jnp.int32, (Q_TILE, LANES), 1)
    head_cols = [slice(r * Q_TILE, (r + 1) * Q_TILE) for r in range(HEADS_PER_KV)]
    kc = kc_ref[0, 0]
    vct = vct_ref[0, 0]
    mt = mt_ref[...]
    jj = lax.broadcasted_iota(jnp.int32, (n_sel, Q_TILE), 0)
    qcol = lax.broadcasted_iota(jnp.int32, (n_sel, Q_TILE), 1)
    zeros_tail = jnp.zeros((LANES - HEAD_DIM, Q_TILE), bf16)
    zeros_rest = jnp.zeros((LANES - HEAD_DIM - n_sel, Q_TILE), bf16)

    def select_stage(t):
        tok = slice(t * Q_TILE, (t + 1) * Q_TILE)
        heads = [qt_ref[r * HEAD_DIM:(r + 1) * HEAD_DIM, tok] for r in range(HEADS_PER_KV)]
        q_plain = jnp.concatenate([jnp.concatenate([hd, zeros_tail], axis=0) for hd in heads], axis=1)

        s = _dot(kc, q_plain) + bc_ref[0, t]
        p = jnp.exp2(s - jnp.max(s, axis=0, keepdims=True))
        norm = 1.0 / jnp.sum(p, axis=0, keepdims=True)
        if (t + 1) * Q_TILE > CMP_LEN - 1 >= t * Q_TILE:
            pos = t * Q_TILE + (lax.broadcasted_iota(jnp.int32, (1, cols), 1) & (Q_TILE - 1))
            norm = jnp.where(pos >= CMP_LEN - 1, norm, 0.0)
        p = p * norm
        o_c = _dot(vct, p.astype(bf16))

        n_causal = ((t + 1) * Q_TILE - 1) // SEL_BLOCK + 1
        if n_causal <= N_SELECT:
            return q_plain, q_plain, o_c

        psum = p[:, head_cols[0]] + p[:, head_cols[1]] + p[:, head_cols[2]] + p[:, head_cols[3]]
        p_hi = psum.astype(bf16)
        p_lo = (psum - p_hi.astype(f32)).astype(bf16)
        imp = _dot(mt, p_hi) + _dot(mt, p_lo)
        dblk = (t * Q_TILE + qcol) // SEL_BLOCK - jj
        forced = (jj == 0) | ((dblk >= 0) & (dblk < N_LOCAL_FORCED))
        imp = jnp.where(forced, FORCE_SCORE, jnp.where(dblk >= 0, imp, -FORCE_SCORE))
        rank = jnp.zeros((n_sel, Q_TILE), f32)
        for i in range(n_causal):
            row = imp[i:i + 1, :]
            ahead = (row > imp) | ((row == imp) & (jj > i))
            rank = rank + jnp.where(ahead, 1.0, 0.0)
        mask_rows = jnp.where(rank < N_SELECT, 0.0, NEG_INF).astype(bf16)
        q_masked = jnp.concatenate(
            [jnp.concatenate([hd, mask_rows, zeros_rest], axis=0) for hd in heads], axis=1)
        return q_plain, q_masked, o_c

    def attend_stage(t, q_plain, q_masked, o_c):
        tok = slice(t * Q_TILE, (t + 1) * Q_TILE)
        o_s = _attend(q_masked, ks_ref, vst_ref, t, 0, bs_ref)
        first_win = max((t - WINDOW // Q_TILE) // TILE_RATIO, 0)
        o_w = _attend(q_plain, kw_ref, vwt_ref, t, first_win, bw_ref)
        gt = jax.nn.sigmoid(gtt_ref[:, tok])
        gate = lambda br: jnp.concatenate([gt[3 * r + br:3 * r + br + 1, :] for r in range(HEADS_PER_KV)], axis=1)
        out_t = gate(0) * o_c + gate(1) * o_s + gate(2) * o_w
        outs = [out_t[:, head_cols[r]].T for r in range(HEADS_PER_KV)]
        pairs = [jnp.where(lane < HEAD_DIM, outs[2 * h], pltpu.roll(outs[2 * h + 1], HEAD_DIM, 1))
                 for h in range(HEADS_PER_KV // 2)]
        o_ref[0, tok, :] = jnp.concatenate(pairs, axis=-1)

    staged = select_stage(0)
    for t in range(n_tiles):
        nxt = select_stage(t + 1) if t + 1 < n_tiles else None
        attend_stage(t, *staged)
        staged = nxt


def _nsa(qt, kc, vct, ks, vst, kw, vwt, gtt, bc, bs, bw, mt):
    batch, seq, _ = ks.shape
    nq = seq // Q_TILE
    cols = HEADS_PER_KV * Q_TILE
    ncmp = kc.shape[2]
    width = HEADS_PER_KV * HEAD_DIM
    k_spec = pl.BlockSpec((1, seq, LANES), lambda b, gi: (b, 0, gi))
    vt_spec = pl.BlockSpec((LANES, seq), lambda b, gi: (gi, b))
    return pl.pallas_call(
        _nsa_body,
        grid=(batch, N_KV),
        in_specs=[pl.BlockSpec((width, seq), lambda b, gi: (gi, b)),
                  pl.BlockSpec((1, 1, ncmp, LANES), lambda b, gi: (b, gi, 0, 0)),
                  pl.BlockSpec((1, 1, LANES, ncmp), lambda b, gi: (b, gi, 0, 0)),
                  k_spec, vt_spec, k_spec, vt_spec,
                  pl.BlockSpec((GATE_ROWS, seq), lambda b, gi: (gi, b)),
                  pl.BlockSpec((1, nq, ncmp, cols), lambda b, gi: (gi, 0, 0, 0)),
                  pl.BlockSpec((1, N_SEL_BIAS, K_TILE, cols), lambda b, gi: (gi, 0, 0, 0)),
                  pl.BlockSpec((1, N_WIN_BIAS, K_TILE, cols), lambda b, gi: (gi, 0, 0, 0)),
                  _const_spec(mt.shape)],
        out_specs=pl.BlockSpec((1, seq, width), lambda b, gi: (b, 0, gi)),
        out_shape=jax.ShapeDtypeStruct((batch, seq, N_KV * width), f32),
        compiler_params=_params(("parallel", "arbitrary")),
        name="nsa",
    )(qt, kc, vct, ks, vst, kw, vwt, gtt, bc, bs, bw, mt)


def _out_proj_body(x_ref, yl_ref, yn_ref, gn_ref, w_ref, gp_ref, gm_ref, o_ref, a_ref):
    yn = _rms(yn_ref[...], gn_ref[...]).astype(bf16)
    y = _dot(yl_ref[...], w_ref[0:LRU_WIDTH, :]) + _dot(yn, w_ref[LRU_WIDTH:, :])
    h = x_ref[...] + _rms(y, gp_ref[...])
    o_ref[...] = h
    a_ref[...] = _rms(h, gm_ref[...]).astype(bf16)


def _out_proj(x2, yl, yn, gn, w, gp, gm, tm=512):
    m = x2.shape[0]
    row = pl.BlockSpec((tm, D_MODEL), lambda i: (i, 0))
    return pl.pallas_call(
        _out_proj_body,
        grid=(m // tm,),
        in_specs=[row,
                  pl.BlockSpec((tm, LRU_WIDTH), lambda i: (i, 0)),
                  pl.BlockSpec((tm, D_MODEL - LRU_WIDTH), lambda i: (i, 0)),
                  _const_spec((1, D_MODEL - LRU_WIDTH)),
                  _const_spec((D_MODEL, D_MODEL)),
                  _const_spec((1, D_MODEL)), _const_spec((1, D_MODEL))],
        out_specs=[row, row],
        out_shape=[jax.ShapeDtypeStruct((m, D_MODEL), f32), jax.ShapeDtypeStruct((m, D_MODEL), bf16)],
        compiler_params=_params(("parallel",)),
        name="out_proj",
    )(x2, yl, yn, gn, w, gp, gm)


def _mlp_body(a_ref, w1_ref, w2_ref, g2_ref, o_ref, acc_sc):
    j = pl.program_id(1)

    @pl.when(j == 0)
    def _():
        acc_sc[...] = jnp.zeros_like(acc_sc)

    hid = jnp.maximum(_dot(a_ref[...], w1_ref[...]), 0.0)
    acc_sc[...] += _dot((hid * hid).astype(bf16), w2_ref[...])

    @pl.when(j == pl.num_programs(1) - 1)
    def _():
        o_ref[...] = _rms(acc_sc[...], g2_ref[...])


def _mlp(a, w1, w2, g2, tm=1024, tf=512):
    m = a.shape[0]
    return pl.pallas_call(
        _mlp_body,
        grid=(m // tm, D_FF // tf),
        in_specs=[pl.BlockSpec((tm, D_MODEL), lambda i, j: (i, 0)),
                  pl.BlockSpec((D_MODEL, tf), lambda i, j: (0, j)),
                  pl.BlockSpec((tf, D_MODEL), lambda i, j: (j, 0)),
                  _const_spec((1, D_MODEL))],
        out_specs=pl.BlockSpec((tm, D_MODEL), lambda i, j: (i, 0)),
        out_shape=jax.ShapeDtypeStruct((m, D_MODEL), f32),
        scratch_shapes=[pltpu.VMEM((tm, D_MODEL), f32)],
        compiler_params=_params(("parallel", "arbitrary")),
        name="mlp",
    )(a, w1, w2, g2)


def _ple_body(h_ref, f_ref, p_ref, wg_ref, wp_ref, o_ref):
    h = h_ref[...] + f_ref[...]
    gate = jax.nn.sigmoid(_dot(h.astype(bf16), wg_ref[...]))
    o_ref[...] = h + gate * _dot(p_ref[...].astype(bf16), wp_ref[...])


def _ple(h, f, p2, wg, wp, tm=512):
    m = h.shape[0]
    row = pl.BlockSpec((tm, D_MODEL), lambda i: (i, 0))
    return pl.pallas_call(
        _ple_body,
        grid=(m // tm,),
        in_specs=[row, row,
                  pl.BlockSpec((tm, PLE_DIM), lambda i: (i, 0)),
                  _const_spec((D_MODEL, D_MODEL)),
                  _const_spec((PLE_DIM, D_MODEL))],
        out_specs=row,
        out_shape=jax.ShapeDtypeStruct((m, D_MODEL), f32),
        compiler_params=_params(("parallel",)),
        name="ple",
    )(h, f, p2, wg, wp)


def _block_diag_chunks(w):
    per = 256 // LRU_BLOCK_DIM
    w = w.reshape(LRU_BLOCKS // per, per, LRU_BLOCK_DIM, LRU_BLOCK_DIM)
    eye = jnp.eye(per, dtype=w.dtype)
    return jnp.einsum('cpij,pq->cpiqj', w, eye).reshape(LRU_BLOCKS // per, 256, 256)


def _layer(h, p_i, i, prm, bias):
    batch, seq, _ = h.shape
    m = batch * seq
    x2 = h.reshape(m, D_MODEL)
    row = lambda v: v.reshape(1, -1)
    per_batch = lambda t: t.reshape(batch, seq, t.shape[-1])

    w_in = prm["w_in"][i].astype(bf16)
    w_tok = jnp.concatenate([w_in[:, 0:2048], w_in[:, 3072:3840], w_in[:, 4096:4352]], axis=1)
    per = 3 * HEADS_PER_KV
    w_gate = jnp.pad(w_in[:, GATE_LO:IN_DIM].reshape(D_MODEL, N_KV, per), ((0, 0), (0, 0), (0, GATE_ROWS - per)))
    w_feat = jnp.concatenate([w_in[:, 2048:3072], w_in[:, 3840:4096], w_in[:, 4352:4608],
                              w_gate.reshape(D_MODEL, N_KV * GATE_ROWS)], axis=1).T
    u, gl, kc, vc, ks, kw, qt, vst, vwt, gtt = _in_proj(x2, row(prm["norm_mix_pre"][i]), w_tok, w_feat, seq)

    y_lru = _rglru(u, gl, prm["conv_w"][i], row(prm["conv_b"][i]),
                   _block_diag_chunks(prm["lru_wa"][i]).astype(bf16), row(prm["lru_ba"][i]),
                   _block_diag_chunks(prm["lru_wx"][i]).astype(bf16), row(prm["lru_bx"][i]),
                   row(prm["lru_lambda"][i]), row(prm["gnorm_lru"][i]), batch, seq)

    lane_pad = lambda w: jnp.pad(w, ((0, 0), (0, LANES - HEAD_DIM))).astype(bf16)
    kcc, vcct = _compress(per_batch(kc), per_batch(vc),
                          prm["cmp_pe_k"][i].reshape(1, -1), prm["cmp_w1_k"][i].astype(bf16),
                          lane_pad(prm["cmp_w2_k"][i]),
                          prm["cmp_pe_v"][i].reshape(1, -1), prm["cmp_w1_v"][i].astype(bf16),
                          lane_pad(prm["cmp_w2_v"][i]).T)

    bc, bs, bw = bias
    y_nsa = _nsa(qt, kcc, vcct, per_batch(ks), vst, per_batch(kw), vwt, gtt, bc, bs, bw,
                 jnp.asarray(_importance_matrix_t(seq)).astype(bf16))

    h1, a1 = _out_proj(x2, y_lru, y_nsa.reshape(m, -1), row(prm["gnorm_nsa"][i]),
                       prm["w_out"][i].astype(bf16), row(prm["norm_mix_post"][i]), row(prm["norm_mlp_pre"][i]))
    f = _mlp(a1, prm["mlp_w1"][i].astype(bf16), prm["mlp_w2"][i].astype(bf16), row(prm["norm_mlp_post"][i]))
    h3 = _ple(h1, f, p_i.reshape(m, PLE_DIM), prm["ple_gate"][i].astype(bf16), prm["ple_proj"][i].astype(bf16))
    return h3.reshape(batch, seq, D_MODEL)


def kernel(x, p, norm_mix_pre, norm_mix_post, norm_mlp_pre, norm_mlp_post, w_in, conv_w, conv_b, lru_wa, lru_ba, lru_wx, lru_bx, lru_lambda, cmp_pe_k, cmp_w1_k, cmp_w2_k, cmp_pe_v, cmp_w1_v, cmp_w2_v, rel_bias, gnorm_lru, gnorm_nsa, w_out, mlp_w1, mlp_w2, ple_gate, ple_proj):
    prm = dict(norm_mix_pre=norm_mix_pre, norm_mix_post=norm_mix_post, norm_mlp_pre=norm_mlp_pre,
               norm_mlp_post=norm_mlp_post, w_in=w_in, conv_w=conv_w, conv_b=conv_b, lru_wa=lru_wa,
               lru_ba=lru_ba, lru_wx=lru_wx, lru_bx=lru_bx, lru_lambda=lru_lambda, cmp_pe_k=cmp_pe_k,
               cmp_w1_k=cmp_w1_k, cmp_w2_k=cmp_w2_k, cmp_pe_v=cmp_pe_v, cmp_w1_v=cmp_w1_v,
               cmp_w2_v=cmp_w2_v, gnorm_lru=gnorm_lru, gnorm_nsa=gnorm_nsa, w_out=w_out, mlp_w1=mlp_w1,
               mlp_w2=mlp_w2, ple_gate=ple_gate, ple_proj=ple_proj)
    bias = _bias_tiles(rel_bias, x.shape[1])
    h = x
    for i in range(w_in.shape[0]):
        h = _layer(h, p[i], i, prm, bias)
    return h
```

```python
import functools
import math

import numpy as np
import jax
import jax.numpy as jnp
from jax import lax
from jax.experimental import pallas as pl
from jax.experimental.pallas import tpu as pltpu

D_MODEL = 2048
PLE_DIM = 256
LRU_WIDTH = 1024
LRU_BLOCKS = 16
LRU_BLOCK_DIM = 64
CONV_WIDTH = 4
LRU_C = 8.0
HEAD_DIM = 64
N_HEADS = 16
N_KV = 4
HEADS_PER_KV = 4
CMP_LEN = 32
CMP_STRIDE = 16
CMP_HIDDEN = 256
SEL_BLOCK = 64
N_SELECT = 16
N_LOCAL_FORCED = 2
WINDOW = 512
N_BUCKETS = 32
MAX_DISTANCE = 128
D_FF = 4 * D_MODEL
NORM_EPS = 1e-6
NEG_INF = -1e30
FORCE_SCORE = 1e4
IN_DIM = 4656
IN_PAD = 4736
GATE_LO = 4608

LANES = 128
MASK_LO = HEAD_DIM
Q_TILE = 128
N_SEL_BIAS = 2
V_ROWS = 80
LOG2E = math.log2(math.e)

VMEM_LIMIT = 56 * 1024 * 1024

f32 = jnp.float32
bf16 = jnp.bfloat16


def _rms(x, g):
    return x * lax.rsqrt(jnp.mean(x * x, axis=-1, keepdims=True) + NORM_EPS) * g


def _dot(a, b):
    return jnp.dot(a, b, preferred_element_type=f32)


def _dot_nt(a, b):
    return lax.dot_general(a, b, (((1,), (1,)), ((), ())), preferred_element_type=f32)


def _dot_tn(a, b):
    return lax.dot_general(a, b, (((0,), (0,)), ((), ())), preferred_element_type=f32)


def _const_spec(shape):
    nd = len(shape)
    return pl.BlockSpec(shape, lambda *_: (0,) * nd)


def _params(sem):
    return pltpu.CompilerParams(dimension_semantics=sem, vmem_limit_bytes=VMEM_LIMIT)


def _spread_groups(z, fill):
    lane = lax.broadcasted_iota(jnp.int32, (z.shape[0], LANES), 1)
    parts = []
    for g in range(N_KV):
        pair = z[:, (g // 2) * LANES:(g // 2 + 1) * LANES]
        if g % 2:
            pair = pltpu.roll(pair, HEAD_DIM, 1)
        parts.append(jnp.where(lane < HEAD_DIM, pair, fill))
    return jnp.concatenate(parts, axis=1)


def _value_rows(z):
    tm = z.shape[1]
    ones_row = jnp.where(lax.broadcasted_iota(jnp.int32, (LANES - HEAD_DIM, tm), 0) == 0, 1.0, 0.0)
    parts = []
    for g in range(N_KV):
        parts += [z[g * HEAD_DIM:(g + 1) * HEAD_DIM, :], ones_row]
    return jnp.concatenate(parts, axis=0)


TOK_U, TOK_GL, TOK_KC, TOK_VC, TOK_KS, TOK_KW, TOK_END = 0, 1024, 2048, 2304, 2560, 2816, 3072
FEAT_Q, FEAT_VS, FEAT_VW, FEAT_GT, FEAT_END = 0, 1024, 1280, 1536, 1600
GATE_ROWS = 16


def _in_proj_body(seq, x_ref, g_ref, w_ref, wt_ref, u_ref, gl_ref, kc_ref, vc_ref, ks_ref, kw_ref,
                  qt_ref, vst_ref, vwt_ref, gtt_ref):
    tm = x_ref.shape[0]
    a = _rms(x_ref[...], g_ref[...]).astype(bf16)

    def tok(lo, hi):
        return _dot(a, w_ref[:, lo:hi])

    def feat(lo, hi):
        return _dot_nt(wt_ref[lo:hi, :], a)

    u_ref[...] = tok(TOK_U, TOK_GL)
    gl_ref[...] = tok(TOK_GL, TOK_KC)
    kc_ref[...] = tok(TOK_KC, TOK_VC)
    vc_ref[...] = tok(TOK_VC, TOK_KS)
    lane = lax.broadcasted_iota(jnp.int32, (tm, LANES), 1)
    pos = (pl.program_id(0) * tm) % seq + lax.broadcasted_iota(jnp.int32, (tm, LANES), 0)
    block_onehot = jnp.where(lane - MASK_LO == pos // SEL_BLOCK, 1.0, 0.0)
    ks_ref[...] = _spread_groups(tok(TOK_KS, TOK_KW), block_onehot).astype(bf16)
    kw_ref[...] = _spread_groups(tok(TOK_KW, TOK_END), 0.0).astype(bf16)

    qt_ref[...] = (feat(FEAT_Q, FEAT_VS) * (LOG2E * HEAD_DIM ** -0.5)).astype(bf16)
    vst_ref[...] = _value_rows(feat(FEAT_VS, FEAT_VW)).astype(bf16)
    vwt_ref[...] = _value_rows(feat(FEAT_VW, FEAT_GT)).astype(bf16)
    gtt_ref[...] = feat(FEAT_GT, FEAT_END)


def _in_proj(x2, g, w, wt, seq, tm=512):
    m = x2.shape[0]
    wide = N_KV * LANES
    tok_out = [(1024, f32), (1024, f32), (256, f32), (256, f32), (wide, bf16), (wide, bf16)]
    feat_out = [(N_HEADS * HEAD_DIM, bf16), (wide, bf16), (wide, bf16), (N_KV * GATE_ROWS, f32)]
    body = lambda *refs: _in_proj_body(seq, *refs)
    return pl.pallas_call(
        body,
        grid=(m // tm,),
        in_specs=[pl.BlockSpec((tm, D_MODEL), lambda i: (i, 0)),
                  _const_spec((1, D_MODEL)),
                  pl.BlockSpec((D_MODEL, TOK_END), lambda i: (0, 0), pipeline_mode=pl.Buffered(1)),
                  pl.BlockSpec((FEAT_END, D_MODEL), lambda i: (0, 0), pipeline_mode=pl.Buffered(1))],
        out_specs=([pl.BlockSpec((tm, n), lambda i: (i, 0)) for n, _ in tok_out]
                   + [pl.BlockSpec((n, tm), lambda i: (0, i)) for n, _ in feat_out]),
        out_shape=([jax.ShapeDtypeStruct((m, n), dt) for n, dt in tok_out]
                   + [jax.ShapeDtypeStruct((n, m), dt) for n, dt in feat_out]),
        compiler_params=_params(("parallel",)),
        name="in_proj",
    )(x2, g, w, wt)


def _rglru_body(u_ref, gl_ref, cw_ref, cb_ref, wa_ref, ba_ref, wx_ref, bx_ref, lam_ref, gn_ref,
                o_ref, ubuf, a_sc, b_sc, hc_sc):
    t = u_ref.shape[0]

    @pl.when(pl.program_id(1) == 0)
    def _():
        ubuf[...] = jnp.zeros_like(ubuf)
        hc_sc[...] = jnp.zeros_like(hc_sc)

    u = u_ref[...]
    u3 = u.reshape(t // 8, 8, LRU_WIDTH)
    tail = ubuf[...]
    cw = cw_ref[...]
    row8w = lax.broadcasted_iota(jnp.int32, (t // 8, 8, LRU_WIDTH), 1)
    xc3 = cb_ref[...] + u3 * cw[CONV_WIDTH - 1]
    for d in range(1, CONV_WIDTH):
        cur = pltpu.roll(u3, d, 1)
        prev = jnp.concatenate([pltpu.roll(tail, d, 0)[None], cur[:-1]], axis=0)
        xc3 = xc3 + jnp.where(row8w >= d, cur, prev) * cw[CONV_WIDTH - 1 - d]
    ubuf[...] = u[t - 8:t, :]
    xc = xc3.reshape(t, LRU_WIDTH)

    xb = xc.astype(bf16)
    sp = jax.nn.softplus(-lam_ref[...])
    row8 = lax.broadcasted_iota(jnp.int32, (t // 8, 8, 256), 1)
    for c in range(LRU_WIDTH // 256):
        sl = slice(c * 256, (c + 1) * 256)
        xcb = xb[:, sl]
        r = jax.nn.sigmoid(_dot(xcb, wa_ref[c]) + ba_ref[:, sl])
        ig = jax.nn.sigmoid(_dot(xcb, wx_ref[c]) + bx_ref[:, sl])
        log_a = (-LRU_C) * r * sp[:, sl]
        a = jnp.exp(log_a)
        b = jnp.sqrt(1.0 - a * a) * (ig * xc[:, sl])
        a = a.reshape(t // 8, 8, 256)
        b = b.reshape(t // 8, 8, 256)
        for d in (1, 2, 4):
            keep = row8 >= d
            a_prev = pltpu.roll(a, d, 1)
            b_prev = pltpu.roll(b, d, 1)
            b = jnp.where(keep, a * b_prev + b, b)
            a = jnp.where(keep, a * a_prev, a)
        a_sc[:, sl] = a.reshape(t, 256)
        b_sc[:, sl] = b.reshape(t, 256)

    def group(gi, h):
        off = pl.multiple_of(gi * 8, 8)
        hg = b_sc[pl.ds(off, 8), :] + a_sc[pl.ds(off, 8), :] * h
        b_sc[pl.ds(off, 8), :] = hg
        return hg[7:8, :]

    h_last = lax.fori_loop(0, t // 8, group, hc_sc[0:1, :])
    hc_sc[0:1, :] = h_last
    y = b_sc[...] * jax.nn.gelu(gl_ref[...])
    o_ref[...] = _rms(y, gn_ref[...]).astype(o_ref.dtype)


def _rglru(u, gl, cw, cb, wa, ba, wx, bx, lam, gn, batch, seq, t=256):
    ns = seq // t
    row = pl.BlockSpec((t, LRU_WIDTH), lambda b, s: (b * ns + s, 0))
    vec = _const_spec((1, LRU_WIDTH))
    wspec = _const_spec((LRU_WIDTH // 256, 256, 256))
    return pl.pallas_call(
        _rglru_body,
        grid=(batch, ns),
        in_specs=[row, row, _const_spec((CONV_WIDTH, LRU_WIDTH)), vec, wspec, vec, wspec, vec, vec, vec],
        out_specs=row,
        out_shape=jax.ShapeDtypeStruct((batch * seq, LRU_WIDTH), bf16),
        scratch_shapes=[pltpu.VMEM((8, LRU_WIDTH), f32), pltpu.VMEM((t, LRU_WIDTH), f32),
                        pltpu.VMEM((t, LRU_WIDTH), f32), pltpu.VMEM((8, LRU_WIDTH), f32)],
        compiler_params=_params(("parallel", "arbitrary")),
        name="rglru",
    )(u, gl, cw, cb, wa, ba, wx, bx, lam, gn)


def _compress_body(kc_ref, vc_ref, pek_ref, w1k_ref, w2k_ref, pev_ref, w1v_ref, w2v_ref, ko_ref, vo_ref):
    nrow = ko_ref.shape[2]
    rows = lax.broadcasted_iota(jnp.int32, (nrow, LANES), 0)

    cols = lax.broadcasted_iota(jnp.int32, (LANES, nrow), 1)

    def one(t_ref, pe_ref, w1_ref, w2_ref, o_ref, transposed):
        pe = jnp.broadcast_to(pe_ref[...], (8, CMP_LEN * HEAD_DIM)).astype(bf16)
        c0 = _dot(pe, w1_ref[...])[0:1, :]
        toks = [t_ref[0, pl.ds(j, nrow, stride=CMP_STRIDE), :] for j in range(CMP_STRIDE)]
        for g in range(LANES // HEAD_DIM):
            lo = jnp.zeros((nrow, CMP_HIDDEN), f32)
            hi = jnp.zeros((nrow, CMP_HIDDEN), f32)
            for j in range(CMP_STRIDE):
                tj = toks[j][:, g * HEAD_DIM:(g + 1) * HEAD_DIM].astype(bf16)
                lo = lo + _dot(tj, w1_ref[j * HEAD_DIM:(j + 1) * HEAD_DIM, :])
                hi = hi + _dot(tj, w1_ref[(CMP_STRIDE + j) * HEAD_DIM:(CMP_STRIDE + j + 1) * HEAD_DIM, :])
            hid = jax.nn.gelu(lo + pltpu.roll(hi, nrow - 1, 0) + c0)
            if transposed:
                out = _dot_nt(w2_ref[...], hid.astype(bf16))
                o_ref[0, g] = jnp.where(cols < nrow - 1, out, 0.0).astype(o_ref.dtype)
            else:
                out = _dot(hid.astype(bf16), w2_ref[...])
                o_ref[0, g] = jnp.where(rows < nrow - 1, out, 0.0).astype(o_ref.dtype)

    one(kc_ref, pek_ref, w1k_ref, w2k_ref, ko_ref, False)
    one(vc_ref, pev_ref, w1v_ref, w2v_ref, vo_ref, True)


def _compress(kc, vc, pek, w1k, w2k, pev, w1v, w2v):
    batch, seq, width = kc.shape
    nrow = seq // CMP_STRIDE
    pair = LANES // HEAD_DIM
    tspec = pl.BlockSpec((1, seq, LANES), lambda b, h: (b, 0, h))
    ospec = pl.BlockSpec((1, pair, nrow, LANES), lambda b, h: (b, h, 0, 0))
    wts = [_const_spec((1, CMP_LEN * HEAD_DIM)), _const_spec((CMP_LEN * HEAD_DIM, CMP_HIDDEN))]
    tspec_v = pl.BlockSpec((1, pair, LANES, nrow), lambda b, h: (b, h, 0, 0))
    return pl.pallas_call(
        _compress_body,
        grid=(batch, width // LANES),
        in_specs=([tspec, tspec] + wts + [_const_spec((CMP_HIDDEN, LANES))]
                  + wts + [_const_spec((LANES, CMP_HIDDEN))]),
        out_specs=[ospec, tspec_v],
        out_shape=[jax.ShapeDtypeStruct((batch, N_KV, nrow, LANES), bf16),
                   jax.ShapeDtypeStruct((batch, N_KV, LANES, nrow), bf16)],
        compiler_params=_params(("parallel", "parallel")),
        name="compress",
    )(kc, vc, pek, w1k, w2k, pev, w1v, w2v)


def _bucket_thresholds():
    n = np.arange(0, 4096)
    max_exact = N_BUCKETS // 2
    nf = np.maximum(n, 1).astype(np.float32)
    large = max_exact + (np.log(nf / np.float32(max_exact)) / np.float32(math.log(MAX_DISTANCE / max_exact))
                         * np.float32(N_BUCKETS - max_exact)).astype(np.int32)
    large = np.minimum(large, N_BUCKETS - 1)
    bucket = np.where(n < max_exact, n, large)
    assert np.all(np.diff(bucket) >= 0) and bucket[0] == 0 and bucket[-1] == N_BUCKETS - 1
    return [int(np.argmax(bucket >= k)) for k in range(N_BUCKETS)]


_BUCKET_THR = _bucket_thresholds()
assert _BUCKET_THR[-1] <= (N_SEL_BIAS - 1) * Q_TILE + 1


def _bias_of_dist(dist, ok, tab_ref, head):
    last = tab_ref[N_BUCKETS - 1, head]
    val = jnp.full(dist.shape, (tab_ref[0, head] - last) * LOG2E, f32)
    for k in range(1, N_BUCKETS - 1):
        val = jnp.where(dist >= _BUCKET_THR[k], (tab_ref[k, head] - last) * LOG2E, val)
    val = jnp.where(dist >= _BUCKET_THR[N_BUCKETS - 1], 0.0, val)
    return jnp.where(ok, val, NEG_INF)


def _bias_body(tab_ref, bc_ref, bs_ref, bw_ref):
    g = pl.program_id(0)
    qt = pl.program_id(1)
    ncmp = bc_ref.shape[2]
    for r in range(HEADS_PER_KV):
        head = g * HEADS_PER_KV + r
        cols = slice(r * Q_TILE, (r + 1) * Q_TILE)
        n = lax.broadcasted_iota(jnp.int32, (ncmp, Q_TILE), 0)
        i = lax.broadcasted_iota(jnp.int32, (ncmp, Q_TILE), 1)
        dist = qt * Q_TILE + i - (n * CMP_STRIDE + CMP_LEN - 1)
        bc_ref[0, 0, :, cols] = _bias_of_dist(dist, (dist >= 0) & (n < ncmp - 1), tab_ref, head)

    @pl.when(qt == 0)
    def _():
        j = lax.broadcasted_iota(jnp.int32, (Q_TILE, Q_TILE), 0)
        i = lax.broadcasted_iota(jnp.int32, (Q_TILE, Q_TILE), 1)
        for r in range(HEADS_PER_KV):
            head = g * HEADS_PER_KV + r
            cols = slice(r * Q_TILE, (r + 1) * Q_TILE)
            for d in range(N_SEL_BIAS):
                dist = d * Q_TILE + i - j
                bs_ref[0, d, :, cols] = _bias_of_dist(dist, dist >= 0, tab_ref, head)
            dist = WINDOW + i - j
            bw_ref[0, 0, :, cols] = _bias_of_dist(dist, dist < WINDOW, tab_ref, head)


def _bias_tiles(rel_bias, seq):
    nq = seq // Q_TILE
    ncmp = seq // CMP_STRIDE
    rows = HEADS_PER_KV * Q_TILE
    return pl.pallas_call(
        _bias_body,
        grid=(N_KV, nq),
        in_specs=[pl.BlockSpec(memory_space=pltpu.SMEM)],
        out_specs=[pl.BlockSpec((1, 1, ncmp, rows), lambda g, q: (g, q, 0, 0)),
                   pl.BlockSpec((1, N_SEL_BIAS, Q_TILE, rows), lambda g, q: (g, 0, 0, 0)),
                   pl.BlockSpec((1, 1, Q_TILE, rows), lambda g, q: (g, 0, 0, 0))],
        out_shape=[jax.ShapeDtypeStruct((N_KV, nq, ncmp, rows), f32),
                   jax.ShapeDtypeStruct((N_KV, N_SEL_BIAS, Q_TILE, rows), f32),
                   jax.ShapeDtypeStruct((N_KV, 1, Q_TILE, rows), f32)],
        compiler_params=_params(("parallel", "arbitrary")),
        name="bias_tiles",
    )(rel_bias)


def _importance_matrix_t(seq):
    n_cmp = (seq - CMP_LEN) // CMP_STRIDE + 1
    n_sel = seq // SEL_BLOCK
    ratio_sel = SEL_BLOCK // CMP_STRIDE
    ratio_cmp = CMP_LEN // CMP_STRIDE
    jj = np.arange(n_sel)[:, None, None]
    ci = ratio_sel * jj + np.arange(ratio_sel)[None, :, None] - np.arange(ratio_cmp)[None, None, :]
    jb = np.broadcast_to(jj, ci.shape)
    ok = (ci >= 0) & (ci < n_cmp)
    m = np.zeros((n_sel, seq // CMP_STRIDE), np.float32)
    np.add.at(m, (jb[ok], ci[ok]), 1.0)
    return m


def _attend(qt, k_ref, vt_ref, t, first_tile, biases):
    lo = first_tile * Q_TILE
    hi = (t + 1) * Q_TILE
    s = _dot(k_ref[0, lo:hi, :], qt)
    pieces = []
    for kt in range(first_tile, t + 1):
        piece = s[(kt - first_tile) * Q_TILE:(kt - first_tile + 1) * Q_TILE, :]
        if t - kt in biases:
            piece = piece + biases[t - kt]
        pieces.append(piece)
    m = jnp.max(functools.reduce(jnp.maximum, pieces), axis=0, keepdims=True)
    p = jnp.concatenate([jnp.exp2(pc - m).astype(bf16) for pc in pieces], axis=0)
    acc = _dot(vt_ref[0:V_ROWS, lo:hi], p)
    return acc[0:HEAD_DIM, :], 1.0 / acc[MASK_LO:MASK_LO + 1, :]


def _nsa_body(qt_ref, kc_ref, vct_ref, ks_ref, vst_ref, kw_ref, vwt_ref, gtt_ref, bc_ref, bs_ref, bw_ref,
              mt_ref, o_ref):
    n_sel = mt_ref.shape[0]
    n_tiles = qt_ref.shape[1] // Q_TILE
    cols = HEADS_PER_KV * Q_TILE
    head_cols = [slice(r * Q_TILE, (r + 1) * Q_TILE) for r in range(HEADS_PER_KV)]
    kc = kc_ref[0, 0]
    vct = vct_ref[0, 0]
    mt = mt_ref[...]
    jj = lax.broadcasted_iota(jnp.int32, (n_sel, Q_TILE), 0)
    qcol = lax.broadcasted_iota(jnp.int32, (n_sel, Q_TILE), 1)
    zeros_tail = jnp.zeros((LANES - HEAD_DIM, Q_TILE), bf16)
    zeros_rest = jnp.zeros((LANES - HEAD_DIM - n_sel, Q_TILE), bf16)

    def select_stage(t):
        tok = slice(t * Q_TILE, (t + 1) * Q_TILE)
        heads = [qt_ref[r * HEAD_DIM:(r + 1) * HEAD_DIM, tok] for r in range(HEADS_PER_KV)]
        q_plain = jnp.concatenate([jnp.concatenate([hd, zeros_tail], axis=0) for hd in heads], axis=1)

        s = _dot(kc, q_plain) + bc_ref[0, t]
        p = jnp.exp2(s - jnp.max(s, axis=0, keepdims=True))
        norm = 1.0 / jnp.sum(p, axis=0, keepdims=True)
        if (t + 1) * Q_TILE > CMP_LEN - 1 >= t * Q_TILE:
            pos = t * Q_TILE + (lax.broadcasted_iota(jnp.int32, (1, cols), 1) & (Q_TILE - 1))
            norm = jnp.where(pos >= CMP_LEN - 1, norm, 0.0)
        p = p * norm
        o_c = _dot(vct[0:HEAD_DIM, :], p.astype(bf16))

        n_causal = ((t + 1) * Q_TILE - 1) // SEL_BLOCK + 1
        if n_causal <= N_SELECT:
            return q_plain, q_plain, o_c

        psum = p[:, head_cols[0]] + p[:, head_cols[1]] + p[:, head_cols[2]] + p[:, head_cols[3]]
        p_hi = psum.astype(bf16)
        p_lo = (psum - p_hi.astype(f32)).astype(bf16)
        imp = _dot(mt, p_hi) + _dot(mt, p_lo)
        dblk = (t * Q_TILE + qcol) // SEL_BLOCK - jj
        forced = (jj == 0) | ((dblk >= 0) & (dblk < N_LOCAL_FORCED))
        imp = jnp.where(forced, FORCE_SCORE, jnp.where(dblk >= 0, imp, -FORCE_SCORE))
        rank = jnp.zeros((n_sel, Q_TILE), f32)
        for i in range(n_causal):
            row = imp[i:i + 1, :]
            ahead = (row > imp) | ((row == imp) & (jj > i))
            rank = rank + jnp.where(ahead, 1.0, 0.0)
        mask_rows = jnp.where(rank < N_SELECT, 0.0, NEG_INF).astype(bf16)
        q_masked = jnp.concatenate(
            [jnp.concatenate([hd, mask_rows, zeros_rest], axis=0) for hd in heads], axis=1)
        return q_plain, q_masked, o_c

    def attend_stage(t, q_plain, q_masked, o_c):
        tok = slice(t * Q_TILE, (t + 1) * Q_TILE)
        near = {d: bs_ref[0, d] for d in range(N_SEL_BIAS)}
        acc_s, inv_s = _attend(q_masked, ks_ref, vst_ref, t, 0, near)
        band = dict(near)
        band[WINDOW // Q_TILE] = bw_ref[0, 0]
        acc_w, inv_w = _attend(q_plain, kw_ref, vwt_ref, t, max(t - WINDOW // Q_TILE, 0), band)
        gt = jax.nn.sigmoid(gtt_ref[:, tok])
        gate = lambda br: jnp.concatenate([gt[3 * r + br:3 * r + br + 1, :] for r in range(HEADS_PER_KV)], axis=1)
        out_t = gate(0) * o_c + (gate(1) * inv_s) * acc_s + (gate(2) * inv_w) * acc_w
        pairs = [jnp.concatenate([out_t[:, head_cols[2 * h]], out_t[:, head_cols[2 * h + 1]]], axis=0).T
                 for h in range(HEADS_PER_KV // 2)]
        o_ref[0, tok, :] = jnp.concatenate(pairs, axis=-1)

    staged = select_stage(0)
    for t in range(n_tiles):
        nxt = select_stage(t + 1) if t + 1 < n_tiles else None
        attend_stage(t, *staged)
        staged = nxt


def _nsa(qt, kc, vct, ks, vst, kw, vwt, gtt, bc, bs, bw, mt):
    batch, seq, _ = ks.shape
    nq = seq // Q_TILE
    cols = HEADS_PER_KV * Q_TILE
    ncmp = kc.shape[2]
    width = HEADS_PER_KV * HEAD_DIM
    k_spec = pl.BlockSpec((1, seq, LANES), lambda b, gi: (b, 0, gi))
    vt_spec = pl.BlockSpec((LANES, seq), lambda b, gi: (gi, b))
    return pl.pallas_call(
        _nsa_body,
        grid=(batch, N_KV),
        in_specs=[pl.BlockSpec((width, seq), lambda b, gi: (gi, b)),
                  pl.BlockSpec((1, 1, ncmp, LANES), lambda b, gi: (b, gi, 0, 0)),
                  pl.BlockSpec((1, 1, LANES, ncmp), lambda b, gi: (b, gi, 0, 0)),
                  k_spec, vt_spec, k_spec, vt_spec,
                  pl.BlockSpec((GATE_ROWS, seq), lambda b, gi: (gi, b)),
                  pl.BlockSpec((1, nq, ncmp, cols), lambda b, gi: (gi, 0, 0, 0)),
                  pl.BlockSpec((1, N_SEL_BIAS, Q_TILE, cols), lambda b, gi: (gi, 0, 0, 0)),
                  pl.BlockSpec((1, 1, Q_TILE, cols), lambda b, gi: (gi, 0, 0, 0)),
                  _const_spec(mt.shape)],
        out_specs=pl.BlockSpec((1, seq, width), lambda b, gi: (b, 0, gi)),
        out_shape=jax.ShapeDtypeStruct((batch, seq, N_KV * width), f32),
        compiler_params=_params(("parallel", "arbitrary")),
        name="nsa",
    )(qt, kc, vct, ks, vst, kw, vwt, gtt, bc, bs, bw, mt)


def _out_proj_body(x_ref, yl_ref, yn_ref, gn_ref, w_ref, gp_ref, gm_ref, o_ref, a_ref):
    yn = _rms(yn_ref[...], gn_ref[...]).astype(bf16)
    y = _dot(yl_ref[...], w_ref[0:LRU_WIDTH, :]) + _dot(yn, w_ref[LRU_WIDTH:, :])
    h = x_ref[...] + _rms(y, gp_ref[...])
    o_ref[...] = h
    a_ref[...] = _rms(h, gm_ref[...]).astype(bf16)


def _out_proj(x2, yl, yn, gn, w, gp, gm, tm=512):
    m = x2.shape[0]
    row = pl.BlockSpec((tm, D_MODEL), lambda i: (i, 0))
    return pl.pallas_call(
        _out_proj_body,
        grid=(m // tm,),
        in_specs=[row,
                  pl.BlockSpec((tm, LRU_WIDTH), lambda i: (i, 0)),
                  pl.BlockSpec((tm, D_MODEL - LRU_WIDTH), lambda i: (i, 0)),
                  _const_spec((1, D_MODEL - LRU_WIDTH)),
                  _const_spec((D_MODEL, D_MODEL)),
                  _const_spec((1, D_MODEL)), _const_spec((1, D_MODEL))],
        out_specs=[row, row],
        out_shape=[jax.ShapeDtypeStruct((m, D_MODEL), f32), jax.ShapeDtypeStruct((m, D_MODEL), bf16)],
        compiler_params=_params(("parallel",)),
        name="out_proj",
    )(x2, yl, yn, gn, w, gp, gm)


def _mlp_body(a_ref, w1_ref, w2_ref, g2_ref, o_ref, acc_sc):
    j = pl.program_id(1)

    @pl.when(j == 0)
    def _():
        acc_sc[...] = jnp.zeros_like(acc_sc)

    hid = jnp.maximum(_dot(a_ref[...], w1_ref[...]), 0.0)
    acc_sc[...] += _dot((hid * hid).astype(bf16), w2_ref[...])

    @pl.when(j == pl.num_programs(1) - 1)
    def _():
        o_ref[...] = _rms(acc_sc[...], g2_ref[...])


def _mlp(a, w1, w2, g2, tm=1024, tf=512):
    m = a.shape[0]
    return pl.pallas_call(
        _mlp_body,
        grid=(m // tm, D_FF // tf),
        in_specs=[pl.BlockSpec((tm, D_MODEL), lambda i, j: (i, 0)),
                  pl.BlockSpec((D_MODEL, tf), lambda i, j: (0, j)),
                  pl.BlockSpec((tf, D_MODEL), lambda i, j: (j, 0)),
                  _const_spec((1, D_MODEL))],
        out_specs=pl.BlockSpec((tm, D_MODEL), lambda i, j: (i, 0)),
        out_shape=jax.ShapeDtypeStruct((m, D_MODEL), f32),
        scratch_shapes=[pltpu.VMEM((tm, D_MODEL), f32)],
        compiler_params=_params(("parallel", "arbitrary")),
        name="mlp",
    )(a, w1, w2, g2)


def _ple_body(h_ref, f_ref, p_ref, wg_ref, wp_ref, o_ref):
    h = h_ref[...] + f_ref[...]
    gate = jax.nn.sigmoid(_dot(h.astype(bf16), wg_ref[...]))
    o_ref[...] = h + gate * _dot(p_ref[...].astype(bf16), wp_ref[...])


def _ple(h, f, p2, wg, wp, tm=512):
    m = h.shape[0]
    row = pl.BlockSpec((tm, D_MODEL), lambda i: (i, 0))
    return pl.pallas_call(
        _ple_body,
        grid=(m // tm,),
        in_specs=[row, row,
                  pl.BlockSpec((tm, PLE_DIM), lambda i: (i, 0)),
                  _const_spec((D_MODEL, D_MODEL)),
                  _const_spec((PLE_DIM, D_MODEL))],
        out_specs=row,
        out_shape=jax.ShapeDtypeStruct((m, D_MODEL), f32),
        compiler_params=_params(("parallel",)),
        name="ple",
    )(h, f, p2, wg, wp)


def _block_diag_chunks(w):
    per = 256 // LRU_BLOCK_DIM
    w = w.reshape(LRU_BLOCKS // per, per, LRU_BLOCK_DIM, LRU_BLOCK_DIM)
    eye = jnp.eye(per, dtype=w.dtype)
    return jnp.einsum('cpij,pq->cpiqj', w, eye).reshape(LRU_BLOCKS // per, 256, 256)


def _layer(h, p_i, i, prm, bias):
    batch, seq, _ = h.shape
    m = batch * seq
    x2 = h.reshape(m, D_MODEL)
    row = lambda v: v.reshape(1, -1)
    per_batch = lambda t: t.reshape(batch, seq, t.shape[-1])

    w_in = prm["w_in"][i].astype(bf16)
    w_tok = jnp.concatenate([w_in[:, 0:2048], w_in[:, 3072:3840], w_in[:, 4096:4352]], axis=1)
    per = 3 * HEADS_PER_KV
    w_gate = jnp.pad(w_in[:, GATE_LO:IN_DIM].reshape(D_MODEL, N_KV, per), ((0, 0), (0, 0), (0, GATE_ROWS - per)))
    w_feat = jnp.concatenate([w_in[:, 2048:3072], w_in[:, 3840:4096], w_in[:, 4352:4608],
                              w_gate.reshape(D_MODEL, N_KV * GATE_ROWS)], axis=1).T
    u, gl, kc, vc, ks, kw, qt, vst, vwt, gtt = _in_proj(x2, row(prm["norm_mix_pre"][i]), w_tok, w_feat, seq)

    y_lru = _rglru(u, gl, prm["conv_w"][i], row(prm["conv_b"][i]),
                   _block_diag_chunks(prm["lru_wa"][i]).astype(bf16), row(prm["lru_ba"][i]),
                   _block_diag_chunks(prm["lru_wx"][i]).astype(bf16), row(prm["lru_bx"][i]),
                   row(prm["lru_lambda"][i]), row(prm["gnorm_lru"][i]), batch, seq)

    lane_pad = lambda w: jnp.pad(w, ((0, 0), (0, LANES - HEAD_DIM))).astype(bf16)
    kcc, vcct = _compress(per_batch(kc), per_batch(vc),
                          prm["cmp_pe_k"][i].reshape(1, -1), prm["cmp_w1_k"][i].astype(bf16),
                          lane_pad(prm["cmp_w2_k"][i]),
                          prm["cmp_pe_v"][i].reshape(1, -1), prm["cmp_w1_v"][i].astype(bf16),
                          lane_pad(prm["cmp_w2_v"][i]).T)

    bc, bs, bw = bias
    y_nsa = _nsa(qt, kcc, vcct, per_batch(ks), vst, per_batch(kw), vwt, gtt, bc, bs, bw,
                 jnp.asarray(_importance_matrix_t(seq)).astype(bf16))

    h1, a1 = _out_proj(x2, y_lru, y_nsa.reshape(m, -1), row(prm["gnorm_nsa"][i]),
                       prm["w_out"][i].astype(bf16), row(prm["norm_mix_post"][i]), row(prm["norm_mlp_pre"][i]))
    f = _mlp(a1, prm["mlp_w1"][i].astype(bf16), prm["mlp_w2"][i].astype(bf16), row(prm["norm_mlp_post"][i]))
    h3 = _ple(h1, f, p_i.reshape(m, PLE_DIM), prm["ple_gate"][i].astype(bf16), prm["ple_proj"][i].astype(bf16))
    return h3.reshape(batch, seq, D_MODEL)


def kernel(x, p, norm_mix_pre, norm_mix_post, norm_mlp_pre, norm_mlp_post, w_in, conv_w, conv_b, lru_wa, lru_ba, lru_wx, lru_bx, lru_lambda, cmp_pe_k, cmp_w1_k, cmp_w2_k, cmp_pe_v, cmp_w1_v, cmp_w2_v, rel_bias, gnorm_lru, gnorm_nsa, w_out, mlp_w1, mlp_w2, ple_gate, ple_proj):
    prm = dict(norm_mix_pre=norm_mix_pre, norm_mix_post=norm_mix_post, norm_mlp_pre=norm_mlp_pre,
               norm_mlp_post=norm_mlp_post, w_in=w_in, conv_w=conv_w, conv_b=conv_b, lru_wa=lru_wa,
               lru_ba=lru_ba, lru_wx=lru_wx, lru_bx=lru_bx, lru_lambda=lru_lambda, cmp_pe_k=cmp_pe_k,
               cmp_w1_k=cmp_w1_k, cmp_w2_k=cmp_w2_k, cmp_pe_v=cmp_pe_v, cmp_w1_v=cmp_w1_v,
               cmp_w2_v=cmp_w2_v, gnorm_lru=gnorm_lru, gnorm_nsa=gnorm_nsa, w_out=w_out, mlp_w1=mlp_w1,
               mlp_w2=mlp_w2, ple_gate=ple_gate, ple_proj=ple_proj)
    bias = _bias_tiles(rel_bias, x.shape[1])
    h = x
    for i in range(w_in.shape[0]):
        h = _layer(h, p[i], i, prm, bias)
    return h
```

```python
import functools
import math

import numpy as np
import jax
import jax.numpy as jnp
from jax import lax
from jax.experimental import pallas as pl
from jax.experimental.pallas import tpu as pltpu

D_MODEL = 2048
PLE_DIM = 256
LRU_WIDTH = 1024
LRU_BLOCKS = 16
LRU_BLOCK_DIM = 64
CONV_WIDTH = 4
LRU_C = 8.0
HEAD_DIM = 64
N_HEADS = 16
N_KV = 4
HEADS_PER_KV = 4
CMP_LEN = 32
CMP_STRIDE = 16
CMP_HIDDEN = 256
SEL_BLOCK = 64
N_SELECT = 16
N_LOCAL_FORCED = 2
WINDOW = 512
N_BUCKETS = 32
MAX_DISTANCE = 128
D_FF = 4 * D_MODEL
NORM_EPS = 1e-6
NEG_INF = -1e30
FORCE_SCORE = 1e4
IN_DIM = 4656
IN_PAD = 4736
GATE_LO = 4608

LANES = 128
MASK_LO = HEAD_DIM
Q_TILE = 128
N_SEL_BIAS = 2
V_ROWS = 80
LOG2E = math.log2(math.e)

VMEM_LIMIT = 56 * 1024 * 1024

f32 = jnp.float32
bf16 = jnp.bfloat16


def _rms(x, g):
    return x * lax.rsqrt(jnp.mean(x * x, axis=-1, keepdims=True) + NORM_EPS) * g


def _dot(a, b):
    return jnp.dot(a, b, preferred_element_type=f32)


def _dot_nt(a, b):
    return lax.dot_general(a, b, (((1,), (1,)), ((), ())), preferred_element_type=f32)


def _dot_tn(a, b):
    return lax.dot_general(a, b, (((0,), (0,)), ((), ())), preferred_element_type=f32)


def _const_spec(shape):
    nd = len(shape)
    return pl.BlockSpec(shape, lambda *_: (0,) * nd)


def _params(sem):
    return pltpu.CompilerParams(dimension_semantics=sem, vmem_limit_bytes=VMEM_LIMIT)


def _spread_groups(z, fill):
    lane = lax.broadcasted_iota(jnp.int32, (z.shape[0], LANES), 1)
    parts = []
    for g in range(N_KV):
        pair = z[:, (g // 2) * LANES:(g // 2 + 1) * LANES]
        if g % 2:
            pair = pltpu.roll(pair, HEAD_DIM, 1)
        parts.append(jnp.where(lane < HEAD_DIM, pair, fill))
    return jnp.concatenate(parts, axis=1)


def _value_rows(z):
    tm = z.shape[1]
    ones_row = jnp.where(lax.broadcasted_iota(jnp.int32, (LANES - HEAD_DIM, tm), 0) == 0, 1.0, 0.0)
    parts = []
    for g in range(N_KV):
        parts += [z[g * HEAD_DIM:(g + 1) * HEAD_DIM, :], ones_row]
    return jnp.concatenate(parts, axis=0)


TOK_U, TOK_GL, TOK_KC, TOK_VC, TOK_KS, TOK_KW, TOK_END = 0, 1024, 2048, 2304, 2560, 2816, 3072
FEAT_Q, FEAT_VS, FEAT_VW, FEAT_GT, FEAT_END = 0, 1024, 1280, 1536, 1600
GATE_ROWS = 16


def _in_proj_body(seq, x_ref, g_ref, w_ref, wt_ref, u_ref, gl_ref, kc_ref, vc_ref, ks_ref, kw_ref,
                  qt_ref, vst_ref, vwt_ref, gtt_ref):
    tm = x_ref.shape[0]
    a = _rms(x_ref[...], g_ref[...]).astype(bf16)

    def tok(lo, hi):
        return _dot(a, w_ref[:, lo:hi])

    def feat(lo, hi):
        return _dot_nt(wt_ref[lo:hi, :], a)

    u_ref[...] = tok(TOK_U, TOK_GL)
    gl_ref[...] = tok(TOK_GL, TOK_KC)
    kc_ref[...] = tok(TOK_KC, TOK_VC)
    vc_ref[...] = tok(TOK_VC, TOK_KS)
    lane = lax.broadcasted_iota(jnp.int32, (tm, LANES), 1)
    pos = (pl.program_id(0) * tm) % seq + lax.broadcasted_iota(jnp.int32, (tm, LANES), 0)
    block_onehot = jnp.where(lane - MASK_LO == pos // SEL_BLOCK, 1.0, 0.0)
    ks_ref[...] = _spread_groups(tok(TOK_KS, TOK_KW), block_onehot).astype(bf16)
    kw_ref[...] = _spread_groups(tok(TOK_KW, TOK_END), 0.0).astype(bf16)

    qt_ref[...] = (feat(FEAT_Q, FEAT_VS) * (LOG2E * HEAD_DIM ** -0.5)).astype(bf16)
    vst_ref[...] = _value_rows(feat(FEAT_VS, FEAT_VW)).astype(bf16)
    vwt_ref[...] = _value_rows(feat(FEAT_VW, FEAT_GT)).astype(bf16)
    gtt_ref[...] = feat(FEAT_GT, FEAT_END)


def _in_proj(x2, g, w, wt, seq, tm=512):
    m = x2.shape[0]
    wide = N_KV * LANES
    tok_out = [(1024, f32), (1024, f32), (256, f32), (256, f32), (wide, bf16), (wide, bf16)]
    feat_out = [(N_HEADS * HEAD_DIM, bf16), (wide, bf16), (wide, bf16), (N_KV * GATE_ROWS, f32)]
    body = lambda *refs: _in_proj_body(seq, *refs)
    return pl.pallas_call(
        body,
        grid=(m // tm,),
        in_specs=[pl.BlockSpec((tm, D_MODEL), lambda i: (i, 0)),
                  _const_spec((1, D_MODEL)),
                  pl.BlockSpec((D_MODEL, TOK_END), lambda i: (0, 0), pipeline_mode=pl.Buffered(1)),
                  pl.BlockSpec((FEAT_END, D_MODEL), lambda i: (0, 0), pipeline_mode=pl.Buffered(1))],
        out_specs=([pl.BlockSpec((tm, n), lambda i: (i, 0)) for n, _ in tok_out]
                   + [pl.BlockSpec((n, tm), lambda i: (0, i)) for n, _ in feat_out]),
        out_shape=([jax.ShapeDtypeStruct((m, n), dt) for n, dt in tok_out]
                   + [jax.ShapeDtypeStruct((n, m), dt) for n, dt in feat_out]),
        compiler_params=_params(("parallel",)),
        name="in_proj",
    )(x2, g, w, wt)


def _rglru_body(u_ref, gl_ref, cw_ref, cb_ref, wa_ref, ba_ref, wx_ref, bx_ref, lam_ref, gn_ref,
                o_ref, ubuf, a_sc, b_sc, hc_sc):
    t = u_ref.shape[0]

    @pl.when(pl.program_id(1) == 0)
    def _():
        ubuf[...] = jnp.zeros_like(ubuf)
        hc_sc[...] = jnp.zeros_like(hc_sc)

    u = u_ref[...]
    u3 = u.reshape(t // 8, 8, LRU_WIDTH)
    tail = ubuf[...]
    cw = cw_ref[...]
    row8w = lax.broadcasted_iota(jnp.int32, (t // 8, 8, LRU_WIDTH), 1)
    xc3 = cb_ref[...] + u3 * cw[CONV_WIDTH - 1]
    for d in range(1, CONV_WIDTH):
        cur = pltpu.roll(u3, d, 1)
        prev = jnp.concatenate([pltpu.roll(tail, d, 0)[None], cur[:-1]], axis=0)
        xc3 = xc3 + jnp.where(row8w >= d, cur, prev) * cw[CONV_WIDTH - 1 - d]
    ubuf[...] = u[t - 8:t, :]
    xc = xc3.reshape(t, LRU_WIDTH)

    xb = xc.astype(bf16)
    sp = jax.nn.softplus(-lam_ref[...])
    row8 = lax.broadcasted_iota(jnp.int32, (t // 8, 8, 256), 1)
    for c in range(LRU_WIDTH // 256):
        sl = slice(c * 256, (c + 1) * 256)
        xcb = xb[:, sl]
        r = jax.nn.sigmoid(_dot(xcb, wa_ref[c]) + ba_ref[:, sl])
        ig = jax.nn.sigmoid(_dot(xcb, wx_ref[c]) + bx_ref[:, sl])
        log_a = (-LRU_C) * r * sp[:, sl]
        a = jnp.exp(log_a)
        b = jnp.sqrt(1.0 - a * a) * (ig * xc[:, sl])
        a = a.reshape(t // 8, 8, 256)
        b = b.reshape(t // 8, 8, 256)
        for d in (1, 2, 4):
            keep = row8 >= d
            a_prev = pltpu.roll(a, d, 1)
            b_prev = pltpu.roll(b, d, 1)
            b = jnp.where(keep, a * b_prev + b, b)
            a = jnp.where(keep, a * a_prev, a)
        a_sc[:, sl] = a.reshape(t, 256)
        b_sc[:, sl] = b.reshape(t, 256)

    def group(gi, h):
        off = pl.multiple_of(gi * 8, 8)
        hg = b_sc[pl.ds(off, 8), :] + a_sc[pl.ds(off, 8), :] * h
        b_sc[pl.ds(off, 8), :] = hg
        return hg[7:8, :]

    h_last = lax.fori_loop(0, t // 8, group, hc_sc[0:1, :])
    hc_sc[0:1, :] = h_last
    y = b_sc[...] * jax.nn.gelu(gl_ref[...])
    o_ref[...] = _rms(y, gn_ref[...]).astype(o_ref.dtype)


def _rglru(u, gl, cw, cb, wa, ba, wx, bx, lam, gn, batch, seq, t=256):
    ns = seq // t
    row = pl.BlockSpec((t, LRU_WIDTH), lambda b, s: (b * ns + s, 0))
    vec = _const_spec((1, LRU_WIDTH))
    wspec = _const_spec((LRU_WIDTH // 256, 256, 256))
    return pl.pallas_call(
        _rglru_body,
        grid=(batch, ns),
        in_specs=[row, row, _const_spec((CONV_WIDTH, LRU_WIDTH)), vec, wspec, vec, wspec, vec, vec, vec],
        out_specs=row,
        out_shape=jax.ShapeDtypeStruct((batch * seq, LRU_WIDTH), bf16),
        scratch_shapes=[pltpu.VMEM((8, LRU_WIDTH), f32), pltpu.VMEM((t, LRU_WIDTH), f32),
                        pltpu.VMEM((t, LRU_WIDTH), f32), pltpu.VMEM((8, LRU_WIDTH), f32)],
        compiler_params=_params(("parallel", "arbitrary")),
        name="rglru",
    )(u, gl, cw, cb, wa, ba, wx, bx, lam, gn)


def _compress_body(kc_ref, vc_ref, pek_ref, w1k_ref, w2k_ref, pev_ref, w1v_ref, w2v_ref, ko_ref, vo_ref):
    nrow = ko_ref.shape[2]
    rows = lax.broadcasted_iota(jnp.int32, (nrow, LANES), 0)
    cols = lax.broadcasted_iota(jnp.int32, (LANES, nrow), 1)
    lane64 = lax.broadcasted_iota(jnp.int32, (nrow, LANES), 1) < HEAD_DIM
    half = CMP_STRIDE * HEAD_DIM

    def one(t_ref, pe_ref, w1_ref, w2_ref, o_ref, transposed):
        pe = jnp.broadcast_to(pe_ref[...], (8, CMP_LEN * HEAD_DIM)).astype(bf16)
        c0 = _dot(pe, w1_ref[...])[0:1, :]
        toks = [t_ref[0, pl.ds(j, nrow, stride=CMP_STRIDE), :] for j in range(CMP_STRIDE)]
        for g in range(LANES // HEAD_DIM):
            blocks = []
            for i in range(CMP_STRIDE // 2):
                even, odd = toks[2 * i], toks[2 * i + 1]
                if g == 0:
                    blocks.append(jnp.where(lane64, even, pltpu.roll(odd, HEAD_DIM, 1)))
                else:
                    blocks.append(jnp.where(lane64, pltpu.roll(even, HEAD_DIM, 1), odd))
            chunk = jnp.concatenate(blocks, axis=1).astype(bf16)
            lo = _dot(chunk, w1_ref[0:half, :])
            hi = _dot(chunk, w1_ref[half:2 * half, :])
            hid = jax.nn.gelu(lo + pltpu.roll(hi, nrow - 1, 0) + c0)
            if transposed:
                out = _dot_nt(w2_ref[...], hid.astype(bf16))
                o_ref[0, g] = jnp.where(cols < nrow - 1, out, 0.0).astype(o_ref.dtype)
            else:
                out = _dot(hid.astype(bf16), w2_ref[...])
                o_ref[0, g] = jnp.where(rows < nrow - 1, out, 0.0).astype(o_ref.dtype)

    one(kc_ref, pek_ref, w1k_ref, w2k_ref, ko_ref, False)
    one(vc_ref, pev_ref, w1v_ref, w2v_ref, vo_ref, True)


def _compress(kc, vc, pek, w1k, w2k, pev, w1v, w2v):
    batch, seq, width = kc.shape
    nrow = seq // CMP_STRIDE
    pair = LANES // HEAD_DIM
    tspec = pl.BlockSpec((1, seq, LANES), lambda b, h: (b, 0, h))
    ospec = pl.BlockSpec((1, pair, nrow, LANES), lambda b, h: (b, h, 0, 0))
    wts = [_const_spec((1, CMP_LEN * HEAD_DIM)), _const_spec((CMP_LEN * HEAD_DIM, CMP_HIDDEN))]
    tspec_v = pl.BlockSpec((1, pair, LANES, nrow), lambda b, h: (b, h, 0, 0))
    return pl.pallas_call(
        _compress_body,
        grid=(batch, width // LANES),
        in_specs=([tspec, tspec] + wts + [_const_spec((CMP_HIDDEN, LANES))]
                  + wts + [_const_spec((LANES, CMP_HIDDEN))]),
        out_specs=[ospec, tspec_v],
        out_shape=[jax.ShapeDtypeStruct((batch, N_KV, nrow, LANES), bf16),
                   jax.ShapeDtypeStruct((batch, N_KV, LANES, nrow), bf16)],
        compiler_params=_params(("parallel", "parallel")),
        name="compress",
    )(kc, vc, pek, w1k, w2k, pev, w1v, w2v)


def _bucket_thresholds():
    n = np.arange(0, 4096)
    max_exact = N_BUCKETS // 2
    nf = np.maximum(n, 1).astype(np.float32)
    large = max_exact + (np.log(nf / np.float32(max_exact)) / np.float32(math.log(MAX_DISTANCE / max_exact))
                         * np.float32(N_BUCKETS - max_exact)).astype(np.int32)
    large = np.minimum(large, N_BUCKETS - 1)
    bucket = np.where(n < max_exact, n, large)
    assert np.all(np.diff(bucket) >= 0) and bucket[0] == 0 and bucket[-1] == N_BUCKETS - 1
    return [int(np.argmax(bucket >= k)) for k in range(N_BUCKETS)]


_BUCKET_THR = _bucket_thresholds()
assert _BUCKET_THR[-1] <= (N_SEL_BIAS - 1) * Q_TILE + 1


def _bias_of_dist(dist, ok, tab_ref, head):
    last = tab_ref[N_BUCKETS - 1, head]
    val = jnp.full(dist.shape, (tab_ref[0, head] - last) * LOG2E, f32)
    for k in range(1, N_BUCKETS - 1):
        val = jnp.where(dist >= _BUCKET_THR[k], (tab_ref[k, head] - last) * LOG2E, val)
    val = jnp.where(dist >= _BUCKET_THR[N_BUCKETS - 1], 0.0, val)
    return jnp.where(ok, val, NEG_INF)


def _bias_body(tab_ref, bc_ref, bs_ref, bw_ref):
    g = pl.program_id(0)
    qt = pl.program_id(1)
    ncmp = bc_ref.shape[2]
    for r in range(HEADS_PER_KV):
        head = g * HEADS_PER_KV + r
        cols = slice(r * Q_TILE, (r + 1) * Q_TILE)
        n = lax.broadcasted_iota(jnp.int32, (ncmp, Q_TILE), 0)
        i = lax.broadcasted_iota(jnp.int32, (ncmp, Q_TILE), 1)
        dist = qt * Q_TILE + i - (n * CMP_STRIDE + CMP_LEN - 1)
        bc_ref[0, 0, :, cols] = _bias_of_dist(dist, (dist >= 0) & (n < ncmp - 1), tab_ref, head)

    @pl.when(qt == 0)
    def _():
        j = lax.broadcasted_iota(jnp.int32, (Q_TILE, Q_TILE), 0)
        i = lax.broadcasted_iota(jnp.int32, (Q_TILE, Q_TILE), 1)
        for r in range(HEADS_PER_KV):
            head = g * HEADS_PER_KV + r
            cols = slice(r * Q_TILE, (r + 1) * Q_TILE)
            for d in range(N_SEL_BIAS):
                dist = d * Q_TILE + i - j
                bs_ref[0, d, :, cols] = _bias_of_dist(dist, dist >= 0, tab_ref, head)
            dist = WINDOW + i - j
            bw_ref[0, 0, :, cols] = _bias_of_dist(dist, dist < WINDOW, tab_ref, head)


def _bias_tiles(rel_bias, seq):
    nq = seq // Q_TILE
    ncmp = seq // CMP_STRIDE
    rows = HEADS_PER_KV * Q_TILE
    return pl.pallas_call(
        _bias_body,
        grid=(N_KV, nq),
        in_specs=[pl.BlockSpec(memory_space=pltpu.SMEM)],
        out_specs=[pl.BlockSpec((1, 1, ncmp, rows), lambda g, q: (g, q, 0, 0)),
                   pl.BlockSpec((1, N_SEL_BIAS, Q_TILE, rows), lambda g, q: (g, 0, 0, 0)),
                   pl.BlockSpec((1, 1, Q_TILE, rows), lambda g, q: (g, 0, 0, 0))],
        out_shape=[jax.ShapeDtypeStruct((N_KV, nq, ncmp, rows), f32),
                   jax.ShapeDtypeStruct((N_KV, N_SEL_BIAS, Q_TILE, rows), f32),
                   jax.ShapeDtypeStruct((N_KV, 1, Q_TILE, rows), f32)],
        compiler_params=_params(("parallel", "arbitrary")),
        name="bias_tiles",
    )(rel_bias)


def _importance_matrix_t(seq):
    n_cmp = (seq - CMP_LEN) // CMP_STRIDE + 1
    n_sel = seq // SEL_BLOCK
    ratio_sel = SEL_BLOCK // CMP_STRIDE
    ratio_cmp = CMP_LEN // CMP_STRIDE
    jj = np.arange(n_sel)[:, None, None]
    ci = ratio_sel * jj + np.arange(ratio_sel)[None, :, None] - np.arange(ratio_cmp)[None, None, :]
    jb = np.broadcast_to(jj, ci.shape)
    ok = (ci >= 0) & (ci < n_cmp)
    m = np.zeros((n_sel, seq // CMP_STRIDE), np.float32)
    np.add.at(m, (jb[ok], ci[ok]), 1.0)
    return m


def _scores(qt, k_ref, t, first_tile, biases):
    s = _dot(k_ref[0, first_tile * Q_TILE:(t + 1) * Q_TILE, :], qt)
    pieces = []
    for kt in range(first_tile, t + 1):
        piece = s[(kt - first_tile) * Q_TILE:(kt - first_tile + 1) * Q_TILE, :]
        if t - kt in biases:
            piece = piece + biases[t - kt]
        pieces.append(piece)
    return pieces, jnp.max(functools.reduce(jnp.maximum, pieces), axis=0, keepdims=True)


def _weighted_values(pieces, m, vt_ref, t, first_tile):
    p = jnp.concatenate([jnp.exp2(pc - m).astype(bf16) for pc in pieces], axis=0)
    acc = _dot(vt_ref[0:V_ROWS, first_tile * Q_TILE:(t + 1) * Q_TILE], p)
    return acc[0:HEAD_DIM, :], 1.0 / acc[MASK_LO:MASK_LO + 1, :]


def _nsa_body(qt_ref, kc_ref, vct_ref, ks_ref, vst_ref, kw_ref, vwt_ref, gtt_ref, bc_ref, bs_ref, bw_ref,
              mt_ref, o_ref):
    n_sel = mt_ref.shape[0]
    n_tiles = qt_ref.shape[1] // Q_TILE
    cols = HEADS_PER_KV * Q_TILE
    head_cols = [slice(r * Q_TILE, (r + 1) * Q_TILE) for r in range(HEADS_PER_KV)]
    kc = kc_ref[0, 0]
    vct = vct_ref[0, 0]
    mt = mt_ref[...]
    jj = lax.broadcasted_iota(jnp.int32, (n_sel, Q_TILE), 0)
    qcol = lax.broadcasted_iota(jnp.int32, (n_sel, Q_TILE), 1)
    zeros_tail = jnp.zeros((LANES - HEAD_DIM, Q_TILE), bf16)
    zeros_rest = jnp.zeros((LANES - HEAD_DIM - n_sel, Q_TILE), bf16)

    def select_stage(t):
        tok = slice(t * Q_TILE, (t + 1) * Q_TILE)
        heads = [qt_ref[r * HEAD_DIM:(r + 1) * HEAD_DIM, tok] for r in range(HEADS_PER_KV)]
        q_plain = jnp.concatenate([jnp.concatenate([hd, zeros_tail], axis=0) for hd in heads], axis=1)

        s = _dot(kc, q_plain) + bc_ref[0, t]
        p = jnp.exp2(s - jnp.max(s, axis=0, keepdims=True))
        norm = 1.0 / jnp.sum(p, axis=0, keepdims=True)
        if (t + 1) * Q_TILE > CMP_LEN - 1 >= t * Q_TILE:
            pos = t * Q_TILE + (lax.broadcasted_iota(jnp.int32, (1, cols), 1) & (Q_TILE - 1))
            norm = jnp.where(pos >= CMP_LEN - 1, norm, 0.0)
        p = p * norm
        o_c = _dot(vct[0:HEAD_DIM, :], p.astype(bf16))

        n_causal = ((t + 1) * Q_TILE - 1) // SEL_BLOCK + 1
        if n_causal <= N_SELECT:
            return q_plain, q_plain, o_c

        psum = p[:, head_cols[0]] + p[:, head_cols[1]] + p[:, head_cols[2]] + p[:, head_cols[3]]
        p_hi = psum.astype(bf16)
        p_lo = (psum - p_hi.astype(f32)).astype(bf16)
        imp = _dot(mt, p_hi) + _dot(mt, p_lo)
        dblk = (t * Q_TILE + qcol) // SEL_BLOCK - jj
        forced = (jj == 0) | ((dblk >= 0) & (dblk < N_LOCAL_FORCED))
        imp = jnp.where(forced, FORCE_SCORE, jnp.where(dblk >= 0, imp, -FORCE_SCORE))
        rank = jnp.zeros((n_sel, Q_TILE), f32)
        for i in range(n_causal):
            row = imp[i:i + 1, :]
            ahead = (row > imp) | ((row == imp) & (jj > i))
            rank = rank + jnp.where(ahead, 1.0, 0.0)
        mask_rows = jnp.where(rank < N_SELECT, 0.0, NEG_INF).astype(bf16)
        q_masked = jnp.concatenate(
            [jnp.concatenate([hd, mask_rows, zeros_rest], axis=0) for hd in heads], axis=1)
        return q_plain, q_masked, o_c

    def score_stage(t, q_plain, q_masked, o_c):
        near = {d: bs_ref[0, d] for d in range(N_SEL_BIAS)}
        band = dict(near)
        band[WINDOW // Q_TILE] = bw_ref[0, 0]
        sel = _scores(q_masked, ks_ref, t, 0, near)
        win = _scores(q_plain, kw_ref, t, max(t - WINDOW // Q_TILE, 0), band)
        return sel, win, o_c

    def output_stage(t, sel, win, o_c):
        tok = slice(t * Q_TILE, (t + 1) * Q_TILE)
        acc_s, inv_s = _weighted_values(*sel, vst_ref, t, 0)
        acc_w, inv_w = _weighted_values(*win, vwt_ref, t, max(t - WINDOW // Q_TILE, 0))
        gt = jax.nn.sigmoid(gtt_ref[:, tok])
        gate = lambda br: jnp.concatenate([gt[3 * r + br:3 * r + br + 1, :] for r in range(HEADS_PER_KV)], axis=1)
        out_t = gate(0) * o_c + (gate(1) * inv_s) * acc_s + (gate(2) * inv_w) * acc_w
        pairs = [jnp.concatenate([out_t[:, head_cols[2 * h]], out_t[:, head_cols[2 * h + 1]]], axis=0).T
                 for h in range(HEADS_PER_KV // 2)]
        o_ref[0, tok, :] = jnp.concatenate(pairs, axis=-1)

    selected = {t: select_stage(t) for t in range(min(2, n_tiles))}
    scored = {0: score_stage(0, *selected.pop(0))}
    for t in range(n_tiles):
        if t + 2 < n_tiles:
            selected[t + 2] = select_stage(t + 2)
        if t + 1 < n_tiles:
            scored[t + 1] = score_stage(t + 1, *selected.pop(t + 1))
        output_stage(t, *scored.pop(t))


def _nsa(qt, kc, vct, ks, vst, kw, vwt, gtt, bc, bs, bw, mt):
    batch, seq, _ = ks.shape
    nq = seq // Q_TILE
    cols = HEADS_PER_KV * Q_TILE
    ncmp = kc.shape[2]
    width = HEADS_PER_KV * HEAD_DIM
    k_spec = pl.BlockSpec((1, seq, LANES), lambda b, gi: (b, 0, gi))
    vt_spec = pl.BlockSpec((LANES, seq), lambda b, gi: (gi, b))
    return pl.pallas_call(
        _nsa_body,
        grid=(batch, N_KV),
        in_specs=[pl.BlockSpec((width, seq), lambda b, gi: (gi, b)),
                  pl.BlockSpec((1, 1, ncmp, LANES), lambda b, gi: (b, gi, 0, 0)),
                  pl.BlockSpec((1, 1, LANES, ncmp), lambda b, gi: (b, gi, 0, 0)),
                  k_spec, vt_spec, k_spec, vt_spec,
                  pl.BlockSpec((GATE_ROWS, seq), lambda b, gi: (gi, b)),
                  pl.BlockSpec((1, nq, ncmp, cols), lambda b, gi: (gi, 0, 0, 0)),
                  pl.BlockSpec((1, N_SEL_BIAS, Q_TILE, cols), lambda b, gi: (gi, 0, 0, 0)),
                  pl.BlockSpec((1, 1, Q_TILE, cols), lambda b, gi: (gi, 0, 0, 0)),
                  _const_spec(mt.shape)],
        out_specs=pl.BlockSpec((1, seq, width), lambda b, gi: (b, 0, gi)),
        out_shape=jax.ShapeDtypeStruct((batch, seq, N_KV * width), f32),
        compiler_params=_params(("parallel", "arbitrary")),
        name="nsa",
    )(qt, kc, vct, ks, vst, kw, vwt, gtt, bc, bs, bw, mt)


def _out_proj_body(x_ref, yl_ref, yn_ref, gn_ref, w_ref, gp_ref, gm_ref, o_ref, a_ref):
    yn = _rms(yn_ref[...], gn_ref[...]).astype(bf16)
    y = _dot(yl_ref[...], w_ref[0:LRU_WIDTH, :]) + _dot(yn, w_ref[LRU_WIDTH:, :])
    h = x_ref[...] + _rms(y, gp_ref[...])
    o_ref[...] = h
    a_ref[...] = _rms(h, gm_ref[...]).astype(bf16)


def _out_proj(x2, yl, yn, gn, w, gp, gm, tm=512):
    m = x2.shape[0]
    row = pl.BlockSpec((tm, D_MODEL), lambda i: (i, 0))
    return pl.pallas_call(
        _out_proj_body,
        grid=(m // tm,),
        in_specs=[row,
                  pl.BlockSpec((tm, LRU_WIDTH), lambda i: (i, 0)),
                  pl.BlockSpec((tm, D_MODEL - LRU_WIDTH), lambda i: (i, 0)),
                  _const_spec((1, D_MODEL - LRU_WIDTH)),
                  _const_spec((D_MODEL, D_MODEL)),
                  _const_spec((1, D_MODEL)), _const_spec((1, D_MODEL))],
        out_specs=[row, row],
        out_shape=[jax.ShapeDtypeStruct((m, D_MODEL), f32), jax.ShapeDtypeStruct((m, D_MODEL), bf16)],
        compiler_params=_params(("parallel",)),
        name="out_proj",
    )(x2, yl, yn, gn, w, gp, gm)


def _mlp_body(a_ref, w1_ref, w2_ref, g2_ref, o_ref, acc_sc):
    j = pl.program_id(1)

    @pl.when(j == 0)
    def _():
        acc_sc[...] = jnp.zeros_like(acc_sc)

    hid = jnp.maximum(_dot(a_ref[...], w1_ref[...]), 0.0)
    acc_sc[...] += _dot((hid * hid).astype(bf16), w2_ref[...])

    @pl.when(j == pl.num_programs(1) - 1)
    def _():
        o_ref[...] = _rms(acc_sc[...], g2_ref[...])


def _mlp(a, w1, w2, g2, tm=1024, tf=512):
    m = a.shape[0]
    return pl.pallas_call(
        _mlp_body,
        grid=(m // tm, D_FF // tf),
        in_specs=[pl.BlockSpec((tm, D_MODEL), lambda i, j: (i, 0)),
                  pl.BlockSpec((D_MODEL, tf), lambda i, j: (0, j)),
                  pl.BlockSpec((tf, D_MODEL), lambda i, j: (j, 0)),
                  _const_spec((1, D_MODEL))],
        out_specs=pl.BlockSpec((tm, D_MODEL), lambda i, j: (i, 0)),
        out_shape=jax.ShapeDtypeStruct((m, D_MODEL), f32),
        scratch_shapes=[pltpu.VMEM((tm, D_MODEL), f32)],
        compiler_params=_params(("parallel", "arbitrary")),
        name="mlp",
    )(a, w1, w2, g2)


def _ple_body(h_ref, f_ref, p_ref, wg_ref, wp_ref, o_ref):
    h = h_ref[...] + f_ref[...]
    gate = jax.nn.sigmoid(_dot(h.astype(bf16), wg_ref[...]))
    o_ref[...] = h + gate * _dot(p_ref[...].astype(bf16), wp_ref[...])


def _ple(h, f, p2, wg, wp, tm=512):
    m = h.shape[0]
    row = pl.BlockSpec((tm, D_MODEL), lambda i: (i, 0))
    return pl.pallas_call(
        _ple_body,
        grid=(m // tm,),
        in_specs=[row, row,
                  pl.BlockSpec((tm, PLE_DIM), lambda i: (i, 0)),
                  _const_spec((D_MODEL, D_MODEL)),
                  _const_spec((PLE_DIM, D_MODEL))],
        out_specs=row,
        out_shape=jax.ShapeDtypeStruct((m, D_MODEL), f32),
        compiler_params=_params(("parallel",)),
        name="ple",
    )(h, f, p2, wg, wp)


def _block_diag_chunks(w):
    per = 256 // LRU_BLOCK_DIM
    w = w.reshape(LRU_BLOCKS // per, per, LRU_BLOCK_DIM, LRU_BLOCK_DIM)
    eye = jnp.eye(per, dtype=w.dtype)
    return jnp.einsum('cpij,pq->cpiqj', w, eye).reshape(LRU_BLOCKS // per, 256, 256)


def _layer(h, p_i, i, prm, bias):
    batch, seq, _ = h.shape
    m = batch * seq
    x2 = h.reshape(m, D_MODEL)
    row = lambda v: v.reshape(1, -1)
    per_batch = lambda t: t.reshape(batch, seq, t.shape[-1])

    w_in = prm["w_in"][i].astype(bf16)
    w_tok = jnp.concatenate([w_in[:, 0:2048], w_in[:, 3072:3840], w_in[:, 4096:4352]], axis=1)
    per = 3 * HEADS_PER_KV
    w_gate = jnp.pad(w_in[:, GATE_LO:IN_DIM].reshape(D_MODEL, N_KV, per), ((0, 0), (0, 0), (0, GATE_ROWS - per)))
    w_feat = jnp.concatenate([w_in[:, 2048:3072], w_in[:, 3840:4096], w_in[:, 4352:4608],
                              w_gate.reshape(D_MODEL, N_KV * GATE_ROWS)], axis=1).T
    u, gl, kc, vc, ks, kw, qt, vst, vwt, gtt = _in_proj(x2, row(prm["norm_mix_pre"][i]), w_tok, w_feat, seq)

    y_lru = _rglru(u, gl, prm["conv_w"][i], row(prm["conv_b"][i]),
                   _block_diag_chunks(prm["lru_wa"][i]).astype(bf16), row(prm["lru_ba"][i]),
                   _block_diag_chunks(prm["lru_wx"][i]).astype(bf16), row(prm["lru_bx"][i]),
                   row(prm["lru_lambda"][i]), row(prm["gnorm_lru"][i]), batch, seq)

    lane_pad = lambda w: jnp.pad(w, ((0, 0), (0, LANES - HEAD_DIM))).astype(bf16)
    kcc, vcct = _compress(per_batch(kc), per_batch(vc),
                          prm["cmp_pe_k"][i].reshape(1, -1), prm["cmp_w1_k"][i].astype(bf16),
                          lane_pad(prm["cmp_w2_k"][i]),
                          prm["cmp_pe_v"][i].reshape(1, -1), prm["cmp_w1_v"][i].astype(bf16),
                          lane_pad(prm["cmp_w2_v"][i]).T)

    bc, bs, bw = bias
    y_nsa = _nsa(qt, kcc, vcct, per_batch(ks), vst, per_batch(kw), vwt, gtt, bc, bs, bw,
                 jnp.asarray(_importance_matrix_t(seq)).astype(bf16))

    h1, a1 = _out_proj(x2, y_lru, y_nsa.reshape(m, -1), row(prm["gnorm_nsa"][i]),
                       prm["w_out"][i].astype(bf16), row(prm["norm_mix_post"][i]), row(prm["norm_mlp_pre"][i]))
    f = _mlp(a1, prm["mlp_w1"][i].astype(bf16), prm["mlp_w2"][i].astype(bf16), row(prm["norm_mlp_post"][i]))
    h3 = _ple(h1, f, p_i.reshape(m, PLE_DIM), prm["ple_gate"][i].astype(bf16), prm["ple_proj"][i].astype(bf16))
    return h3.reshape(batch, seq, D_MODEL)


def kernel(x, p, norm_mix_pre, norm_mix_post, norm_mlp_pre, norm_mlp_post, w_in, conv_w, conv_b, lru_wa, lru_ba, lru_wx, lru_bx, lru_lambda, cmp_pe_k, cmp_w1_k, cmp_w2_k, cmp_pe_v, cmp_w1_v, cmp_w2_v, rel_bias, gnorm_lru, gnorm_nsa, w_out, mlp_w1, mlp_w2, ple_gate, ple_proj):
    prm = dict(norm_mix_pre=norm_mix_pre, norm_mix_post=norm_mix_post, norm_mlp_pre=norm_mlp_pre,
               norm_mlp_post=norm_mlp_post, w_in=w_in, conv_w=conv_w, conv_b=conv_b, lru_wa=lru_wa,
               lru_ba=lru_ba, lru_wx=lru_wx, lru_bx=lru_bx, lru_lambda=lru_lambda, cmp_pe_k=cmp_pe_k,
               cmp_w1_k=cmp_w1_k, cmp_w2_k=cmp_w2_k, cmp_pe_v=cmp_pe_v, cmp_w1_v=cmp_w1_v,
               cmp_w2_v=cmp_w2_v, gnorm_lru=gnorm_lru, gnorm_nsa=gnorm_nsa, w_out=w_out, mlp_w1=mlp_w1,
               mlp_w2=mlp_w2, ple_gate=ple_gate, ple_proj=ple_proj)
    bias = _bias_tiles(rel_bias, x.shape[1])
    h = x
    for i in range(w_in.shape[0]):
        h = _layer(h, p[i], i, prm, bias)
    return h
```

```python
import functools
import math

import numpy as np
import jax
import jax.numpy as jnp
from jax import lax
from jax.experimental import pallas as pl
from jax.experimental.pallas import tpu as pltpu

D_MODEL = 2048
PLE_DIM = 256
LRU_WIDTH = 1024
LRU_BLOCKS = 16
LRU_BLOCK_DIM = 64
CONV_WIDTH = 4
LRU_C = 8.0
HEAD_DIM = 64
N_HEADS = 16
N_KV = 4
HEADS_PER_KV = 4
CMP_LEN = 32
CMP_STRIDE = 16
CMP_HIDDEN = 256
SEL_BLOCK = 64
N_SELECT = 16
N_LOCAL_FORCED = 2
WINDOW = 512
N_BUCKETS = 32
MAX_DISTANCE = 128
D_FF = 4 * D_MODEL
NORM_EPS = 1e-6
NEG_INF = -1e30
FORCE_SCORE = 1e4
IN_DIM = 4656
IN_PAD = 4736
GATE_LO = 4608

LANES = 128
MASK_LO = HEAD_DIM
Q_TILE = 128
N_SEL_BIAS = 2
V_ROWS = 80
LOG2E = math.log2(math.e)

VMEM_LIMIT = 56 * 1024 * 1024

f32 = jnp.float32
bf16 = jnp.bfloat16


def _rms(x, g):
    return x * lax.rsqrt(jnp.mean(x * x, axis=-1, keepdims=True) + NORM_EPS) * g


def _dot(a, b):
    return jnp.dot(a, b, preferred_element_type=f32)


def _dot_nt(a, b):
    return lax.dot_general(a, b, (((1,), (1,)), ((), ())), preferred_element_type=f32)


def _dot_tn(a, b):
    return lax.dot_general(a, b, (((0,), (0,)), ((), ())), preferred_element_type=f32)


def _const_spec(shape):
    nd = len(shape)
    return pl.BlockSpec(shape, lambda *_: (0,) * nd)


def _params(sem):
    return pltpu.CompilerParams(dimension_semantics=sem, vmem_limit_bytes=VMEM_LIMIT)


def _spread_groups(z, fill):
    lane = lax.broadcasted_iota(jnp.int32, (z.shape[0], LANES), 1)
    parts = []
    for g in range(N_KV):
        pair = z[:, (g // 2) * LANES:(g // 2 + 1) * LANES]
        if g % 2:
            pair = pltpu.roll(pair, HEAD_DIM, 1)
        parts.append(jnp.where(lane < HEAD_DIM, pair, fill))
    return jnp.concatenate(parts, axis=1)


def _value_rows(z):
    tm = z.shape[1]
    ones_row = jnp.where(lax.broadcasted_iota(jnp.int32, (LANES - HEAD_DIM, tm), 0) == 0, 1.0, 0.0)
    parts = []
    for g in range(N_KV):
        parts += [z[g * HEAD_DIM:(g + 1) * HEAD_DIM, :], ones_row]
    return jnp.concatenate(parts, axis=0)


TOK_U, TOK_GL, TOK_KC, TOK_VC, TOK_KS, TOK_KW, TOK_END = 0, 1024, 2048, 2304, 2560, 2816, 3072
FEAT_Q, FEAT_VS, FEAT_VW, FEAT_GT, FEAT_END = 0, 1024, 1280, 1536, 1600
GATE_ROWS = 16


def _in_proj_body(seq, x_ref, g_ref, w_ref, wt_ref, u_ref, gl_ref, kc_ref, vc_ref, ks_ref, kw_ref,
                  qt_ref, vst_ref, vwt_ref, gtt_ref):
    tm = x_ref.shape[0]
    a = _rms(x_ref[...], g_ref[...]).astype(bf16)

    def tok(lo, hi):
        return _dot(a, w_ref[:, lo:hi])

    def feat(lo, hi):
        return _dot_nt(wt_ref[lo:hi, :], a)

    u_ref[...] = tok(TOK_U, TOK_GL)
    gl_ref[...] = tok(TOK_GL, TOK_KC)
    kc_ref[...] = tok(TOK_KC, TOK_VC)
    vc_ref[...] = tok(TOK_VC, TOK_KS)
    lane = lax.broadcasted_iota(jnp.int32, (tm, LANES), 1)
    pos = (pl.program_id(0) * tm) % seq + lax.broadcasted_iota(jnp.int32, (tm, LANES), 0)
    block_onehot = jnp.where(lane - MASK_LO == pos // SEL_BLOCK, 1.0, 0.0)
    ks_ref[...] = _spread_groups(tok(TOK_KS, TOK_KW), block_onehot).astype(bf16)
    kw_ref[...] = _spread_groups(tok(TOK_KW, TOK_END), 0.0).astype(bf16)

    qt_ref[...] = (feat(FEAT_Q, FEAT_VS) * (LOG2E * HEAD_DIM ** -0.5)).astype(bf16)
    vst_ref[...] = _value_rows(feat(FEAT_VS, FEAT_VW)).astype(bf16)
    vwt_ref[...] = _value_rows(feat(FEAT_VW, FEAT_GT)).astype(bf16)
    gtt_ref[...] = feat(FEAT_GT, FEAT_END)


def _in_proj(x2, g, w, wt, seq, tm=512):
    m = x2.shape[0]
    wide = N_KV * LANES
    tok_out = [(1024, f32), (1024, f32), (256, f32), (256, f32), (wide, bf16), (wide, bf16)]
    feat_out = [(N_HEADS * HEAD_DIM, bf16), (wide, bf16), (wide, bf16), (N_KV * GATE_ROWS, f32)]
    body = lambda *refs: _in_proj_body(seq, *refs)
    return pl.pallas_call(
        body,
        grid=(m // tm,),
        in_specs=[pl.BlockSpec((tm, D_MODEL), lambda i: (i, 0)),
                  _const_spec((1, D_MODEL)),
                  pl.BlockSpec((D_MODEL, TOK_END), lambda i: (0, 0), pipeline_mode=pl.Buffered(1)),
                  pl.BlockSpec((FEAT_END, D_MODEL), lambda i: (0, 0), pipeline_mode=pl.Buffered(1))],
        out_specs=([pl.BlockSpec((tm, n), lambda i: (i, 0)) for n, _ in tok_out]
                   + [pl.BlockSpec((n, tm), lambda i: (0, i)) for n, _ in feat_out]),
        out_shape=([jax.ShapeDtypeStruct((m, n), dt) for n, dt in tok_out]
                   + [jax.ShapeDtypeStruct((n, m), dt) for n, dt in feat_out]),
        compiler_params=_params(("parallel",)),
        name="in_proj",
    )(x2, g, w, wt)


def _rglru_body(u_ref, gl_ref, cw_ref, cb_ref, wa_ref, ba_ref, wx_ref, bx_ref, lam_ref, gn_ref,
                o_ref, ubuf, a_sc, b_sc, hc_sc):
    t = u_ref.shape[0]

    @pl.when(pl.program_id(1) == 0)
    def _():
        ubuf[...] = jnp.zeros_like(ubuf)
        hc_sc[...] = jnp.zeros_like(hc_sc)

    u = u_ref[...]
    u3 = u.reshape(t // 8, 8, LRU_WIDTH)
    tail = ubuf[...]
    cw = cw_ref[...]
    row8w = lax.broadcasted_iota(jnp.int32, (t // 8, 8, LRU_WIDTH), 1)
    xc3 = cb_ref[...] + u3 * cw[CONV_WIDTH - 1]
    for d in range(1, CONV_WIDTH):
        cur = pltpu.roll(u3, d, 1)
        prev = jnp.concatenate([pltpu.roll(tail, d, 0)[None], cur[:-1]], axis=0)
        xc3 = xc3 + jnp.where(row8w >= d, cur, prev) * cw[CONV_WIDTH - 1 - d]
    ubuf[...] = u[t - 8:t, :]
    xc = xc3.reshape(t, LRU_WIDTH)

    xb = xc.astype(bf16)
    sp = jax.nn.softplus(-lam_ref[...])
    row8 = lax.broadcasted_iota(jnp.int32, (t // 8, 8, 256), 1)
    for c in range(LRU_WIDTH // 256):
        sl = slice(c * 256, (c + 1) * 256)
        xcb = xb[:, sl]
        r = jax.nn.sigmoid(_dot(xcb, wa_ref[c]) + ba_ref[:, sl])
        ig = jax.nn.sigmoid(_dot(xcb, wx_ref[c]) + bx_ref[:, sl])
        log_a = (-LRU_C) * r * sp[:, sl]
        a = jnp.exp(log_a)
        b = jnp.sqrt(1.0 - a * a) * (ig * xc[:, sl])
        a = a.reshape(t // 8, 8, 256)
        b = b.reshape(t // 8, 8, 256)
        for d in (1, 2, 4):
            keep = row8 >= d
            a_prev = pltpu.roll(a, d, 1)
            b_prev = pltpu.roll(b, d, 1)
            b = jnp.where(keep, a * b_prev + b, b)
            a = jnp.where(keep, a * a_prev, a)
        a_sc[:, sl] = a.reshape(t, 256)
        b_sc[:, sl] = b.reshape(t, 256)

    def group(gi, h):
        off = pl.multiple_of(gi * 8, 8)
        hg = b_sc[pl.ds(off, 8), :] + a_sc[pl.ds(off, 8), :] * h
        b_sc[pl.ds(off, 8), :] = hg
        return hg[7:8, :]

    h_last = lax.fori_loop(0, t // 8, group, hc_sc[0:1, :])
    hc_sc[0:1, :] = h_last
    y = b_sc[...] * jax.nn.gelu(gl_ref[...])
    o_ref[...] = _rms(y, gn_ref[...]).astype(o_ref.dtype)


def _rglru(u, gl, cw, cb, wa, ba, wx, bx, lam, gn, batch, seq, t=256):
    ns = seq // t
    row = pl.BlockSpec((t, LRU_WIDTH), lambda b, s: (b * ns + s, 0))
    vec = _const_spec((1, LRU_WIDTH))
    wspec = _const_spec((LRU_WIDTH // 256, 256, 256))
    return pl.pallas_call(
        _rglru_body,
        grid=(batch, ns),
        in_specs=[row, row, _const_spec((CONV_WIDTH, LRU_WIDTH)), vec, wspec, vec, wspec, vec, vec, vec],
        out_specs=row,
        out_shape=jax.ShapeDtypeStruct((batch * seq, LRU_WIDTH), bf16),
        scratch_shapes=[pltpu.VMEM((8, LRU_WIDTH), f32), pltpu.VMEM((t, LRU_WIDTH), f32),
                        pltpu.VMEM((t, LRU_WIDTH), f32), pltpu.VMEM((8, LRU_WIDTH), f32)],
        compiler_params=_params(("parallel", "arbitrary")),
        name="rglru",
    )(u, gl, cw, cb, wa, ba, wx, bx, lam, gn)


def _compress_body(kc_ref, vc_ref, pek_ref, w1k_ref, w2k_ref, pev_ref, w1v_ref, w2v_ref, ko_ref, vo_ref):
    nrow = ko_ref.shape[2]
    rows = lax.broadcasted_iota(jnp.int32, (nrow, LANES), 0)
    cols = lax.broadcasted_iota(jnp.int32, (LANES, nrow), 1)
    lane64 = lax.broadcasted_iota(jnp.int32, (nrow, LANES), 1) < HEAD_DIM
    half = CMP_STRIDE * HEAD_DIM

    def one(t_ref, pe_ref, w1_ref, w2_ref, o_ref, transposed):
        pe = jnp.broadcast_to(pe_ref[...], (8, CMP_LEN * HEAD_DIM)).astype(bf16)
        c0 = _dot(pe, w1_ref[...])[0:1, :]
        toks = [t_ref[0, pl.ds(j, nrow, stride=CMP_STRIDE), :] for j in range(CMP_STRIDE)]
        for g in range(LANES // HEAD_DIM):
            blocks = []
            for i in range(CMP_STRIDE // 2):
                even, odd = toks[2 * i], toks[2 * i + 1]
                if g == 0:
                    blocks.append(jnp.where(lane64, even, pltpu.roll(odd, HEAD_DIM, 1)))
                else:
                    blocks.append(jnp.where(lane64, pltpu.roll(even, HEAD_DIM, 1), odd))
            chunk = jnp.concatenate(blocks, axis=1).astype(bf16)
            lo = _dot(chunk, w1_ref[0:half, :])
            hi = _dot(chunk, w1_ref[half:2 * half, :])
            hid = jax.nn.gelu(lo + pltpu.roll(hi, nrow - 1, 0) + c0)
            if transposed:
                out = _dot_nt(w2_ref[...], hid.astype(bf16))
                o_ref[0, g] = jnp.where(cols < nrow - 1, out, 0.0).astype(o_ref.dtype)
            else:
                out = _dot(hid.astype(bf16), w2_ref[...])
                o_ref[0, g] = jnp.where(rows < nrow - 1, out, 0.0).astype(o_ref.dtype)

    one(kc_ref, pek_ref, w1k_ref, w2k_ref, ko_ref, False)
    one(vc_ref, pev_ref, w1v_ref, w2v_ref, vo_ref, True)


def _compress(kc, vc, pek, w1k, w2k, pev, w1v, w2v):
    batch, seq, width = kc.shape
    nrow = seq // CMP_STRIDE
    pair = LANES // HEAD_DIM
    tspec = pl.BlockSpec((1, seq, LANES), lambda b, h: (b, 0, h))
    ospec = pl.BlockSpec((1, pair, nrow, LANES), lambda b, h: (b, h, 0, 0))
    wts = [_const_spec((1, CMP_LEN * HEAD_DIM)), _const_spec((CMP_LEN * HEAD_DIM, CMP_HIDDEN))]
    tspec_v = pl.BlockSpec((1, pair, LANES, nrow), lambda b, h: (b, h, 0, 0))
    return pl.pallas_call(
        _compress_body,
        grid=(batch, width // LANES),
        in_specs=([tspec, tspec] + wts + [_const_spec((CMP_HIDDEN, LANES))]
                  + wts + [_const_spec((LANES, CMP_HIDDEN))]),
        out_specs=[ospec, tspec_v],
        out_shape=[jax.ShapeDtypeStruct((batch, N_KV, nrow, LANES), bf16),
                   jax.ShapeDtypeStruct((batch, N_KV, LANES, nrow), bf16)],
        compiler_params=_params(("parallel", "parallel")),
        name="compress",
    )(kc, vc, pek, w1k, w2k, pev, w1v, w2v)


def _bucket_thresholds():
    n = np.arange(0, 4096)
    max_exact = N_BUCKETS // 2
    nf = np.maximum(n, 1).astype(np.float32)
    large = max_exact + (np.log(nf / np.float32(max_exact)) / np.float32(math.log(MAX_DISTANCE / max_exact))
                         * np.float32(N_BUCKETS - max_exact)).astype(np.int32)
    large = np.minimum(large, N_BUCKETS - 1)
    bucket = np.where(n < max_exact, n, large)
    assert np.all(np.diff(bucket) >= 0) and bucket[0] == 0 and bucket[-1] == N_BUCKETS - 1
    return [int(np.argmax(bucket >= k)) for k in range(N_BUCKETS)]


_BUCKET_THR = _bucket_thresholds()
assert _BUCKET_THR[-1] <= (N_SEL_BIAS - 1) * Q_TILE + 1


def _bias_of_dist(dist, ok, tab_ref, head):
    last = tab_ref[N_BUCKETS - 1, head]
    val = jnp.full(dist.shape, (tab_ref[0, head] - last) * LOG2E, f32)
    for k in range(1, N_BUCKETS - 1):
        val = jnp.where(dist >= _BUCKET_THR[k], (tab_ref[k, head] - last) * LOG2E, val)
    val = jnp.where(dist >= _BUCKET_THR[N_BUCKETS - 1], 0.0, val)
    return jnp.where(ok, val, NEG_INF)


def _bias_body(tab_ref, bc_ref, bs_ref, bw_ref):
    g = pl.program_id(0)
    qt = pl.program_id(1)
    ncmp = bc_ref.shape[2]
    for r in range(HEADS_PER_KV):
        head = g * HEADS_PER_KV + r
        cols = slice(r * Q_TILE, (r + 1) * Q_TILE)
        n = lax.broadcasted_iota(jnp.int32, (ncmp, Q_TILE), 0)
        i = lax.broadcasted_iota(jnp.int32, (ncmp, Q_TILE), 1)
        dist = qt * Q_TILE + i - (n * CMP_STRIDE + CMP_LEN - 1)
        bc_ref[0, 0, :, cols] = _bias_of_dist(dist, (dist >= 0) & (n < ncmp - 1), tab_ref, head)

    @pl.when(qt == 0)
    def _():
        j = lax.broadcasted_iota(jnp.int32, (Q_TILE, Q_TILE), 0)
        i = lax.broadcasted_iota(jnp.int32, (Q_TILE, Q_TILE), 1)
        for r in range(HEADS_PER_KV):
            head = g * HEADS_PER_KV + r
            cols = slice(r * Q_TILE, (r + 1) * Q_TILE)
            for d in range(N_SEL_BIAS):
                dist = d * Q_TILE + i - j
                bs_ref[0, d, :, cols] = _bias_of_dist(dist, dist >= 0, tab_ref, head)
            dist = WINDOW + i - j
            bw_ref[0, 0, :, cols] = _bias_of_dist(dist, dist < WINDOW, tab_ref, head)


def _bias_tiles(rel_bias, seq):
    nq = seq // Q_TILE
    ncmp = seq // CMP_STRIDE
    rows = HEADS_PER_KV * Q_TILE
    return pl.pallas_call(
        _bias_body,
        grid=(N_KV, nq),
        in_specs=[pl.BlockSpec(memory_space=pltpu.SMEM)],
        out_specs=[pl.BlockSpec((1, 1, ncmp, rows), lambda g, q: (g, q, 0, 0)),
                   pl.BlockSpec((1, N_SEL_BIAS, Q_TILE, rows), lambda g, q: (g, 0, 0, 0)),
                   pl.BlockSpec((1, 1, Q_TILE, rows), lambda g, q: (g, 0, 0, 0))],
        out_shape=[jax.ShapeDtypeStruct((N_KV, nq, ncmp, rows), f32),
                   jax.ShapeDtypeStruct((N_KV, N_SEL_BIAS, Q_TILE, rows), f32),
                   jax.ShapeDtypeStruct((N_KV, 1, Q_TILE, rows), f32)],
        compiler_params=_params(("parallel", "arbitrary")),
        name="bias_tiles",
    )(rel_bias)


def _importance_matrix_t(seq):
    n_cmp = (seq - CMP_LEN) // CMP_STRIDE + 1
    n_sel = seq // SEL_BLOCK
    ratio_sel = SEL_BLOCK // CMP_STRIDE
    ratio_cmp = CMP_LEN // CMP_STRIDE
    jj = np.arange(n_sel)[:, None, None]
    ci = ratio_sel * jj + np.arange(ratio_sel)[None, :, None] - np.arange(ratio_cmp)[None, None, :]
    jb = np.broadcast_to(jj, ci.shape)
    ok = (ci >= 0) & (ci < n_cmp)
    m = np.zeros((n_sel, seq // CMP_STRIDE), np.float32)
    np.add.at(m, (jb[ok], ci[ok]), 1.0)
    return m


def _scores(qt, k_ref, t, first_tile, biases):
    s = _dot(k_ref[0, first_tile * Q_TILE:(t + 1) * Q_TILE, :], qt)
    pieces = []
    for kt in range(first_tile, t + 1):
        piece = s[(kt - first_tile) * Q_TILE:(kt - first_tile + 1) * Q_TILE, :]
        if t - kt in biases:
            piece = piece + biases[t - kt]
        pieces.append(piece)
    return pieces, jnp.max(functools.reduce(jnp.maximum, pieces), axis=0, keepdims=True)


def _weighted_values(pieces, m, vt_ref, t, first_tile):
    p = jnp.concatenate([jnp.exp2(pc - m).astype(bf16) for pc in pieces], axis=0)
    acc = _dot(vt_ref[0:V_ROWS, first_tile * Q_TILE:(t + 1) * Q_TILE], p)
    return acc[0:HEAD_DIM, :], 1.0 / acc[MASK_LO:MASK_LO + 1, :]


def _nsa_body(qt_ref, kc_ref, vct_ref, ks_ref, vst_ref, kw_ref, vwt_ref, gtt_ref, bc_ref, bs_ref, bw_ref,
              mt_ref, o_ref):
    n_sel = mt_ref.shape[0]
    n_tiles = qt_ref.shape[1] // Q_TILE
    cols = HEADS_PER_KV * Q_TILE
    head_cols = [slice(r * Q_TILE, (r + 1) * Q_TILE) for r in range(HEADS_PER_KV)]
    kc = kc_ref[0, 0]
    vct = vct_ref[0, 0]
    mt = mt_ref[...]
    jj = lax.broadcasted_iota(jnp.int32, (n_sel, Q_TILE), 0)
    qcol = lax.broadcasted_iota(jnp.int32, (n_sel, Q_TILE), 1)
    zeros_tail = jnp.zeros((LANES - HEAD_DIM, Q_TILE), bf16)
    zeros_rest = jnp.zeros((LANES - HEAD_DIM - n_sel, Q_TILE), bf16)

    def compressed_scores(t):
        tok = slice(t * Q_TILE, (t + 1) * Q_TILE)
        heads = [qt_ref[r * HEAD_DIM:(r + 1) * HEAD_DIM, tok] for r in range(HEADS_PER_KV)]
        q_plain = jnp.concatenate([jnp.concatenate([hd, zeros_tail], axis=0) for hd in heads], axis=1)
        return heads, q_plain, _dot(kc, q_plain) + bc_ref[0, t]

    def select_stage(t, heads, q_plain, s):
        p = jnp.exp2(s - jnp.max(s, axis=0, keepdims=True))
        norm = 1.0 / jnp.sum(p, axis=0, keepdims=True)
        if (t + 1) * Q_TILE > CMP_LEN - 1 >= t * Q_TILE:
            pos = t * Q_TILE + (lax.broadcasted_iota(jnp.int32, (1, cols), 1) & (Q_TILE - 1))
            norm = jnp.where(pos >= CMP_LEN - 1, norm, 0.0)
        p = p * norm
        o_c = _dot(vct[0:HEAD_DIM, :], p.astype(bf16))

        n_causal = ((t + 1) * Q_TILE - 1) // SEL_BLOCK + 1
        if n_causal <= N_SELECT:
            return q_plain, q_plain, o_c

        psum = p[:, head_cols[0]] + p[:, head_cols[1]] + p[:, head_cols[2]] + p[:, head_cols[3]]
        p_hi = psum.astype(bf16)
        p_lo = (psum - p_hi.astype(f32)).astype(bf16)
        imp = _dot(mt, p_hi) + _dot(mt, p_lo)
        dblk = (t * Q_TILE + qcol) // SEL_BLOCK - jj
        forced = (jj == 0) | ((dblk >= 0) & (dblk < N_LOCAL_FORCED))
        imp = jnp.where(forced, FORCE_SCORE, jnp.where(dblk >= 0, imp, -FORCE_SCORE))
        rank = jnp.zeros((n_sel, Q_TILE), f32)
        for i in range(n_causal):
            row = imp[i:i + 1, :]
            ahead = (row > imp) | ((row == imp) & (jj > i))
            rank = rank + jnp.where(ahead, 1.0, 0.0)
        mask_rows = jnp.where(rank < N_SELECT, 0.0, NEG_INF).astype(bf16)
        q_masked = jnp.concatenate(
            [jnp.concatenate([hd, mask_rows, zeros_rest], axis=0) for hd in heads], axis=1)
        return q_plain, q_masked, o_c

    def score_stage(t, q_plain, q_masked, o_c):
        near = {d: bs_ref[0, d] for d in range(N_SEL_BIAS)}
        band = dict(near)
        band[WINDOW // Q_TILE] = bw_ref[0, 0]
        sel = _scores(q_masked, ks_ref, t, 0, near)
        win = _scores(q_plain, kw_ref, t, max(t - WINDOW // Q_TILE, 0), band)
        return sel, win, o_c

    def output_stage(t, sel, win, o_c):
        tok = slice(t * Q_TILE, (t + 1) * Q_TILE)
        acc_s, inv_s = _weighted_values(*sel, vst_ref, t, 0)
        acc_w, inv_w = _weighted_values(*win, vwt_ref, t, max(t - WINDOW // Q_TILE, 0))
        gt = jax.nn.sigmoid(gtt_ref[:, tok])
        gate = lambda br: jnp.concatenate([gt[3 * r + br:3 * r + br + 1, :] for r in range(HEADS_PER_KV)], axis=1)
        out_t = gate(0) * o_c + (gate(1) * inv_s) * acc_s + (gate(2) * inv_w) * acc_w
        pairs = [jnp.concatenate([out_t[:, head_cols[2 * h]], out_t[:, head_cols[2 * h + 1]]], axis=0).T
                 for h in range(HEADS_PER_KV // 2)]
        o_ref[0, tok, :] = jnp.concatenate(pairs, axis=-1)

    compressed = {t: compressed_scores(t) for t in range(min(3, n_tiles))}
    selected = {t: select_stage(t, *compressed.pop(t)) for t in range(min(2, n_tiles))}
    scored = {0: score_stage(0, *selected.pop(0))}
    for t in range(n_tiles):
        if t + 3 < n_tiles:
            compressed[t + 3] = compressed_scores(t + 3)
        if t + 1 < n_tiles:
            scored[t + 1] = score_stage(t + 1, *selected.pop(t + 1))
        if t + 2 < n_tiles:
            selected[t + 2] = select_stage(t + 2, *compressed.pop(t + 2))
        output_stage(t, *scored.pop(t))


def _nsa(qt, kc, vct, ks, vst, kw, vwt, gtt, bc, bs, bw, mt):
    batch, seq, _ = ks.shape
    nq = seq // Q_TILE
    cols = HEADS_PER_KV * Q_TILE
    ncmp = kc.shape[2]
    width = HEADS_PER_KV * HEAD_DIM
    k_spec = pl.BlockSpec((1, seq, LANES), lambda b, gi: (b, 0, gi))
    vt_spec = pl.BlockSpec((LANES, seq), lambda b, gi: (gi, b))
    return pl.pallas_call(
        _nsa_body,
        grid=(batch, N_KV),
        in_specs=[pl.BlockSpec((width, seq), lambda b, gi: (gi, b)),
                  pl.BlockSpec((1, 1, ncmp, LANES), lambda b, gi: (b, gi, 0, 0)),
                  pl.BlockSpec((1, 1, LANES, ncmp), lambda b, gi: (b, gi, 0, 0)),
                  k_spec, vt_spec, k_spec, vt_spec,
                  pl.BlockSpec((GATE_ROWS, seq), lambda b, gi: (gi, b)),
                  pl.BlockSpec((1, nq, ncmp, cols), lambda b, gi: (gi, 0, 0, 0)),
                  pl.BlockSpec((1, N_SEL_BIAS, Q_TILE, cols), lambda b, gi: (gi, 0, 0, 0)),
                  pl.BlockSpec((1, 1, Q_TILE, cols), lambda b, gi: (gi, 0, 0, 0)),
                  _const_spec(mt.shape)],
        out_specs=pl.BlockSpec((1, seq, width), lambda b, gi: (b, 0, gi)),
        out_shape=jax.ShapeDtypeStruct((batch, seq, N_KV * width), f32),
        compiler_params=_params(("parallel", "arbitrary")),
        name="nsa",
    )(qt, kc, vct, ks, vst, kw, vwt, gtt, bc, bs, bw, mt)


def _out_proj_body(x_ref, yl_ref, yn_ref, gn_ref, w_ref, gp_ref, gm_ref, o_ref, a_ref):
    yn = _rms(yn_ref[...], gn_ref[...]).astype(bf16)
    y = _dot(yl_ref[...], w_ref[0:LRU_WIDTH, :]) + _dot(yn, w_ref[LRU_WIDTH:, :])
    h = x_ref[...] + _rms(y, gp_ref[...])
    o_ref[...] = h
    a_ref[...] = _rms(h, gm_ref[...]).astype(bf16)


def _out_proj(x2, yl, yn, gn, w, gp, gm, tm=512):
    m = x2.shape[0]
    row = pl.BlockSpec((tm, D_MODEL), lambda i: (i, 0))
    return pl.pallas_call(
        _out_proj_body,
        grid=(m // tm,),
        in_specs=[row,
                  pl.BlockSpec((tm, LRU_WIDTH), lambda i: (i, 0)),
                  pl.BlockSpec((tm, D_MODEL - LRU_WIDTH), lambda i: (i, 0)),
                  _const_spec((1, D_MODEL - LRU_WIDTH)),
                  _const_spec((D_MODEL, D_MODEL)),
                  _const_spec((1, D_MODEL)), _const_spec((1, D_MODEL))],
        out_specs=[row, row],
        out_shape=[jax.ShapeDtypeStruct((m, D_MODEL), f32), jax.ShapeDtypeStruct((m, D_MODEL), bf16)],
        compiler_params=_params(("parallel",)),
        name="out_proj",
    )(x2, yl, yn, gn, w, gp, gm)


def _mlp_body(a_ref, w1_ref, w2_ref, g2_ref, o_ref, acc_sc):
    j = pl.program_id(1)

    @pl.when(j == 0)
    def _():
        acc_sc[...] = jnp.zeros_like(acc_sc)

    hid = jnp.maximum(_dot(a_ref[...], w1_ref[...]), 0.0)
    acc_sc[...] += _dot((hid * hid).astype(bf16), w2_ref[...])

    @pl.when(j == pl.num_programs(1) - 1)
    def _():
        o_ref[...] = _rms(acc_sc[...], g2_ref[...])


def _mlp(a, w1, w2, g2, tm=1024, tf=1024):
    m = a.shape[0]
    return pl.pallas_call(
        _mlp_body,
        grid=(m // tm, D_FF // tf),
        in_specs=[pl.BlockSpec((tm, D_MODEL), lambda i, j: (i, 0)),
                  pl.BlockSpec((D_MODEL, tf), lambda i, j: (0, j)),
                  pl.BlockSpec((tf, D_MODEL), lambda i, j: (j, 0)),
                  _const_spec((1, D_MODEL))],
        out_specs=pl.BlockSpec((tm, D_MODEL), lambda i, j: (i, 0)),
        out_shape=jax.ShapeDtypeStruct((m, D_MODEL), f32),
        scratch_shapes=[pltpu.VMEM((tm, D_MODEL), f32)],
        compiler_params=_params(("parallel", "arbitrary")),
        name="mlp",
    )(a, w1, w2, g2)


def _ple_body(h_ref, f_ref, p_ref, wg_ref, wp_ref, o_ref):
    h = h_ref[...] + f_ref[...]
    gate = jax.nn.sigmoid(_dot(h.astype(bf16), wg_ref[...]))
    o_ref[...] = h + gate * _dot(p_ref[...].astype(bf16), wp_ref[...])


def _ple(h, f, p2, wg, wp, tm=512):
    m = h.shape[0]
    row = pl.BlockSpec((tm, D_MODEL), lambda i: (i, 0))
    return pl.pallas_call(
        _ple_body,
        grid=(m // tm,),
        in_specs=[row, row,
                  pl.BlockSpec((tm, PLE_DIM), lambda i: (i, 0)),
                  _const_spec((D_MODEL, D_MODEL)),
                  _const_spec((PLE_DIM, D_MODEL))],
        out_specs=row,
        out_shape=jax.ShapeDtypeStruct((m, D_MODEL), f32),
        compiler_params=_params(("parallel",)),
        name="ple",
    )(h, f, p2, wg, wp)


def _block_diag_chunks(w):
    per = 256 // LRU_BLOCK_DIM
    w = w.reshape(LRU_BLOCKS // per, per, LRU_BLOCK_DIM, LRU_BLOCK_DIM)
    eye = jnp.eye(per, dtype=w.dtype)
    return jnp.einsum('cpij,pq->cpiqj', w, eye).reshape(LRU_BLOCKS // per, 256, 256)


def _layer(h, p_i, i, prm, bias):
    batch, seq, _ = h.shape
    m = batch * seq
    x2 = h.reshape(m, D_MODEL)
    row = lambda v: v.reshape(1, -1)
    per_batch = lambda t: t.reshape(batch, seq, t.shape[-1])

    w_in = prm["w_in"][i].astype(bf16)
    w_tok = jnp.concatenate([w_in[:, 0:2048], w_in[:, 3072:3840], w_in[:, 4096:4352]], axis=1)
    per = 3 * HEADS_PER_KV
    w_gate = jnp.pad(w_in[:, GATE_LO:IN_DIM].reshape(D_MODEL, N_KV, per), ((0, 0), (0, 0), (0, GATE_ROWS - per)))
    w_feat = jnp.concatenate([w_in[:, 2048:3072], w_in[:, 3840:4096], w_in[:, 4352:4608],
                              w_gate.reshape(D_MODEL, N_KV * GATE_ROWS)], axis=1).T
    u, gl, kc, vc, ks, kw, qt, vst, vwt, gtt = _in_proj(x2, row(prm["norm_mix_pre"][i]), w_tok, w_feat, seq)

    y_lru = _rglru(u, gl, prm["conv_w"][i], row(prm["conv_b"][i]),
                   _block_diag_chunks(prm["lru_wa"][i]).astype(bf16), row(prm["lru_ba"][i]),
                   _block_diag_chunks(prm["lru_wx"][i]).astype(bf16), row(prm["lru_bx"][i]),
                   row(prm["lru_lambda"][i]), row(prm["gnorm_lru"][i]), batch, seq)

    lane_pad = lambda w: jnp.pad(w, ((0, 0), (0, LANES - HEAD_DIM))).astype(bf16)
    kcc, vcct = _compress(per_batch(kc), per_batch(vc),
                          prm["cmp_pe_k"][i].reshape(1, -1), prm["cmp_w1_k"][i].astype(bf16),
                          lane_pad(prm["cmp_w2_k"][i]),
                          prm["cmp_pe_v"][i].reshape(1, -1), prm["cmp_w1_v"][i].astype(bf16),
                          lane_pad(prm["cmp_w2_v"][i]).T)

    bc, bs, bw = bias
    y_nsa = _nsa(qt, kcc, vcct, per_batch(ks), vst, per_batch(kw), vwt, gtt, bc, bs, bw,
                 jnp.asarray(_importance_matrix_t(seq)).astype(bf16))

    h1, a1 = _out_proj(x2, y_lru, y_nsa.reshape(m, -1), row(prm["gnorm_nsa"][i]),
                       prm["w_out"][i].astype(bf16), row(prm["norm_mix_post"][i]), row(prm["norm_mlp_pre"][i]))
    f = _mlp(a1, prm["mlp_w1"][i].astype(bf16), prm["mlp_w2"][i].astype(bf16), row(prm["norm_mlp_post"][i]))
    h3 = _ple(h1, f, p_i.reshape(m, PLE_DIM), prm["ple_gate"][i].astype(bf16), prm["ple_proj"][i].astype(bf16))
    return h3.reshape(batch, seq, D_MODEL)


def kernel(x, p, norm_mix_pre, norm_mix_post, norm_mlp_pre, norm_mlp_post, w_in, conv_w, conv_b, lru_wa, lru_ba, lru_wx, lru_bx, lru_lambda, cmp_pe_k, cmp_w1_k, cmp_w2_k, cmp_pe_v, cmp_w1_v, cmp_w2_v, rel_bias, gnorm_lru, gnorm_nsa, w_out, mlp_w1, mlp_w2, ple_gate, ple_proj):
    prm = dict(norm_mix_pre=norm_mix_pre, norm_mix_post=norm_mix_post, norm_mlp_pre=norm_mlp_pre,
               norm_mlp_post=norm_mlp_post, w_in=w_in, conv_w=conv_w, conv_b=conv_b, lru_wa=lru_wa,
               lru_ba=lru_ba, lru_wx=lru_wx, lru_bx=lru_bx, lru_lambda=lru_lambda, cmp_pe_k=cmp_pe_k,
               cmp_w1_k=cmp_w1_k, cmp_w2_k=cmp_w2_k, cmp_pe_v=cmp_pe_v, cmp_w1_v=cmp_w1_v,
               cmp_w2_v=cmp_w2_v, gnorm_lru=gnorm_lru, gnorm_nsa=gnorm_nsa, w_out=w_out, mlp_w1=mlp_w1,
               mlp_w2=mlp_w2, ple_gate=ple_gate, ple_proj=ple_proj)
    bias = _bias_tiles(rel_bias, x.shape[1])
    h = x
    for i in range(w_in.shape[0]):
        h = _layer(h, p[i], i, prm, bias)
    return h
```

```python
import functools
import math

import numpy as np
import jax
import jax.numpy as jnp
from jax import lax
from jax.experimental import pallas as pl
from jax.experimental.pallas import tpu as pltpu

D_MODEL = 2048
PLE_DIM = 256
LRU_WIDTH = 1024
LRU_BLOCKS = 16
LRU_BLOCK_DIM = 64
CONV_WIDTH = 4
LRU_C = 8.0
HEAD_DIM = 64
N_HEADS = 16
N_KV = 4
HEADS_PER_KV = 4
CMP_LEN = 32
CMP_STRIDE = 16
CMP_HIDDEN = 256
SEL_BLOCK = 64
N_SELECT = 16
N_LOCAL_FORCED = 2
WINDOW = 512
N_BUCKETS = 32
MAX_DISTANCE = 128
D_FF = 4 * D_MODEL
NORM_EPS = 1e-6
NEG_INF = -1e30
FORCE_SCORE = 1e4
IN_DIM = 4656
IN_PAD = 4736
GATE_LO = 4608

LANES = 128
MASK_LO = HEAD_DIM
Q_TILE = 128
N_SEL_BIAS = 2
V_ROWS = 80
LOG2E = math.log2(math.e)

VMEM_LIMIT = 56 * 1024 * 1024

f32 = jnp.float32
bf16 = jnp.bfloat16


def _rms(x, g):
    return x * lax.rsqrt(jnp.mean(x * x, axis=-1, keepdims=True) + NORM_EPS) * g


def _dot(a, b):
    return jnp.dot(a, b, preferred_element_type=f32)


def _dot_nt(a, b):
    return lax.dot_general(a, b, (((1,), (1,)), ((), ())), preferred_element_type=f32)


def _dot_tn(a, b):
    return lax.dot_general(a, b, (((0,), (0,)), ((), ())), preferred_element_type=f32)


def _const_spec(shape):
    nd = len(shape)
    return pl.BlockSpec(shape, lambda *_: (0,) * nd)


def _params(sem):
    return pltpu.CompilerParams(dimension_semantics=sem, vmem_limit_bytes=VMEM_LIMIT)


def _spread_groups(z, fill):
    lane = lax.broadcasted_iota(jnp.int32, (z.shape[0], LANES), 1)
    parts = []
    for g in range(N_KV):
        pair = z[:, (g // 2) * LANES:(g // 2 + 1) * LANES]
        if g % 2:
            pair = pltpu.roll(pair, HEAD_DIM, 1)
        parts.append(jnp.where(lane < HEAD_DIM, pair, fill))
    return jnp.concatenate(parts, axis=1)


def _value_rows(z):
    tm = z.shape[1]
    ones_row = jnp.where(lax.broadcasted_iota(jnp.int32, (LANES - HEAD_DIM, tm), 0) == 0, 1.0, 0.0)
    parts = []
    for g in range(N_KV):
        parts += [z[g * HEAD_DIM:(g + 1) * HEAD_DIM, :], ones_row]
    return jnp.concatenate(parts, axis=0)


TOK_U, TOK_GL, TOK_KC, TOK_VC, TOK_KS, TOK_KW, TOK_END = 0, 1024, 2048, 2304, 2560, 2816, 3072
FEAT_Q, FEAT_VS, FEAT_VW, FEAT_GT, FEAT_END = 0, 1024, 1280, 1536, 1600
GATE_ROWS = 16


def _in_proj_body(seq, x_ref, g_ref, w_ref, wt_ref, u_ref, gl_ref, kc_ref, vc_ref, ks_ref, kw_ref,
                  qt_ref, vst_ref, vwt_ref, gtt_ref):
    tm = x_ref.shape[0]
    a = _rms(x_ref[...], g_ref[...]).astype(bf16)

    def tok(lo, hi):
        return _dot(a, w_ref[:, lo:hi])

    def feat(lo, hi):
        return _dot_nt(wt_ref[lo:hi, :], a)

    u_ref[...] = tok(TOK_U, TOK_GL)
    gl_ref[...] = tok(TOK_GL, TOK_KC)
    kc_ref[...] = tok(TOK_KC, TOK_VC)
    vc_ref[...] = tok(TOK_VC, TOK_KS)
    lane = lax.broadcasted_iota(jnp.int32, (tm, LANES), 1)
    pos = (pl.program_id(0) * tm) % seq + lax.broadcasted_iota(jnp.int32, (tm, LANES), 0)
    block_onehot = jnp.where(lane - MASK_LO == pos // SEL_BLOCK, 1.0, 0.0)
    ks_ref[...] = _spread_groups(tok(TOK_KS, TOK_KW), block_onehot).astype(bf16)
    kw_ref[...] = _spread_groups(tok(TOK_KW, TOK_END), 0.0).astype(bf16)

    qt_ref[...] = (feat(FEAT_Q, FEAT_VS) * (LOG2E * HEAD_DIM ** -0.5)).astype(bf16)
    vst_ref[...] = _value_rows(feat(FEAT_VS, FEAT_VW)).astype(bf16)
    vwt_ref[...] = _value_rows(feat(FEAT_VW, FEAT_GT)).astype(bf16)
    gtt_ref[...] = feat(FEAT_GT, FEAT_END)


def _in_proj(x2, g, w, wt, seq, tm=512):
    m = x2.shape[0]
    wide = N_KV * LANES
    tok_out = [(1024, f32), (1024, f32), (256, f32), (256, f32), (wide, bf16), (wide, bf16)]
    feat_out = [(N_HEADS * HEAD_DIM, bf16), (wide, bf16), (wide, bf16), (N_KV * GATE_ROWS, f32)]
    body = lambda *refs: _in_proj_body(seq, *refs)
    return pl.pallas_call(
        body,
        grid=(m // tm,),
        in_specs=[pl.BlockSpec((tm, D_MODEL), lambda i: (i, 0)),
                  _const_spec((1, D_MODEL)),
                  pl.BlockSpec((D_MODEL, TOK_END), lambda i: (0, 0), pipeline_mode=pl.Buffered(1)),
                  pl.BlockSpec((FEAT_END, D_MODEL), lambda i: (0, 0), pipeline_mode=pl.Buffered(1))],
        out_specs=([pl.BlockSpec((tm, n), lambda i: (i, 0)) for n, _ in tok_out]
                   + [pl.BlockSpec((n, tm), lambda i: (0, i)) for n, _ in feat_out]),
        out_shape=([jax.ShapeDtypeStruct((m, n), dt) for n, dt in tok_out]
                   + [jax.ShapeDtypeStruct((n, m), dt) for n, dt in feat_out]),
        compiler_params=_params(("parallel",)),
        name="in_proj",
    )(x2, g, w, wt)


def _rglru_body(u_ref, gl_ref, cw_ref, cb_ref, wa_ref, ba_ref, wx_ref, bx_ref, lam_ref, gn_ref,
                o_ref, ubuf, a_sc, b_sc, hc_sc):
    t = u_ref.shape[0]

    @pl.when(pl.program_id(1) == 0)
    def _():
        ubuf[...] = jnp.zeros_like(ubuf)
        hc_sc[...] = jnp.zeros_like(hc_sc)

    u = u_ref[...]
    u3 = u.reshape(t // 8, 8, LRU_WIDTH)
    tail = ubuf[...]
    cw = cw_ref[...]
    row8w = lax.broadcasted_iota(jnp.int32, (t // 8, 8, LRU_WIDTH), 1)
    xc3 = cb_ref[...] + u3 * cw[CONV_WIDTH - 1]
    for d in range(1, CONV_WIDTH):
        cur = pltpu.roll(u3, d, 1)
        prev = jnp.concatenate([pltpu.roll(tail, d, 0)[None], cur[:-1]], axis=0)
        xc3 = xc3 + jnp.where(row8w >= d, cur, prev) * cw[CONV_WIDTH - 1 - d]
    ubuf[...] = u[t - 8:t, :]
    xc = xc3.reshape(t, LRU_WIDTH)

    xb = xc.astype(bf16)
    sp = jax.nn.softplus(-lam_ref[...])
    row8 = lax.broadcasted_iota(jnp.int32, (t // 8, 8, 256), 1)
    for c in range(LRU_WIDTH // 256):
        sl = slice(c * 256, (c + 1) * 256)
        xcb = xb[:, sl]
        r = jax.nn.sigmoid(_dot(xcb, wa_ref[c]) + ba_ref[:, sl])
        ig = jax.nn.sigmoid(_dot(xcb, wx_ref[c]) + bx_ref[:, sl])
        log_a = (-LRU_C) * r * sp[:, sl]
        a = jnp.exp(log_a)
        b = jnp.sqrt(1.0 - a * a) * (ig * xc[:, sl])
        a = a.reshape(t // 8, 8, 256)
        b = b.reshape(t // 8, 8, 256)
        for d in (1, 2, 4):
            keep = row8 >= d
            a_prev = pltpu.roll(a, d, 1)
            b_prev = pltpu.roll(b, d, 1)
            b = jnp.where(keep, a * b_prev + b, b)
            a = jnp.where(keep, a * a_prev, a)
        a_sc[:, sl] = a.reshape(t, 256)
        b_sc[:, sl] = b.reshape(t, 256)

    def group(gi, h):
        off = pl.multiple_of(gi * 8, 8)
        hg = b_sc[pl.ds(off, 8), :] + a_sc[pl.ds(off, 8), :] * h
        b_sc[pl.ds(off, 8), :] = hg
        return hg[7:8, :]

    h_last = lax.fori_loop(0, t // 8, group, hc_sc[0:1, :])
    hc_sc[0:1, :] = h_last
    y = b_sc[...] * jax.nn.gelu(gl_ref[...])
    o_ref[...] = _rms(y, gn_ref[...]).astype(o_ref.dtype)


def _rglru(u, gl, cw, cb, wa, ba, wx, bx, lam, gn, batch, seq, t=256):
    ns = seq // t
    row = pl.BlockSpec((t, LRU_WIDTH), lambda b, s: (b * ns + s, 0))
    vec = _const_spec((1, LRU_WIDTH))
    wspec = _const_spec((LRU_WIDTH // 256, 256, 256))
    return pl.pallas_call(
        _rglru_body,
        grid=(batch, ns),
        in_specs=[row, row, _const_spec((CONV_WIDTH, LRU_WIDTH)), vec, wspec, vec, wspec, vec, vec, vec],
        out_specs=row,
        out_shape=jax.ShapeDtypeStruct((batch * seq, LRU_WIDTH), bf16),
        scratch_shapes=[pltpu.VMEM((8, LRU_WIDTH), f32), pltpu.VMEM((t, LRU_WIDTH), f32),
                        pltpu.VMEM((t, LRU_WIDTH), f32), pltpu.VMEM((8, LRU_WIDTH), f32)],
        compiler_params=_params(("parallel", "arbitrary")),
        name="rglru",
    )(u, gl, cw, cb, wa, ba, wx, bx, lam, gn)


def _compress_body(kc_ref, vc_ref, pek_ref, w1k_ref, w2k_ref, pev_ref, w1v_ref, w2v_ref, ko_ref, vo_ref):
    nrow = ko_ref.shape[2]
    rows = lax.broadcasted_iota(jnp.int32, (nrow, LANES), 0)
    cols = lax.broadcasted_iota(jnp.int32, (LANES, nrow), 1)
    lane64 = lax.broadcasted_iota(jnp.int32, (nrow, LANES), 1) < HEAD_DIM
    half = CMP_STRIDE * HEAD_DIM

    def one(t_ref, pe_ref, w1_ref, w2_ref, o_ref, transposed):
        pe = jnp.broadcast_to(pe_ref[...], (8, CMP_LEN * HEAD_DIM)).astype(bf16)
        c0 = _dot(pe, w1_ref[...])[0:1, :]
        toks = [t_ref[0, pl.ds(j, nrow, stride=CMP_STRIDE), :] for j in range(CMP_STRIDE)]
        for g in range(LANES // HEAD_DIM):
            blocks = []
            for i in range(CMP_STRIDE // 2):
                even, odd = toks[2 * i], toks[2 * i + 1]
                if g == 0:
                    blocks.append(jnp.where(lane64, even, pltpu.roll(odd, HEAD_DIM, 1)))
                else:
                    blocks.append(jnp.where(lane64, pltpu.roll(even, HEAD_DIM, 1), odd))
            chunk = jnp.concatenate(blocks, axis=1).astype(bf16)
            lo = _dot(chunk, w1_ref[0:half, :])
            hi = _dot(chunk, w1_ref[half:2 * half, :])
            hid = jax.nn.gelu(lo + pltpu.roll(hi, nrow - 1, 0) + c0)
            if transposed:
                out = _dot_nt(w2_ref[...], hid.astype(bf16))
                o_ref[0, g] = jnp.where(cols < nrow - 1, out, 0.0).astype(o_ref.dtype)
            else:
                out = _dot(hid.astype(bf16), w2_ref[...])
                o_ref[0, g] = jnp.where(rows < nrow - 1, out, 0.0).astype(o_ref.dtype)

    one(kc_ref, pek_ref, w1k_ref, w2k_ref, ko_ref, False)
    one(vc_ref, pev_ref, w1v_ref, w2v_ref, vo_ref, True)


def _compress(kc, vc, pek, w1k, w2k, pev, w1v, w2v):
    batch, seq, width = kc.shape
    nrow = seq // CMP_STRIDE
    pair = LANES // HEAD_DIM
    tspec = pl.BlockSpec((1, seq, LANES), lambda b, h: (b, 0, h))
    ospec = pl.BlockSpec((1, pair, nrow, LANES), lambda b, h: (b, h, 0, 0))
    wts = [_const_spec((1, CMP_LEN * HEAD_DIM)), _const_spec((CMP_LEN * HEAD_DIM, CMP_HIDDEN))]
    tspec_v = pl.BlockSpec((1, pair, LANES, nrow), lambda b, h: (b, h, 0, 0))
    return pl.pallas_call(
        _compress_body,
        grid=(batch, width // LANES),
        in_specs=([tspec, tspec] + wts + [_const_spec((CMP_HIDDEN, LANES))]
                  + wts + [_const_spec((LANES, CMP_HIDDEN))]),
        out_specs=[ospec, tspec_v],
        out_shape=[jax.ShapeDtypeStruct((batch, N_KV, nrow, LANES), bf16),
                   jax.ShapeDtypeStruct((batch, N_KV, LANES, nrow), bf16)],
        compiler_params=_params(("parallel", "parallel")),
        name="compress",
    )(kc, vc, pek, w1k, w2k, pev, w1v, w2v)


def _bucket_thresholds():
    n = np.arange(0, 4096)
    max_exact = N_BUCKETS // 2
    nf = np.maximum(n, 1).astype(np.float32)
    large = max_exact + (np.log(nf / np.float32(max_exact)) / np.float32(math.log(MAX_DISTANCE / max_exact))
                         * np.float32(N_BUCKETS - max_exact)).astype(np.int32)
    large = np.minimum(large, N_BUCKETS - 1)
    bucket = np.where(n < max_exact, n, large)
    assert np.all(np.diff(bucket) >= 0) and bucket[0] == 0 and bucket[-1] == N_BUCKETS - 1
    return [int(np.argmax(bucket >= k)) for k in range(N_BUCKETS)]


_BUCKET_THR = _bucket_thresholds()
assert _BUCKET_THR[-1] <= (N_SEL_BIAS - 1) * Q_TILE + 1


def _bias_of_dist(dist, ok, tab_ref, head):
    last = tab_ref[N_BUCKETS - 1, head]
    val = jnp.full(dist.shape, (tab_ref[0, head] - last) * LOG2E, f32)
    for k in range(1, N_BUCKETS - 1):
        val = jnp.where(dist >= _BUCKET_THR[k], (tab_ref[k, head] - last) * LOG2E, val)
    val = jnp.where(dist >= _BUCKET_THR[N_BUCKETS - 1], 0.0, val)
    return jnp.where(ok, val, NEG_INF)


def _bias_body(tab_ref, bc_ref, bs_ref, bw_ref):
    g = pl.program_id(0)
    qt = pl.program_id(1)
    ncmp = bc_ref.shape[2]
    for r in range(HEADS_PER_KV):
        head = g * HEADS_PER_KV + r
        cols = slice(r * Q_TILE, (r + 1) * Q_TILE)
        n = lax.broadcasted_iota(jnp.int32, (ncmp, Q_TILE), 0)
        i = lax.broadcasted_iota(jnp.int32, (ncmp, Q_TILE), 1)
        dist = qt * Q_TILE + i - (n * CMP_STRIDE + CMP_LEN - 1)
        bc_ref[0, 0, :, cols] = _bias_of_dist(dist, (dist >= 0) & (n < ncmp - 1), tab_ref, head)

    @pl.when(qt == 0)
    def _():
        j = lax.broadcasted_iota(jnp.int32, (Q_TILE, Q_TILE), 0)
        i = lax.broadcasted_iota(jnp.int32, (Q_TILE, Q_TILE), 1)
        for r in range(HEADS_PER_KV):
            head = g * HEADS_PER_KV + r
            cols = slice(r * Q_TILE, (r + 1) * Q_TILE)
            for d in range(N_SEL_BIAS):
                dist = d * Q_TILE + i - j
                bs_ref[0, d, :, cols] = _bias_of_dist(dist, dist >= 0, tab_ref, head)
            dist = WINDOW + i - j
            bw_ref[0, 0, :, cols] = _bias_of_dist(dist, dist < WINDOW, tab_ref, head)


def _bias_tiles(rel_bias, seq):
    nq = seq // Q_TILE
    ncmp = seq // CMP_STRIDE
    rows = HEADS_PER_KV * Q_TILE
    return pl.pallas_call(
        _bias_body,
        grid=(N_KV, nq),
        in_specs=[pl.BlockSpec(memory_space=pltpu.SMEM)],
        out_specs=[pl.BlockSpec((1, 1, ncmp, rows), lambda g, q: (g, q, 0, 0)),
                   pl.BlockSpec((1, N_SEL_BIAS, Q_TILE, rows), lambda g, q: (g, 0, 0, 0)),
                   pl.BlockSpec((1, 1, Q_TILE, rows), lambda g, q: (g, 0, 0, 0))],
        out_shape=[jax.ShapeDtypeStruct((N_KV, nq, ncmp, rows), f32),
                   jax.ShapeDtypeStruct((N_KV, N_SEL_BIAS, Q_TILE, rows), f32),
                   jax.ShapeDtypeStruct((N_KV, 1, Q_TILE, rows), f32)],
        compiler_params=_params(("parallel", "arbitrary")),
        name="bias_tiles",
    )(rel_bias)


def _importance_matrix_t(seq):
    n_cmp = (seq - CMP_LEN) // CMP_STRIDE + 1
    n_sel = seq // SEL_BLOCK
    ratio_sel = SEL_BLOCK // CMP_STRIDE
    ratio_cmp = CMP_LEN // CMP_STRIDE
    jj = np.arange(n_sel)[:, None, None]
    ci = ratio_sel * jj + np.arange(ratio_sel)[None, :, None] - np.arange(ratio_cmp)[None, None, :]
    jb = np.broadcast_to(jj, ci.shape)
    ok = (ci >= 0) & (ci < n_cmp)
    m = np.zeros((n_sel, seq // CMP_STRIDE), np.float32)
    np.add.at(m, (jb[ok], ci[ok]), 1.0)
    return m


def _scores(qt, k_ref, t, first_tile, biases):
    s = _dot(k_ref[0, first_tile * Q_TILE:(t + 1) * Q_TILE, :], qt)
    pieces = []
    for kt in range(first_tile, t + 1):
        piece = s[(kt - first_tile) * Q_TILE:(kt - first_tile + 1) * Q_TILE, :]
        if t - kt in biases:
            piece = piece + biases[t - kt]
        pieces.append(piece)
    return pieces, jnp.max(functools.reduce(jnp.maximum, pieces), axis=0, keepdims=True)


def _weighted_values(pieces, m, vt_ref, t, first_tile):
    p = jnp.concatenate([jnp.exp2(pc - m).astype(bf16) for pc in pieces], axis=0)
    acc = _dot(vt_ref[0:V_ROWS, first_tile * Q_TILE:(t + 1) * Q_TILE], p)
    return acc[0:HEAD_DIM, :], 1.0 / acc[MASK_LO:MASK_LO + 1, :]


def _nsa_body(qt_ref, kc_ref, vct_ref, ks_ref, vst_ref, kw_ref, vwt_ref, gtt_ref, bc_ref, bs_ref, bw_ref,
              mt_ref, o_ref):
    n_sel = mt_ref.shape[0]
    n_tiles = qt_ref.shape[1] // Q_TILE
    cols = HEADS_PER_KV * Q_TILE
    head_cols = [slice(r * Q_TILE, (r + 1) * Q_TILE) for r in range(HEADS_PER_KV)]
    kc = kc_ref[0, 0]
    vct = vct_ref[0, 0]
    mt = mt_ref[...]
    jj = lax.broadcasted_iota(jnp.int32, (n_sel, Q_TILE), 0)
    qcol = lax.broadcasted_iota(jnp.int32, (n_sel, Q_TILE), 1)
    zeros_tail = jnp.zeros((LANES - HEAD_DIM, Q_TILE), bf16)
    zeros_rest = jnp.zeros((LANES - HEAD_DIM - n_sel, Q_TILE), bf16)

    def compressed_scores(t):
        tok = slice(t * Q_TILE, (t + 1) * Q_TILE)
        heads = [qt_ref[r * HEAD_DIM:(r + 1) * HEAD_DIM, tok] for r in range(HEADS_PER_KV)]
        q_plain = jnp.concatenate([jnp.concatenate([hd, zeros_tail], axis=0) for hd in heads], axis=1)
        return heads, q_plain, _dot(kc, q_plain) + bc_ref[0, t]

    def select_stage(t, heads, q_plain, s):
        p = jnp.exp2(s - jnp.max(s, axis=0, keepdims=True))
        norm = 1.0 / jnp.sum(p, axis=0, keepdims=True)
        if (t + 1) * Q_TILE > CMP_LEN - 1 >= t * Q_TILE:
            pos = t * Q_TILE + (lax.broadcasted_iota(jnp.int32, (1, cols), 1) & (Q_TILE - 1))
            norm = jnp.where(pos >= CMP_LEN - 1, norm, 0.0)
        p = p * norm
        o_c = _dot(vct[0:HEAD_DIM, :], p.astype(bf16))

        n_causal = ((t + 1) * Q_TILE - 1) // SEL_BLOCK + 1
        if n_causal <= N_SELECT:
            return q_plain, q_plain, o_c

        psum = p[:, head_cols[0]] + p[:, head_cols[1]] + p[:, head_cols[2]] + p[:, head_cols[3]]
        p_hi = psum.astype(bf16)
        p_lo = (psum - p_hi.astype(f32)).astype(bf16)
        imp = _dot(mt, p_hi) + _dot(mt, p_lo)
        dblk = (t * Q_TILE + qcol) // SEL_BLOCK - jj
        forced = (jj == 0) | ((dblk >= 0) & (dblk < N_LOCAL_FORCED))
        imp = jnp.where(forced, FORCE_SCORE, jnp.where(dblk >= 0, imp, -FORCE_SCORE))
        rank = jnp.zeros((n_sel, Q_TILE), f32)
        for i in range(n_causal):
            row = imp[i:i + 1, :]
            ahead = (row > imp) | ((row == imp) & (jj > i))
            rank = rank + jnp.where(ahead, 1.0, 0.0)
        mask_rows = jnp.where(rank < N_SELECT, 0.0, NEG_INF).astype(bf16)
        q_masked = jnp.concatenate(
            [jnp.concatenate([hd, mask_rows, zeros_rest], axis=0) for hd in heads], axis=1)
        return q_plain, q_masked, o_c

    first_win = lambda t: max(t - WINDOW // Q_TILE, 0)

    def sel_scores(t, q_masked):
        return _scores(q_masked, ks_ref, t, 0, {d: bs_ref[0, d] for d in range(N_SEL_BIAS)})

    def win_scores(t, q_plain):
        band = {d: bs_ref[0, d] for d in range(N_SEL_BIAS)}
        band[WINDOW // Q_TILE] = bw_ref[0, 0]
        return _scores(q_plain, kw_ref, t, first_win(t), band)

    def finish(t, o_c, sel_out, win_out):
        tok = slice(t * Q_TILE, (t + 1) * Q_TILE)
        acc_s, inv_s = sel_out
        acc_w, inv_w = win_out
        gt = jax.nn.sigmoid(gtt_ref[:, tok])
        gate = lambda br: jnp.concatenate([gt[3 * r + br:3 * r + br + 1, :] for r in range(HEADS_PER_KV)], axis=1)
        out_t = gate(0) * o_c + (gate(1) * inv_s) * acc_s + (gate(2) * inv_w) * acc_w
        pairs = [jnp.concatenate([out_t[:, head_cols[2 * h]], out_t[:, head_cols[2 * h + 1]]], axis=0).T
                 for h in range(HEADS_PER_KV // 2)]
        o_ref[0, tok, :] = jnp.concatenate(pairs, axis=-1)

    compressed = {t: compressed_scores(t) for t in range(min(3, n_tiles))}
    selected = {t: select_stage(t, *compressed.pop(t)) for t in range(min(2, n_tiles))}
    s_sel = {0: sel_scores(0, selected[0][1])}
    s_win = {0: win_scores(0, selected[0][0])}
    for t in range(n_tiles):
        o_c = selected.pop(t)[2]
        if t + 3 < n_tiles:
            compressed[t + 3] = compressed_scores(t + 3)
        if t + 1 < n_tiles:
            s_sel[t + 1] = sel_scores(t + 1, selected[t + 1][1])
        win_out = _weighted_values(*s_win.pop(t), vwt_ref, t, first_win(t))
        if t + 1 < n_tiles:
            s_win[t + 1] = win_scores(t + 1, selected[t + 1][0])
        if t + 2 < n_tiles:
            selected[t + 2] = select_stage(t + 2, *compressed.pop(t + 2))
        sel_out = _weighted_values(*s_sel.pop(t), vst_ref, t, 0)
        finish(t, o_c, sel_out, win_out)


def _nsa(qt, kc, vct, ks, vst, kw, vwt, gtt, bc, bs, bw, mt):
    batch, seq, _ = ks.shape
    nq = seq // Q_TILE
    cols = HEADS_PER_KV * Q_TILE
    ncmp = kc.shape[2]
    width = HEADS_PER_KV * HEAD_DIM
    k_spec = pl.BlockSpec((1, seq, LANES), lambda b, gi: (b, 0, gi))
    vt_spec = pl.BlockSpec((LANES, seq), lambda b, gi: (gi, b))
    return pl.pallas_call(
        _nsa_body,
        grid=(batch, N_KV),
        in_specs=[pl.BlockSpec((width, seq), lambda b, gi: (gi, b)),
                  pl.BlockSpec((1, 1, ncmp, LANES), lambda b, gi: (b, gi, 0, 0)),
                  pl.BlockSpec((1, 1, LANES, ncmp), lambda b, gi: (b, gi, 0, 0)),
                  k_spec, vt_spec, k_spec, vt_spec,
                  pl.BlockSpec((GATE_ROWS, seq), lambda b, gi: (gi, b)),
                  pl.BlockSpec((1, nq, ncmp, cols), lambda b, gi: (gi, 0, 0, 0)),
                  pl.BlockSpec((1, N_SEL_BIAS, Q_TILE, cols), lambda b, gi: (gi, 0, 0, 0)),
                  pl.BlockSpec((1, 1, Q_TILE, cols), lambda b, gi: (gi, 0, 0, 0)),
                  _const_spec(mt.shape)],
        out_specs=pl.BlockSpec((1, seq, width), lambda b, gi: (b, 0, gi)),
        out_shape=jax.ShapeDtypeStruct((batch, seq, N_KV * width), f32),
        compiler_params=_params(("parallel", "arbitrary")),
        name="nsa",
    )(qt, kc, vct, ks, vst, kw, vwt, gtt, bc, bs, bw, mt)


def _out_proj_body(x_ref, yl_ref, yn_ref, gn_ref, w_ref, gp_ref, gm_ref, o_ref, a_ref):
    yn = _rms(yn_ref[...], gn_ref[...]).astype(bf16)
    y = _dot(yl_ref[...], w_ref[0:LRU_WIDTH, :]) + _dot(yn, w_ref[LRU_WIDTH:, :])
    h = x_ref[...] + _rms(y, gp_ref[...])
    o_ref[...] = h
    a_ref[...] = _rms(h, gm_ref[...]).astype(bf16)


def _out_proj(x2, yl, yn, gn, w, gp, gm, tm=512):
    m = x2.shape[0]
    row = pl.BlockSpec((tm, D_MODEL), lambda i: (i, 0))
    return pl.pallas_call(
        _out_proj_body,
        grid=(m // tm,),
        in_specs=[row,
                  pl.BlockSpec((tm, LRU_WIDTH), lambda i: (i, 0)),
                  pl.BlockSpec((tm, D_MODEL - LRU_WIDTH), lambda i: (i, 0)),
                  _const_spec((1, D_MODEL - LRU_WIDTH)),
                  _const_spec((D_MODEL, D_MODEL)),
                  _const_spec((1, D_MODEL)), _const_spec((1, D_MODEL))],
        out_specs=[row, row],
        out_shape=[jax.ShapeDtypeStruct((m, D_MODEL), f32), jax.ShapeDtypeStruct((m, D_MODEL), bf16)],
        compiler_params=_params(("parallel",)),
        name="out_proj",
    )(x2, yl, yn, gn, w, gp, gm)


def _mlp_body(a_ref, w1_ref, w2_ref, g2_ref, o_ref, acc_sc):
    j = pl.program_id(1)

    @pl.when(j == 0)
    def _():
        acc_sc[...] = jnp.zeros_like(acc_sc)

    hid = jnp.maximum(_dot(a_ref[...], w1_ref[...]), 0.0)
    acc_sc[...] += _dot((hid * hid).astype(bf16), w2_ref[...])

    @pl.when(j == pl.num_programs(1) - 1)
    def _():
        o_ref[...] = _rms(acc_sc[...], g2_ref[...])


def _mlp(a, w1, w2, g2, tm=1024, tf=1024):
    m = a.shape[0]
    return pl.pallas_call(
        _mlp_body,
        grid=(m // tm, D_FF // tf),
        in_specs=[pl.BlockSpec((tm, D_MODEL), lambda i, j: (i, 0)),
                  pl.BlockSpec((D_MODEL, tf), lambda i, j: (0, j)),
                  pl.BlockSpec((tf, D_MODEL), lambda i, j: (j, 0)),
                  _const_spec((1, D_MODEL))],
        out_specs=pl.BlockSpec((tm, D_MODEL), lambda i, j: (i, 0)),
        out_shape=jax.ShapeDtypeStruct((m, D_MODEL), f32),
        scratch_shapes=[pltpu.VMEM((tm, D_MODEL), f32)],
        compiler_params=_params(("parallel", "arbitrary")),
        name="mlp",
    )(a, w1, w2, g2)


def _ple_body(h_ref, f_ref, p_ref, wg_ref, wp_ref, o_ref):
    h = h_ref[...] + f_ref[...]
    gate = jax.nn.sigmoid(_dot(h.astype(bf16), wg_ref[...]))
    o_ref[...] = h + gate * _dot(p_ref[...].astype(bf16), wp_ref[...])


def _ple(h, f, p2, wg, wp, tm=512):
    m = h.shape[0]
    row = pl.BlockSpec((tm, D_MODEL), lambda i: (i, 0))
    return pl.pallas_call(
        _ple_body,
        grid=(m // tm,),
        in_specs=[row, row,
                  pl.BlockSpec((tm, PLE_DIM), lambda i: (i, 0)),
                  _const_spec((D_MODEL, D_MODEL)),
                  _const_spec((PLE_DIM, D_MODEL))],
        out_specs=row,
        out_shape=jax.ShapeDtypeStruct((m, D_MODEL), f32),
        compiler_params=_params(("parallel",)),
        name="ple",
    )(h, f, p2, wg, wp)


def _block_diag_chunks(w):
    per = 256 // LRU_BLOCK_DIM
    w = w.reshape(LRU_BLOCKS // per, per, LRU_BLOCK_DIM, LRU_BLOCK_DIM)
    eye = jnp.eye(per, dtype=w.dtype)
    return jnp.einsum('cpij,pq->cpiqj', w, eye).reshape(LRU_BLOCKS // per, 256, 256)


def _layer(h, p_i, i, prm, bias):
    batch, seq, _ = h.shape
    m = batch * seq
    x2 = h.reshape(m, D_MODEL)
    row = lambda v: v.reshape(1, -1)
    per_batch = lambda t: t.reshape(batch, seq, t.shape[-1])

    w_in = prm["w_in"][i].astype(bf16)
    w_tok = jnp.concatenate([w_in[:, 0:2048], w_in[:, 3072:3840], w_in[:, 4096:4352]], axis=1)
    per = 3 * HEADS_PER_KV
    w_gate = jnp.pad(w_in[:, GATE_LO:IN_DIM].reshape(D_MODEL, N_KV, per), ((0, 0), (0, 0), (0, GATE_ROWS - per)))
    w_feat = jnp.concatenate([w_in[:, 2048:3072], w_in[:, 3840:4096], w_in[:, 4352:4608],
                              w_gate.reshape(D_MODEL, N_KV * GATE_ROWS)], axis=1).T
    u, gl, kc, vc, ks, kw, qt, vst, vwt, gtt = _in_proj(x2, row(prm["norm_mix_pre"][i]), w_tok, w_feat, seq)

    y_lru = _rglru(u, gl, prm["conv_w"][i], row(prm["conv_b"][i]),
                   _block_diag_chunks(prm["lru_wa"][i]).astype(bf16), row(prm["lru_ba"][i]),
                   _block_diag_chunks(prm["lru_wx"][i]).astype(bf16), row(prm["lru_bx"][i]),
                   row(prm["lru_lambda"][i]), row(prm["gnorm_lru"][i]), batch, seq)

    lane_pad = lambda w: jnp.pad(w, ((0, 0), (0, LANES - HEAD_DIM))).astype(bf16)
    kcc, vcct = _compress(per_batch(kc), per_batch(vc),
                          prm["cmp_pe_k"][i].reshape(1, -1), prm["cmp_w1_k"][i].astype(bf16),
                          lane_pad(prm["cmp_w2_k"][i]),
                          prm["cmp_pe_v"][i].reshape(1, -1), prm["cmp_w1_v"][i].astype(bf16),
                          lane_pad(prm["cmp_w2_v"][i]).T)

    bc, bs, bw = bias
    y_nsa = _nsa(qt, kcc, vcct, per_batch(ks), vst, per_batch(kw), vwt, gtt, bc, bs, bw,
                 jnp.asarray(_importance_matrix_t(seq)).astype(bf16))

    h1, a1 = _out_proj(x2, y_lru, y_nsa.reshape(m, -1), row(prm["gnorm_nsa"][i]),
                       prm["w_out"][i].astype(bf16), row(prm["norm_mix_post"][i]), row(prm["norm_mlp_pre"][i]))
    f = _mlp(a1, prm["mlp_w1"][i].astype(bf16), prm["mlp_w2"][i].astype(bf16), row(prm["norm_mlp_post"][i]))
    h3 = _ple(h1, f, p_i.reshape(m, PLE_DIM), prm["ple_gate"][i].astype(bf16), prm["ple_proj"][i].astype(bf16))
    return h3.reshape(batch, seq, D_MODEL)


def kernel(x, p, norm_mix_pre, norm_mix_post, norm_mlp_pre, norm_mlp_post, w_in, conv_w, conv_b, lru_wa, lru_ba, lru_wx, lru_bx, lru_lambda, cmp_pe_k, cmp_w1_k, cmp_w2_k, cmp_pe_v, cmp_w1_v, cmp_w2_v, rel_bias, gnorm_lru, gnorm_nsa, w_out, mlp_w1, mlp_w2, ple_gate, ple_proj):
    prm = dict(norm_mix_pre=norm_mix_pre, norm_mix_post=norm_mix_post, norm_mlp_pre=norm_mlp_pre,
               norm_mlp_post=norm_mlp_post, w_in=w_in, conv_w=conv_w, conv_b=conv_b, lru_wa=lru_wa,
               lru_ba=lru_ba, lru_wx=lru_wx, lru_bx=lru_bx, lru_lambda=lru_lambda, cmp_pe_k=cmp_pe_k,
               cmp_w1_k=cmp_w1_k, cmp_w2_k=cmp_w2_k, cmp_pe_v=cmp_pe_v, cmp_w1_v=cmp_w1_v,
               cmp_w2_v=cmp_w2_v, gnorm_lru=gnorm_lru, gnorm_nsa=gnorm_nsa, w_out=w_out, mlp_w1=mlp_w1,
               mlp_w2=mlp_w2, ple_gate=ple_gate, ple_proj=ple_proj)
    bias = _bias_tiles(rel_bias, x.shape[1])
    h = x
    for i in range(w_in.shape[0]):
        h = _layer(h, p[i], i, prm, bias)
    return h
```

```python
import functools
import math

import numpy as np
import jax
import jax.numpy as jnp
from jax import lax
from jax.experimental import pallas as pl
from jax.experimental.pallas import tpu as pltpu

D_MODEL = 2048
PLE_DIM = 256
LRU_WIDTH = 1024
LRU_BLOCKS = 16
LRU_BLOCK_DIM = 64
CONV_WIDTH = 4
LRU_C = 8.0
HEAD_DIM = 64
N_HEADS = 16
N_KV = 4
HEADS_PER_KV = 4
CMP_LEN = 32
CMP_STRIDE = 16
CMP_HIDDEN = 256
SEL_BLOCK = 64
N_SELECT = 16
N_LOCAL_FORCED = 2
WINDOW = 512
N_BUCKETS = 32
MAX_DISTANCE = 128
D_FF = 4 * D_MODEL
NORM_EPS = 1e-6
NEG_INF = -1e30
FORCE_SCORE = 1e4
IN_DIM = 4656
IN_PAD = 4736
GATE_LO = 4608

LANES = 128
MASK_LO = HEAD_DIM
Q_TILE = 128
N_SEL_BIAS = 2
V_ROWS = 80
LOG2E = math.log2(math.e)

VMEM_LIMIT = 56 * 1024 * 1024

f32 = jnp.float32
bf16 = jnp.bfloat16


def _rms(x, g):
    return x * lax.rsqrt(jnp.mean(x * x, axis=-1, keepdims=True) + NORM_EPS) * g


def _dot(a, b):
    return jnp.dot(a, b, preferred_element_type=f32)


def _dot_nt(a, b):
    return lax.dot_general(a, b, (((1,), (1,)), ((), ())), preferred_element_type=f32)


def _dot_tn(a, b):
    return lax.dot_general(a, b, (((0,), (0,)), ((), ())), preferred_element_type=f32)


def _const_spec(shape):
    nd = len(shape)
    return pl.BlockSpec(shape, lambda *_: (0,) * nd)


def _params(sem):
    return pltpu.CompilerParams(dimension_semantics=sem, vmem_limit_bytes=VMEM_LIMIT)


def _spread_groups(z, fill):
    lane = lax.broadcasted_iota(jnp.int32, (z.shape[0], LANES), 1)
    parts = []
    for g in range(N_KV):
        pair = z[:, (g // 2) * LANES:(g // 2 + 1) * LANES]
        if g % 2:
            pair = pltpu.roll(pair, HEAD_DIM, 1)
        parts.append(jnp.where(lane < HEAD_DIM, pair, fill))
    return jnp.concatenate(parts, axis=1)


def _value_rows(z):
    tm = z.shape[1]
    ones_row = jnp.where(lax.broadcasted_iota(jnp.int32, (LANES - HEAD_DIM, tm), 0) == 0, 1.0, 0.0)
    parts = []
    for g in range(N_KV):
        parts += [z[g * HEAD_DIM:(g + 1) * HEAD_DIM, :], ones_row]
    return jnp.concatenate(parts, axis=0)


TOK_U, TOK_GL, TOK_KC, TOK_VC, TOK_KS, TOK_KW, TOK_END = 0, 1024, 2048, 2304, 2560, 2816, 3072
FEAT_Q, FEAT_VS, FEAT_VW, FEAT_GT, FEAT_END = 0, 1024, 1280, 1536, 1600
GATE_ROWS = 16


def _in_proj_body(seq, x_ref, g_ref, w_ref, wt_ref, u_ref, gl_ref, kc_ref, vc_ref, ks_ref, kw_ref,
                  qt_ref, vst_ref, vwt_ref, gtt_ref):
    tm = x_ref.shape[0]
    a = _rms(x_ref[...], g_ref[...]).astype(bf16)

    def tok(lo, hi):
        return _dot(a, w_ref[:, lo:hi])

    def feat(lo, hi):
        return _dot_nt(wt_ref[lo:hi, :], a)

    u_ref[...] = tok(TOK_U, TOK_GL)
    gl_ref[...] = tok(TOK_GL, TOK_KC)
    kc_ref[...] = tok(TOK_KC, TOK_VC)
    vc_ref[...] = tok(TOK_VC, TOK_KS)
    lane = lax.broadcasted_iota(jnp.int32, (tm, LANES), 1)
    pos = (pl.program_id(0) * tm) % seq + lax.broadcasted_iota(jnp.int32, (tm, LANES), 0)
    block_onehot = jnp.where(lane - MASK_LO == pos // SEL_BLOCK, 1.0, 0.0)
    ks_ref[...] = _spread_groups(tok(TOK_KS, TOK_KW), block_onehot).astype(bf16)
    kw_ref[...] = _spread_groups(tok(TOK_KW, TOK_END), 0.0).astype(bf16)

    qt_ref[...] = (feat(FEAT_Q, FEAT_VS) * (LOG2E * HEAD_DIM ** -0.5)).astype(bf16)
    vst_ref[...] = _value_rows(feat(FEAT_VS, FEAT_VW)).astype(bf16)
    vwt_ref[...] = _value_rows(feat(FEAT_VW, FEAT_GT)).astype(bf16)
    gtt_ref[...] = feat(FEAT_GT, FEAT_END)


def _in_proj(x2, g, w, wt, seq, tm=512):
    m = x2.shape[0]
    wide = N_KV * LANES
    tok_out = [(1024, f32), (1024, f32), (256, f32), (256, f32), (wide, bf16), (wide, bf16)]
    feat_out = [(N_HEADS * HEAD_DIM, bf16), (wide, bf16), (wide, bf16), (N_KV * GATE_ROWS, f32)]
    body = lambda *refs: _in_proj_body(seq, *refs)
    return pl.pallas_call(
        body,
        grid=(m // tm,),
        in_specs=[pl.BlockSpec((tm, D_MODEL), lambda i: (i, 0)),
                  _const_spec((1, D_MODEL)),
                  pl.BlockSpec((D_MODEL, TOK_END), lambda i: (0, 0), pipeline_mode=pl.Buffered(1)),
                  pl.BlockSpec((FEAT_END, D_MODEL), lambda i: (0, 0), pipeline_mode=pl.Buffered(1))],
        out_specs=([pl.BlockSpec((tm, n), lambda i: (i, 0)) for n, _ in tok_out]
                   + [pl.BlockSpec((n, tm), lambda i: (0, i)) for n, _ in feat_out]),
        out_shape=([jax.ShapeDtypeStruct((m, n), dt) for n, dt in tok_out]
                   + [jax.ShapeDtypeStruct((n, m), dt) for n, dt in feat_out]),
        compiler_params=_params(("parallel",)),
        name="in_proj",
    )(x2, g, w, wt)


def _rglru_body(u_ref, gl_ref, cw_ref, cb_ref, wa_ref, ba_ref, wx_ref, bx_ref, lam_ref, gn_ref,
                o_ref, ubuf, a_sc, b_sc, hc_sc):
    t = u_ref.shape[0]

    @pl.when(pl.program_id(1) == 0)
    def _():
        ubuf[...] = jnp.zeros_like(ubuf)
        hc_sc[...] = jnp.zeros_like(hc_sc)

    u = u_ref[...]
    u3 = u.reshape(t // 8, 8, LRU_WIDTH)
    tail = ubuf[...]
    cw = cw_ref[...]
    row8w = lax.broadcasted_iota(jnp.int32, (t // 8, 8, LRU_WIDTH), 1)
    xc3 = cb_ref[...] + u3 * cw[CONV_WIDTH - 1]
    for d in range(1, CONV_WIDTH):
        cur = pltpu.roll(u3, d, 1)
        prev = jnp.concatenate([pltpu.roll(tail, d, 0)[None], cur[:-1]], axis=0)
        xc3 = xc3 + jnp.where(row8w >= d, cur, prev) * cw[CONV_WIDTH - 1 - d]
    ubuf[...] = u[t - 8:t, :]
    xc = xc3.reshape(t, LRU_WIDTH)

    xb = xc.astype(bf16)
    sp = jax.nn.softplus(-lam_ref[...])
    row8 = lax.broadcasted_iota(jnp.int32, (t // 8, 8, 256), 1)
    for c in range(LRU_WIDTH // 256):
        sl = slice(c * 256, (c + 1) * 256)
        xcb = xb[:, sl]
        r = jax.nn.sigmoid(_dot(xcb, wa_ref[c]) + ba_ref[:, sl])
        ig = jax.nn.sigmoid(_dot(xcb, wx_ref[c]) + bx_ref[:, sl])
        log_a = (-LRU_C) * r * sp[:, sl]
        a = jnp.exp(log_a)
        b = jnp.sqrt(1.0 - a * a) * (ig * xc[:, sl])
        a = a.reshape(t // 8, 8, 256)
        b = b.reshape(t // 8, 8, 256)
        for d in (1, 2, 4):
            keep = row8 >= d
            a_prev = pltpu.roll(a, d, 1)
            b_prev = pltpu.roll(b, d, 1)
            b = jnp.where(keep, a * b_prev + b, b)
            a = jnp.where(keep, a * a_prev, a)
        a_sc[:, sl] = a.reshape(t, 256)
        b_sc[:, sl] = b.reshape(t, 256)

    def group(gi, h):
        off = pl.multiple_of(gi * 8, 8)
        hg = b_sc[pl.ds(off, 8), :] + a_sc[pl.ds(off, 8), :] * h
        b_sc[pl.ds(off, 8), :] = hg
        return hg[7:8, :]

    h_last = lax.fori_loop(0, t // 8, group, hc_sc[0:1, :])
    hc_sc[0:1, :] = h_last
    y = b_sc[...] * jax.nn.gelu(gl_ref[...])
    o_ref[...] = _rms(y, gn_ref[...]).astype(o_ref.dtype)


def _rglru(u, gl, cw, cb, wa, ba, wx, bx, lam, gn, batch, seq, t=256):
    ns = seq // t
    row = pl.BlockSpec((t, LRU_WIDTH), lambda b, s: (b * ns + s, 0))
    vec = _const_spec((1, LRU_WIDTH))
    wspec = _const_spec((LRU_WIDTH // 256, 256, 256))
    return pl.pallas_call(
        _rglru_body,
        grid=(batch, ns),
        in_specs=[row, row, _const_spec((CONV_WIDTH, LRU_WIDTH)), vec, wspec, vec, wspec, vec, vec, vec],
        out_specs=row,
        out_shape=jax.ShapeDtypeStruct((batch * seq, LRU_WIDTH), bf16),
        scratch_shapes=[pltpu.VMEM((8, LRU_WIDTH), f32), pltpu.VMEM((t, LRU_WIDTH), f32),
                        pltpu.VMEM((t, LRU_WIDTH), f32), pltpu.VMEM((8, LRU_WIDTH), f32)],
        compiler_params=_params(("parallel", "arbitrary")),
        name="rglru",
    )(u, gl, cw, cb, wa, ba, wx, bx, lam, gn)


def _compress_body(kc_ref, vc_ref, pek_ref, w1k_ref, w2k_ref, pev_ref, w1v_ref, w2v_ref, ko_ref, vo_ref):
    nrow = ko_ref.shape[2]
    rows = lax.broadcasted_iota(jnp.int32, (nrow, LANES), 0)
    cols = lax.broadcasted_iota(jnp.int32, (LANES, nrow), 1)
    lane64 = lax.broadcasted_iota(jnp.int32, (nrow, LANES), 1) < HEAD_DIM
    half = CMP_STRIDE * HEAD_DIM

    def one(t_ref, pe_ref, w1_ref, w2_ref, o_ref, transposed):
        pe = jnp.broadcast_to(pe_ref[...], (8, CMP_LEN * HEAD_DIM)).astype(bf16)
        c0 = _dot(pe, w1_ref[...])[0:1, :]
        toks = [t_ref[0, pl.ds(j, nrow, stride=CMP_STRIDE), :] for j in range(CMP_STRIDE)]
        for g in range(LANES // HEAD_DIM):
            blocks = []
            for i in range(CMP_STRIDE // 2):
                even, odd = toks[2 * i], toks[2 * i + 1]
                if g == 0:
                    blocks.append(jnp.where(lane64, even, pltpu.roll(odd, HEAD_DIM, 1)))
                else:
                    blocks.append(jnp.where(lane64, pltpu.roll(even, HEAD_DIM, 1), odd))
            chunk = jnp.concatenate(blocks, axis=1).astype(bf16)
            lo = _dot(chunk, w1_ref[0:half, :])
            hi = _dot(chunk, w1_ref[half:2 * half, :])
            hid = jax.nn.gelu(lo + pltpu.roll(hi, nrow - 1, 0) + c0)
            if transposed:
                out = _dot_nt(w2_ref[...], hid.astype(bf16))
                o_ref[0, g] = jnp.where(cols < nrow - 1, out, 0.0).astype(o_ref.dtype)
            else:
                out = _dot(hid.astype(bf16), w2_ref[...])
                o_ref[0, g] = jnp.where(rows < nrow - 1, out, 0.0).astype(o_ref.dtype)

    one(kc_ref, pek_ref, w1k_ref, w2k_ref, ko_ref, False)
    one(vc_ref, pev_ref, w1v_ref, w2v_ref, vo_ref, True)


def _compress(kc, vc, pek, w1k, w2k, pev, w1v, w2v):
    batch, seq, width = kc.shape
    nrow = seq // CMP_STRIDE
    pair = LANES // HEAD_DIM
    tspec = pl.BlockSpec((1, seq, LANES), lambda b, h: (b, 0, h))
    ospec = pl.BlockSpec((1, pair, nrow, LANES), lambda b, h: (b, h, 0, 0))
    wts = [_const_spec((1, CMP_LEN * HEAD_DIM)), _const_spec((CMP_LEN * HEAD_DIM, CMP_HIDDEN))]
    tspec_v = pl.BlockSpec((1, pair, LANES, nrow), lambda b, h: (b, h, 0, 0))
    return pl.pallas_call(
        _compress_body,
        grid=(batch, width // LANES),
        in_specs=([tspec, tspec] + wts + [_const_spec((CMP_HIDDEN, LANES))]
                  + wts + [_const_spec((LANES, CMP_HIDDEN))]),
        out_specs=[ospec, tspec_v],
        out_shape=[jax.ShapeDtypeStruct((batch, N_KV, nrow, LANES), bf16),
                   jax.ShapeDtypeStruct((batch, N_KV, LANES, nrow), bf16)],
        compiler_params=_params(("parallel", "parallel")),
        name="compress",
    )(kc, vc, pek, w1k, w2k, pev, w1v, w2v)


def _bucket_thresholds():
    n = np.arange(0, 4096)
    max_exact = N_BUCKETS // 2
    nf = np.maximum(n, 1).astype(np.float32)
    large = max_exact + (np.log(nf / np.float32(max_exact)) / np.float32(math.log(MAX_DISTANCE / max_exact))
                         * np.float32(N_BUCKETS - max_exact)).astype(np.int32)
    large = np.minimum(large, N_BUCKETS - 1)
    bucket = np.where(n < max_exact, n, large)
    assert np.all(np.diff(bucket) >= 0) and bucket[0] == 0 and bucket[-1] == N_BUCKETS - 1
    return [int(np.argmax(bucket >= k)) for k in range(N_BUCKETS)]


_BUCKET_THR = _bucket_thresholds()
assert _BUCKET_THR[-1] <= (N_SEL_BIAS - 1) * Q_TILE + 1


def _bias_of_dist(dist, ok, tab_ref, head):
    last = tab_ref[N_BUCKETS - 1, head]
    val = jnp.full(dist.shape, (tab_ref[0, head] - last) * LOG2E, f32)
    for k in range(1, N_BUCKETS - 1):
        val = jnp.where(dist >= _BUCKET_THR[k], (tab_ref[k, head] - last) * LOG2E, val)
    val = jnp.where(dist >= _BUCKET_THR[N_BUCKETS - 1], 0.0, val)
    return jnp.where(ok, val, NEG_INF)


def _bias_body(tab_ref, bc_ref, bs_ref, bw_ref):
    g = pl.program_id(0)
    qt = pl.program_id(1)
    ncmp = bc_ref.shape[2]
    for r in range(HEADS_PER_KV):
        head = g * HEADS_PER_KV + r
        cols = slice(r * Q_TILE, (r + 1) * Q_TILE)
        n = lax.broadcasted_iota(jnp.int32, (ncmp, Q_TILE), 0)
        i = lax.broadcasted_iota(jnp.int32, (ncmp, Q_TILE), 1)
        dist = qt * Q_TILE + i - (n * CMP_STRIDE + CMP_LEN - 1)
        bc_ref[0, 0, :, cols] = _bias_of_dist(dist, (dist >= 0) & (n < ncmp - 1), tab_ref, head)

    @pl.when(qt == 0)
    def _():
        j = lax.broadcasted_iota(jnp.int32, (Q_TILE, Q_TILE), 0)
        i = lax.broadcasted_iota(jnp.int32, (Q_TILE, Q_TILE), 1)
        for r in range(HEADS_PER_KV):
            head = g * HEADS_PER_KV + r
            cols = slice(r * Q_TILE, (r + 1) * Q_TILE)
            for d in range(N_SEL_BIAS):
                dist = d * Q_TILE + i - j
                bs_ref[0, d, :, cols] = _bias_of_dist(dist, dist >= 0, tab_ref, head)
            dist = WINDOW + i - j
            bw_ref[0, 0, :, cols] = _bias_of_dist(dist, dist < WINDOW, tab_ref, head)


def _bias_tiles(rel_bias, seq):
    nq = seq // Q_TILE
    ncmp = seq // CMP_STRIDE
    rows = HEADS_PER_KV * Q_TILE
    return pl.pallas_call(
        _bias_body,
        grid=(N_KV, nq),
        in_specs=[pl.BlockSpec(memory_space=pltpu.SMEM)],
        out_specs=[pl.BlockSpec((1, 1, ncmp, rows), lambda g, q: (g, q, 0, 0)),
                   pl.BlockSpec((1, N_SEL_BIAS, Q_TILE, rows), lambda g, q: (g, 0, 0, 0)),
                   pl.BlockSpec((1, 1, Q_TILE, rows), lambda g, q: (g, 0, 0, 0))],
        out_shape=[jax.ShapeDtypeStruct((N_KV, nq, ncmp, rows), f32),
                   jax.ShapeDtypeStruct((N_KV, N_SEL_BIAS, Q_TILE, rows), f32),
                   jax.ShapeDtypeStruct((N_KV, 1, Q_TILE, rows), f32)],
        compiler_params=_params(("parallel", "arbitrary")),
        name="bias_tiles",
    )(rel_bias)


def _importance_matrix_t(seq):
    n_cmp = (seq - CMP_LEN) // CMP_STRIDE + 1
    n_sel = seq // SEL_BLOCK
    ratio_sel = SEL_BLOCK // CMP_STRIDE
    ratio_cmp = CMP_LEN // CMP_STRIDE
    jj = np.arange(n_sel)[:, None, None]
    ci = ratio_sel * jj + np.arange(ratio_sel)[None, :, None] - np.arange(ratio_cmp)[None, None, :]
    jb = np.broadcast_to(jj, ci.shape)
    ok = (ci >= 0) & (ci < n_cmp)
    m = np.zeros((n_sel, seq // CMP_STRIDE), np.float32)
    np.add.at(m, (jb[ok], ci[ok]), 1.0)
    return m


def _scores(qt, k_ref, t, first_tile, biases):
    s = _dot(k_ref[0, first_tile * Q_TILE:(t + 1) * Q_TILE, :], qt)
    pieces = []
    for kt in range(first_tile, t + 1):
        piece = s[(kt - first_tile) * Q_TILE:(kt - first_tile + 1) * Q_TILE, :]
        if t - kt in biases:
            piece = piece + biases[t - kt]
        pieces.append(piece)
    return pieces, jnp.max(functools.reduce(jnp.maximum, pieces), axis=0, keepdims=True)


def _weighted_values(pieces, m, vt_ref, t, first_tile):
    p = jnp.concatenate([jnp.exp2((pc - m).astype(bf16)) for pc in pieces], axis=0)
    acc = _dot(vt_ref[0:V_ROWS, first_tile * Q_TILE:(t + 1) * Q_TILE], p)
    return acc[0:HEAD_DIM, :], 1.0 / acc[MASK_LO:MASK_LO + 1, :]


def _nsa_body(qt_ref, kc_ref, vct_ref, ks_ref, vst_ref, kw_ref, vwt_ref, gtt_ref, bc_ref, bs_ref, bw_ref,
              mt_ref, o_ref):
    n_sel = mt_ref.shape[0]
    n_tiles = qt_ref.shape[1] // Q_TILE
    cols = HEADS_PER_KV * Q_TILE
    head_cols = [slice(r * Q_TILE, (r + 1) * Q_TILE) for r in range(HEADS_PER_KV)]
    kc = kc_ref[0, 0]
    vct = vct_ref[0, 0]
    mt = mt_ref[...]
    jj = lax.broadcasted_iota(jnp.int32, (n_sel, Q_TILE), 0)
    qcol = lax.broadcasted_iota(jnp.int32, (n_sel, Q_TILE), 1)
    zeros_tail = jnp.zeros((LANES - HEAD_DIM, Q_TILE), bf16)
    zeros_rest = jnp.zeros((LANES - HEAD_DIM - n_sel, Q_TILE), bf16)

    def compressed_scores(t):
        tok = slice(t * Q_TILE, (t + 1) * Q_TILE)
        heads = [qt_ref[r * HEAD_DIM:(r + 1) * HEAD_DIM, tok] for r in range(HEADS_PER_KV)]
        q_plain = jnp.concatenate([jnp.concatenate([hd, zeros_tail], axis=0) for hd in heads], axis=1)
        return heads, q_plain, _dot(kc, q_plain) + bc_ref[0, t]

    def select_stage(t, heads, q_plain, s):
        p = jnp.exp2(s - jnp.max(s, axis=0, keepdims=True))
        norm = 1.0 / jnp.sum(p, axis=0, keepdims=True)
        if (t + 1) * Q_TILE > CMP_LEN - 1 >= t * Q_TILE:
            pos = t * Q_TILE + (lax.broadcasted_iota(jnp.int32, (1, cols), 1) & (Q_TILE - 1))
            norm = jnp.where(pos >= CMP_LEN - 1, norm, 0.0)
        p = p * norm
        o_c = _dot(vct[0:HEAD_DIM, :], p.astype(bf16))

        n_causal = ((t + 1) * Q_TILE - 1) // SEL_BLOCK + 1
        if n_causal <= N_SELECT:
            return q_plain, q_plain, o_c

        psum = p[:, head_cols[0]] + p[:, head_cols[1]] + p[:, head_cols[2]] + p[:, head_cols[3]]
        p_hi = psum.astype(bf16)
        p_lo = (psum - p_hi.astype(f32)).astype(bf16)
        imp = _dot(mt, p_hi) + _dot(mt, p_lo)
        dblk = (t * Q_TILE + qcol) // SEL_BLOCK - jj
        forced = (jj == 0) | ((dblk >= 0) & (dblk < N_LOCAL_FORCED))
        imp = jnp.where(forced, FORCE_SCORE, jnp.where(dblk >= 0, imp, -FORCE_SCORE))
        rank = jnp.zeros((n_sel, Q_TILE), f32)
        for i in range(n_causal):
            row = imp[i:i + 1, :]
            ahead = (row > imp) | ((row == imp) & (jj > i))
            rank = rank + jnp.where(ahead, 1.0, 0.0)
        mask_rows = jnp.where(rank < N_SELECT, 0.0, NEG_INF).astype(bf16)
        q_masked = jnp.concatenate(
            [jnp.concatenate([hd, mask_rows, zeros_rest], axis=0) for hd in heads], axis=1)
        return q_plain, q_masked, o_c

    first_win = lambda t: max(t - WINDOW // Q_TILE, 0)

    def sel_scores(t, q_masked):
        return _scores(q_masked, ks_ref, t, 0, {d: bs_ref[0, d] for d in range(N_SEL_BIAS)})

    def win_scores(t, q_plain):
        band = {d: bs_ref[0, d] for d in range(N_SEL_BIAS)}
        band[WINDOW // Q_TILE] = bw_ref[0, 0]
        return _scores(q_plain, kw_ref, t, first_win(t), band)

    def finish(t, o_c, sel_out, win_out):
        tok = slice(t * Q_TILE, (t + 1) * Q_TILE)
        acc_s, inv_s = sel_out
        acc_w, inv_w = win_out
        gt = jax.nn.sigmoid(gtt_ref[:, tok])
        gate = lambda br: jnp.concatenate([gt[3 * r + br:3 * r + br + 1, :] for r in range(HEADS_PER_KV)], axis=1)
        out_t = gate(0) * o_c + (gate(1) * inv_s) * acc_s + (gate(2) * inv_w) * acc_w
        pairs = [jnp.concatenate([out_t[:, head_cols[2 * h]], out_t[:, head_cols[2 * h + 1]]], axis=0).T
                 for h in range(HEADS_PER_KV // 2)]
        o_ref[0, tok, :] = jnp.concatenate(pairs, axis=-1)

    compressed = {t: compressed_scores(t) for t in range(min(3, n_tiles))}
    selected = {t: select_stage(t, *compressed.pop(t)) for t in range(min(2, n_tiles))}
    s_sel = {0: sel_scores(0, selected[0][1])}
    s_win = {0: win_scores(0, selected[0][0])}
    for t in range(n_tiles):
        o_c = selected.pop(t)[2]
        if t + 3 < n_tiles:
            compressed[t + 3] = compressed_scores(t + 3)
        if t + 1 < n_tiles:
            s_sel[t + 1] = sel_scores(t + 1, selected[t + 1][1])
            s_win[t + 1] = win_scores(t + 1, selected[t + 1][0])
        if t + 2 < n_tiles:
            selected[t + 2] = select_stage(t + 2, *compressed.pop(t + 2))
        sel_out = _weighted_values(*s_sel.pop(t), vst_ref, t, 0)
        win_out = _weighted_values(*s_win.pop(t), vwt_ref, t, first_win(t))
        finish(t, o_c, sel_out, win_out)


def _nsa(qt, kc, vct, ks, vst, kw, vwt, gtt, bc, bs, bw, mt):
    batch, seq, _ = ks.shape
    nq = seq // Q_TILE
    cols = HEADS_PER_KV * Q_TILE
    ncmp = kc.shape[2]
    width = HEADS_PER_KV * HEAD_DIM
    k_spec = pl.BlockSpec((1, seq, LANES), lambda b, gi: (b, 0, gi))
    vt_spec = pl.BlockSpec((LANES, seq), lambda b, gi: (gi, b))
    return pl.pallas_call(
        _nsa_body,
        grid=(batch, N_KV),
        in_specs=[pl.BlockSpec((width, seq), lambda b, gi: (gi, b)),
                  pl.BlockSpec((1, 1, ncmp, LANES), lambda b, gi: (b, gi, 0, 0)),
                  pl.BlockSpec((1, 1, LANES, ncmp), lambda b, gi: (b, gi, 0, 0)),
                  k_spec, vt_spec, k_spec, vt_spec,
                  pl.BlockSpec((GATE_ROWS, seq), lambda b, gi: (gi, b)),
                  pl.BlockSpec((1, nq, ncmp, cols), lambda b, gi: (gi, 0, 0, 0)),
                  pl.BlockSpec((1, N_SEL_BIAS, Q_TILE, cols), lambda b, gi: (gi, 0, 0, 0)),
                  pl.BlockSpec((1, 1, Q_TILE, cols), lambda b, gi: (gi, 0, 0, 0)),
                  _const_spec(mt.shape)],
        out_specs=pl.BlockSpec((1, seq, width), lambda b, gi: (b, 0, gi)),
        out_shape=jax.ShapeDtypeStruct((batch, seq, N_KV * width), f32),
        compiler_params=_params(("parallel", "arbitrary")),
        name="nsa",
    )(qt, kc, vct, ks, vst, kw, vwt, gtt, bc, bs, bw, mt)


def _out_proj_body(x_ref, yl_ref, yn_ref, gn_ref, w_ref, gp_ref, gm_ref, o_ref, a_ref):
    half = x_ref.shape[0] // 2
    rows = [slice(0, half), slice(half, 2 * half)]
    ys = []
    for r in rows:
        yn = _rms(yn_ref[r, :], gn_ref[...]).astype(bf16)
        ys.append(_dot(yl_ref[r, :], w_ref[0:LRU_WIDTH, :]) + _dot(yn, w_ref[LRU_WIDTH:, :]))
    for r, y in zip(rows, ys):
        h = x_ref[r, :] + _rms(y, gp_ref[...])
        o_ref[r, :] = h
        a_ref[r, :] = _rms(h, gm_ref[...]).astype(bf16)


def _out_proj(x2, yl, yn, gn, w, gp, gm, tm=512):
    m = x2.shape[0]
    row = pl.BlockSpec((tm, D_MODEL), lambda i: (i, 0))
    return pl.pallas_call(
        _out_proj_body,
        grid=(m // tm,),
        in_specs=[row,
                  pl.BlockSpec((tm, LRU_WIDTH), lambda i: (i, 0)),
                  pl.BlockSpec((tm, D_MODEL - LRU_WIDTH), lambda i: (i, 0)),
                  _const_spec((1, D_MODEL - LRU_WIDTH)),
                  _const_spec((D_MODEL, D_MODEL)),
                  _const_spec((1, D_MODEL)), _const_spec((1, D_MODEL))],
        out_specs=[row, row],
        out_shape=[jax.ShapeDtypeStruct((m, D_MODEL), f32), jax.ShapeDtypeStruct((m, D_MODEL), bf16)],
        compiler_params=_params(("parallel",)),
        name="out_proj",
    )(x2, yl, yn, gn, w, gp, gm)


def _mlp_body(a_ref, w1_ref, w2_ref, g2_ref, o_ref, acc_sc):
    j = pl.program_id(1)

    @pl.when(j == 0)
    def _():
        acc_sc[...] = jnp.zeros_like(acc_sc)

    hid = jnp.maximum(_dot(a_ref[...], w1_ref[...]), 0.0)
    acc_sc[...] += _dot((hid * hid).astype(bf16), w2_ref[...])

    @pl.when(j == pl.num_programs(1) - 1)
    def _():
        o_ref[...] = _rms(acc_sc[...], g2_ref[...])


def _mlp(a, w1, w2, g2, tm=1024, tf=1024):
    m = a.shape[0]
    return pl.pallas_call(
        _mlp_body,
        grid=(m // tm, D_FF // tf),
        in_specs=[pl.BlockSpec((tm, D_MODEL), lambda i, j: (i, 0)),
                  pl.BlockSpec((D_MODEL, tf), lambda i, j: (0, j)),
                  pl.BlockSpec((tf, D_MODEL), lambda i, j: (j, 0)),
                  _const_spec((1, D_MODEL))],
        out_specs=pl.BlockSpec((tm, D_MODEL), lambda i, j: (i, 0)),
        out_shape=jax.ShapeDtypeStruct((m, D_MODEL), f32),
        scratch_shapes=[pltpu.VMEM((tm, D_MODEL), f32)],
        compiler_params=_params(("parallel", "arbitrary")),
        name="mlp",
    )(a, w1, w2, g2)


def _ple_body(h_ref, f_ref, p_ref, wg_ref, wp_ref, o_ref):
    h = h_ref[...] + f_ref[...]
    gate = jax.nn.sigmoid(_dot(h.astype(bf16), wg_ref[...]))
    o_ref[...] = h + gate * _dot(p_ref[...].astype(bf16), wp_ref[...])


def _ple(h, f, p2, wg, wp, tm=512):
    m = h.shape[0]
    row = pl.BlockSpec((tm, D_MODEL), lambda i: (i, 0))
    return pl.pallas_call(
        _ple_body,
        grid=(m // tm,),
        in_specs=[row, row,
                  pl.BlockSpec((tm, PLE_DIM), lambda i: (i, 0)),
                  _const_spec((D_MODEL, D_MODEL)),
                  _const_spec((PLE_DIM, D_MODEL))],
        out_specs=row,
        out_shape=jax.ShapeDtypeStruct((m, D_MODEL), f32),
        compiler_params=_params(("parallel",)),
        name="ple",
    )(h, f, p2, wg, wp)


def _block_diag_chunks(w):
    per = 256 // LRU_BLOCK_DIM
    w = w.reshape(LRU_BLOCKS // per, per, LRU_BLOCK_DIM, LRU_BLOCK_DIM)
    eye = jnp.eye(per, dtype=w.dtype)
    return jnp.einsum('cpij,pq->cpiqj', w, eye).reshape(LRU_BLOCKS // per, 256, 256)


def _layer(h, p_i, i, prm, bias):
    batch, seq, _ = h.shape
    m = batch * seq
    x2 = h.reshape(m, D_MODEL)
    row = lambda v: v.reshape(1, -1)
    per_batch = lambda t: t.reshape(batch, seq, t.shape[-1])

    w_in = prm["w_in"][i].astype(bf16)
    w_tok = jnp.concatenate([w_in[:, 0:2048], w_in[:, 3072:3840], w_in[:, 4096:4352]], axis=1)
    per = 3 * HEADS_PER_KV
    w_gate = jnp.pad(w_in[:, GATE_LO:IN_DIM].reshape(D_MODEL, N_KV, per), ((0, 0), (0, 0), (0, GATE_ROWS - per)))
    w_feat = jnp.concatenate([w_in[:, 2048:3072], w_in[:, 3840:4096], w_in[:, 4352:4608],
                              w_gate.reshape(D_MODEL, N_KV * GATE_ROWS)], axis=1).T
    u, gl, kc, vc, ks, kw, qt, vst, vwt, gtt = _in_proj(x2, row(prm["norm_mix_pre"][i]), w_tok, w_feat, seq)

    y_lru = _rglru(u, gl, prm["conv_w"][i], row(prm["conv_b"][i]),
                   _block_diag_chunks(prm["lru_wa"][i]).astype(bf16), row(prm["lru_ba"][i]),
                   _block_diag_chunks(prm["lru_wx"][i]).astype(bf16), row(prm["lru_bx"][i]),
                   row(prm["lru_lambda"][i]), row(prm["gnorm_lru"][i]), batch, seq)

    lane_pad = lambda w: jnp.pad(w, ((0, 0), (0, LANES - HEAD_DIM))).astype(bf16)
    kcc, vcct = _compress(per_batch(kc), per_batch(vc),
                          prm["cmp_pe_k"][i].reshape(1, -1), prm["cmp_w1_k"][i].astype(bf16),
                          lane_pad(prm["cmp_w2_k"][i]),
                          prm["cmp_pe_v"][i].reshape(1, -1), prm["cmp_w1_v"][i].astype(bf16),
                          lane_pad(prm["cmp_w2_v"][i]).T)

    bc, bs, bw = bias
    y_nsa = _nsa(qt, kcc, vcct, per_batch(ks), vst, per_batch(kw), vwt, gtt, bc, bs, bw,
                 jnp.asarray(_importance_matrix_t(seq)).astype(bf16))

    h1, a1 = _out_proj(x2, y_lru, y_nsa.reshape(m, -1), row(prm["gnorm_nsa"][i]),
                       prm["w_out"][i].astype(bf16), row(prm["norm_mix_post"][i]), row(prm["norm_mlp_pre"][i]))
    f = _mlp(a1, prm["mlp_w1"][i].astype(bf16), prm["mlp_w2"][i].astype(bf16), row(prm["norm_mlp_post"][i]))
    h3 = _ple(h1, f, p_i.reshape(m, PLE_DIM), prm["ple_gate"][i].astype(bf16), prm["ple_proj"][i].astype(bf16))
    return h3.reshape(batch, seq, D_MODEL)


def kernel(x, p, norm_mix_pre, norm_mix_post, norm_mlp_pre, norm_mlp_post, w_in, conv_w, conv_b, lru_wa, lru_ba, lru_wx, lru_bx, lru_lambda, cmp_pe_k, cmp_w1_k, cmp_w2_k, cmp_pe_v, cmp_w1_v, cmp_w2_v, rel_bias, gnorm_lru, gnorm_nsa, w_out, mlp_w1, mlp_w2, ple_gate, ple_proj):
    prm = dict(norm_mix_pre=norm_mix_pre, norm_mix_post=norm_mix_post, norm_mlp_pre=norm_mlp_pre,
               norm_mlp_post=norm_mlp_post, w_in=w_in, conv_w=conv_w, conv_b=conv_b, lru_wa=lru_wa,
               lru_ba=lru_ba, lru_wx=lru_wx, lru_bx=lru_bx, lru_lambda=lru_lambda, cmp_pe_k=cmp_pe_k,
               cmp_w1_k=cmp_w1_k, cmp_w2_k=cmp_w2_k, cmp_pe_v=cmp_pe_v, cmp_w1_v=cmp_w1_v,
               cmp_w2_v=cmp_w2_v, gnorm_lru=gnorm_lru, gnorm_nsa=gnorm_nsa, w_out=w_out, mlp_w1=mlp_w1,
               mlp_w2=mlp_w2, ple_gate=ple_gate, ple_proj=ple_proj)
    bias = _bias_tiles(rel_bias, x.shape[1])
    h = x
    for i in range(w_in.shape[0]):
        h = _layer(h, p[i], i, prm, bias)
    return h
```

```python
import functools
import math

import numpy as np
import jax
import jax.numpy as jnp
from jax import lax
from jax.experimental import pallas as pl
from jax.experimental.pallas import tpu as pltpu

D_MODEL = 2048
PLE_DIM = 256
LRU_WIDTH = 1024
LRU_BLOCKS = 16
LRU_BLOCK_DIM = 64
CONV_WIDTH = 4
LRU_C = 8.0
HEAD_DIM = 64
N_HEADS = 16
N_KV = 4
HEADS_PER_KV = 4
CMP_LEN = 32
CMP_STRIDE = 16
CMP_HIDDEN = 256
SEL_BLOCK = 64
N_SELECT = 16
N_LOCAL_FORCED = 2
WINDOW = 512
N_BUCKETS = 32
MAX_DISTANCE = 128
D_FF = 4 * D_MODEL
NORM_EPS = 1e-6
NEG_INF = -1e30
FORCE_SCORE = 1e4
IN_DIM = 4656
IN_PAD = 4736
GATE_LO = 4608

LANES = 128
MASK_LO = HEAD_DIM
Q_TILE = 128
N_SEL_BIAS = 2
V_ROWS = 80
LOG2E = math.log2(math.e)

VMEM_LIMIT = 56 * 1024 * 1024

f32 = jnp.float32
bf16 = jnp.bfloat16


def _rms(x, g):
    return x * lax.rsqrt(jnp.mean(x * x, axis=-1, keepdims=True) + NORM_EPS) * g


def _dot(a, b):
    return jnp.dot(a, b, preferred_element_type=f32)


def _dot_nt(a, b):
    return lax.dot_general(a, b, (((1,), (1,)), ((), ())), preferred_element_type=f32)


def _dot_tn(a, b):
    return lax.dot_general(a, b, (((0,), (0,)), ((), ())), preferred_element_type=f32)


def _const_spec(shape):
    nd = len(shape)
    return pl.BlockSpec(shape, lambda *_: (0,) * nd)


def _params(sem):
    return pltpu.CompilerParams(dimension_semantics=sem, vmem_limit_bytes=VMEM_LIMIT)


def _spread_groups(z, fill):
    lane = lax.broadcasted_iota(jnp.int32, (z.shape[0], LANES), 1)
    parts = []
    for g in range(N_KV):
        pair = z[:, (g // 2) * LANES:(g // 2 + 1) * LANES]
        if g % 2:
            pair = pltpu.roll(pair, HEAD_DIM, 1)
        parts.append(jnp.where(lane < HEAD_DIM, pair, fill))
    return jnp.concatenate(parts, axis=1)


def _value_rows(z):
    tm = z.shape[1]
    ones_row = jnp.where(lax.broadcasted_iota(jnp.int32, (LANES - HEAD_DIM, tm), 0) == 0, 1.0, 0.0)
    parts = []
    for g in range(N_KV):
        parts += [z[g * HEAD_DIM:(g + 1) * HEAD_DIM, :], ones_row]
    return jnp.concatenate(parts, axis=0)


TOK_U, TOK_GL, TOK_KC, TOK_VC, TOK_KS, TOK_KW, TOK_END = 0, 1024, 2048, 2304, 2560, 2816, 3072
FEAT_Q, FEAT_VS, FEAT_VW, FEAT_GT, FEAT_END = 0, 1024, 1280, 1536, 1600
GATE_ROWS = 16


def _in_proj_body(seq, x_ref, g_ref, w_ref, wt_ref, u_ref, gl_ref, kc_ref, vc_ref, ks_ref, kw_ref,
                  qt_ref, vst_ref, vwt_ref, gtt_ref):
    tm = x_ref.shape[0]
    a = _rms(x_ref[...], g_ref[...]).astype(bf16)

    def tok(lo, hi):
        return _dot(a, w_ref[:, lo:hi])

    def feat(lo, hi):
        return _dot_nt(wt_ref[lo:hi, :], a)

    u_ref[...] = tok(TOK_U, TOK_GL)
    gl_ref[...] = tok(TOK_GL, TOK_KC)
    kc_ref[...] = tok(TOK_KC, TOK_VC)
    vc_ref[...] = tok(TOK_VC, TOK_KS)
    lane = lax.broadcasted_iota(jnp.int32, (tm, LANES), 1)
    pos = (pl.program_id(0) * tm) % seq + lax.broadcasted_iota(jnp.int32, (tm, LANES), 0)
    block_onehot = jnp.where(lane - MASK_LO == pos // SEL_BLOCK, 1.0, 0.0)
    ks_ref[...] = _spread_groups(tok(TOK_KS, TOK_KW), block_onehot).astype(bf16)
    kw_ref[...] = _spread_groups(tok(TOK_KW, TOK_END), 0.0).astype(bf16)

    qt_ref[...] = (feat(FEAT_Q, FEAT_VS) * (LOG2E * HEAD_DIM ** -0.5)).astype(bf16)
    vst_ref[...] = _value_rows(feat(FEAT_VS, FEAT_VW)).astype(bf16)
    vwt_ref[...] = _value_rows(feat(FEAT_VW, FEAT_GT)).astype(bf16)
    gtt_ref[...] = feat(FEAT_GT, FEAT_END)


def _in_proj(x2, g, w, wt, seq, tm=512):
    m = x2.shape[0]
    wide = N_KV * LANES
    tok_out = [(1024, f32), (1024, f32), (256, f32), (256, f32), (wide, bf16), (wide, bf16)]
    feat_out = [(N_HEADS * HEAD_DIM, bf16), (wide, bf16), (wide, bf16), (N_KV * GATE_ROWS, f32)]
    body = lambda *refs: _in_proj_body(seq, *refs)
    return pl.pallas_call(
        body,
        grid=(m // tm,),
        in_specs=[pl.BlockSpec((tm, D_MODEL), lambda i: (i, 0)),
                  _const_spec((1, D_MODEL)),
                  pl.BlockSpec((D_MODEL, TOK_END), lambda i: (0, 0), pipeline_mode=pl.Buffered(1)),
                  pl.BlockSpec((FEAT_END, D_MODEL), lambda i: (0, 0), pipeline_mode=pl.Buffered(1))],
        out_specs=([pl.BlockSpec((tm, n), lambda i: (i, 0)) for n, _ in tok_out]
                   + [pl.BlockSpec((n, tm), lambda i: (0, i)) for n, _ in feat_out]),
        out_shape=([jax.ShapeDtypeStruct((m, n), dt) for n, dt in tok_out]
                   + [jax.ShapeDtypeStruct((n, m), dt) for n, dt in feat_out]),
        compiler_params=_params(("parallel",)),
        name="in_proj",
    )(x2, g, w, wt)


def _rglru_body(u_ref, gl_ref, cw_ref, cb_ref, wa_ref, ba_ref, wx_ref, bx_ref, lam_ref, gn_ref,
                o_ref, ubuf, a_sc, b_sc, hc_sc):
    t = u_ref.shape[0]

    @pl.when(pl.program_id(1) == 0)
    def _():
        ubuf[...] = jnp.zeros_like(ubuf)
        hc_sc[...] = jnp.zeros_like(hc_sc)

    u = u_ref[...]
    u3 = u.reshape(t // 8, 8, LRU_WIDTH)
    tail = ubuf[...]
    cw = cw_ref[...]
    row8w = lax.broadcasted_iota(jnp.int32, (t // 8, 8, LRU_WIDTH), 1)
    xc3 = cb_ref[...] + u3 * cw[CONV_WIDTH - 1]
    for d in range(1, CONV_WIDTH):
        cur = pltpu.roll(u3, d, 1)
        prev = jnp.concatenate([pltpu.roll(tail, d, 0)[None], cur[:-1]], axis=0)
        xc3 = xc3 + jnp.where(row8w >= d, cur, prev) * cw[CONV_WIDTH - 1 - d]
    ubuf[...] = u[t - 8:t, :]
    xc = xc3.reshape(t, LRU_WIDTH)

    xb = xc.astype(bf16)
    sp = jax.nn.softplus(-lam_ref[...])
    row8 = lax.broadcasted_iota(jnp.int32, (t // 8, 8, 256), 1)
    for c in range(LRU_WIDTH // 256):
        sl = slice(c * 256, (c + 1) * 256)
        xcb = xb[:, sl]
        r = jax.nn.sigmoid(_dot(xcb, wa_ref[c]) + ba_ref[:, sl])
        ig = jax.nn.sigmoid(_dot(xcb, wx_ref[c]) + bx_ref[:, sl])
        log_a = (-LRU_C) * r * sp[:, sl]
        a = jnp.exp(log_a)
        b = jnp.sqrt(1.0 - a * a) * (ig * xc[:, sl])
        a = a.reshape(t // 8, 8, 256)
        b = b.reshape(t // 8, 8, 256)
        for d in (1, 2, 4):
            keep = row8 >= d
            a_prev = pltpu.roll(a, d, 1)
            b_prev = pltpu.roll(b, d, 1)
            b = jnp.where(keep, a * b_prev + b, b)
            a = jnp.where(keep, a * a_prev, a)
        a_sc[:, sl] = a.reshape(t, 256)
        b_sc[:, sl] = b.reshape(t, 256)

    def group(gi, h):
        off = pl.multiple_of(gi * 8, 8)
        hg = b_sc[pl.ds(off, 8), :] + a_sc[pl.ds(off, 8), :] * h
        b_sc[pl.ds(off, 8), :] = hg
        return hg[7:8, :]

    h_last = lax.fori_loop(0, t // 8, group, hc_sc[0:1, :])
    hc_sc[0:1, :] = h_last
    y = b_sc[...] * jax.nn.gelu(gl_ref[...])
    o_ref[...] = _rms(y, gn_ref[...]).astype(o_ref.dtype)


def _rglru(u, gl, cw, cb, wa, ba, wx, bx, lam, gn, batch, seq, t=256):
    ns = seq // t
    row = pl.BlockSpec((t, LRU_WIDTH), lambda b, s: (b * ns + s, 0))
    vec = _const_spec((1, LRU_WIDTH))
    wspec = _const_spec((LRU_WIDTH // 256, 256, 256))
    return pl.pallas_call(
        _rglru_body,
        grid=(batch, ns),
        in_specs=[row, row, _const_spec((CONV_WIDTH, LRU_WIDTH)), vec, wspec, vec, wspec, vec, vec, vec],
        out_specs=row,
        out_shape=jax.ShapeDtypeStruct((batch * seq, LRU_WIDTH), bf16),
        scratch_shapes=[pltpu.VMEM((8, LRU_WIDTH), f32), pltpu.VMEM((t, LRU_WIDTH), f32),
                        pltpu.VMEM((t, LRU_WIDTH), f32), pltpu.VMEM((8, LRU_WIDTH), f32)],
        compiler_params=_params(("parallel", "arbitrary")),
        name="rglru",
    )(u, gl, cw, cb, wa, ba, wx, bx, lam, gn)


def _compress_body(kc_ref, vc_ref, pek_ref, w1k_ref, w2k_ref, pev_ref, w1v_ref, w2v_ref, ko_ref, vo_ref):
    nrow = ko_ref.shape[2]
    rows = lax.broadcasted_iota(jnp.int32, (nrow, LANES), 0)
    cols = lax.broadcasted_iota(jnp.int32, (LANES, nrow), 1)
    lane64 = lax.broadcasted_iota(jnp.int32, (nrow, LANES), 1) < HEAD_DIM
    half = CMP_STRIDE * HEAD_DIM

    def one(t_ref, pe_ref, w1_ref, w2_ref, o_ref, transposed):
        pe = jnp.broadcast_to(pe_ref[...], (8, CMP_LEN * HEAD_DIM)).astype(bf16)
        c0 = _dot(pe, w1_ref[...])[0:1, :]
        toks = [t_ref[0, pl.ds(j, nrow, stride=CMP_STRIDE), :] for j in range(CMP_STRIDE)]
        for g in range(LANES // HEAD_DIM):
            blocks = []
            for i in range(CMP_STRIDE // 2):
                even, odd = toks[2 * i], toks[2 * i + 1]
                if g == 0:
                    blocks.append(jnp.where(lane64, even, pltpu.roll(odd, HEAD_DIM, 1)))
                else:
                    blocks.append(jnp.where(lane64, pltpu.roll(even, HEAD_DIM, 1), odd))
            chunk = jnp.concatenate(blocks, axis=1).astype(bf16)
            lo = _dot(chunk, w1_ref[0:half, :])
            hi = _dot(chunk, w1_ref[half:2 * half, :])
            hid = jax.nn.gelu(lo + pltpu.roll(hi, nrow - 1, 0) + c0)
            if transposed:
                out = _dot_nt(w2_ref[...], hid.astype(bf16))
                o_ref[0, g] = jnp.where(cols < nrow - 1, out, 0.0).astype(o_ref.dtype)
            else:
                out = _dot(hid.astype(bf16), w2_ref[...])
                o_ref[0, g] = jnp.where(rows < nrow - 1, out, 0.0).astype(o_ref.dtype)

    one(kc_ref, pek_ref, w1k_ref, w2k_ref, ko_ref, False)
    one(vc_ref, pev_ref, w1v_ref, w2v_ref, vo_ref, True)


def _compress(kc, vc, pek, w1k, w2k, pev, w1v, w2v):
    batch, seq, width = kc.shape
    nrow = seq // CMP_STRIDE
    pair = LANES // HEAD_DIM
    tspec = pl.BlockSpec((1, seq, LANES), lambda b, h: (b, 0, h))
    ospec = pl.BlockSpec((1, pair, nrow, LANES), lambda b, h: (b, h, 0, 0))
    wts = [_const_spec((1, CMP_LEN * HEAD_DIM)), _const_spec((CMP_LEN * HEAD_DIM, CMP_HIDDEN))]
    tspec_v = pl.BlockSpec((1, pair, LANES, nrow), lambda b, h: (b, h, 0, 0))
    return pl.pallas_call(
        _compress_body,
        grid=(batch, width // LANES),
        in_specs=([tspec, tspec] + wts + [_const_spec((CMP_HIDDEN, LANES))]
                  + wts + [_const_spec((LANES, CMP_HIDDEN))]),
        out_specs=[ospec, tspec_v],
        out_shape=[jax.ShapeDtypeStruct((batch, N_KV, nrow, LANES), bf16),
                   jax.ShapeDtypeStruct((batch, N_KV, LANES, nrow), bf16)],
        compiler_params=_params(("parallel", "parallel")),
        name="compress",
    )(kc, vc, pek, w1k, w2k, pev, w1v, w2v)


def _bucket_thresholds():
    n = np.arange(0, 4096)
    max_exact = N_BUCKETS // 2
    nf = np.maximum(n, 1).astype(np.float32)
    large = max_exact + (np.log(nf / np.float32(max_exact)) / np.float32(math.log(MAX_DISTANCE / max_exact))
                         * np.float32(N_BUCKETS - max_exact)).astype(np.int32)
    large = np.minimum(large, N_BUCKETS - 1)
    bucket = np.where(n < max_exact, n, large)
    assert np.all(np.diff(bucket) >= 0) and bucket[0] == 0 and bucket[-1] == N_BUCKETS - 1
    return [int(np.argmax(bucket >= k)) for k in range(N_BUCKETS)]


_BUCKET_THR = _bucket_thresholds()
assert _BUCKET_THR[-1] <= (N_SEL_BIAS - 1) * Q_TILE + 1


def _bias_of_dist(dist, ok, tab_ref, head):
    last = tab_ref[N_BUCKETS - 1, head]
    val = jnp.full(dist.shape, (tab_ref[0, head] - last) * LOG2E, f32)
    for k in range(1, N_BUCKETS - 1):
        val = jnp.where(dist >= _BUCKET_THR[k], (tab_ref[k, head] - last) * LOG2E, val)
    val = jnp.where(dist >= _BUCKET_THR[N_BUCKETS - 1], 0.0, val)
    return jnp.where(ok, val, NEG_INF)


def _bias_body(tab_ref, bc_ref, bs_ref, bw_ref):
    g = pl.program_id(0)
    qt = pl.program_id(1)
    ncmp = bc_ref.shape[2]
    for r in range(HEADS_PER_KV):
        head = g * HEADS_PER_KV + r
        cols = slice(r * Q_TILE, (r + 1) * Q_TILE)
        n = lax.broadcasted_iota(jnp.int32, (ncmp, Q_TILE), 0)
        i = lax.broadcasted_iota(jnp.int32, (ncmp, Q_TILE), 1)
        dist = qt * Q_TILE + i - (n * CMP_STRIDE + CMP_LEN - 1)
        bc_ref[0, 0, :, cols] = _bias_of_dist(dist, (dist >= 0) & (n < ncmp - 1), tab_ref, head)

    @pl.when(qt == 0)
    def _():
        j = lax.broadcasted_iota(jnp.int32, (Q_TILE, Q_TILE), 0)
        i = lax.broadcasted_iota(jnp.int32, (Q_TILE, Q_TILE), 1)
        for r in range(HEADS_PER_KV):
            head = g * HEADS_PER_KV + r
            cols = slice(r * Q_TILE, (r + 1) * Q_TILE)
            for d in range(N_SEL_BIAS):
                dist = d * Q_TILE + i - j
                bs_ref[0, d, :, cols] = _bias_of_dist(dist, dist >= 0, tab_ref, head)
            dist = WINDOW + i - j
            bw_ref[0, 0, :, cols] = _bias_of_dist(dist, dist < WINDOW, tab_ref, head)


def _bias_tiles(rel_bias, seq):
    nq = seq // Q_TILE
    ncmp = seq // CMP_STRIDE
    rows = HEADS_PER_KV * Q_TILE
    return pl.pallas_call(
        _bias_body,
        grid=(N_KV, nq),
        in_specs=[pl.BlockSpec(memory_space=pltpu.SMEM)],
        out_specs=[pl.BlockSpec((1, 1, ncmp, rows), lambda g, q: (g, q, 0, 0)),
                   pl.BlockSpec((1, N_SEL_BIAS, Q_TILE, rows), lambda g, q: (g, 0, 0, 0)),
                   pl.BlockSpec((1, 1, Q_TILE, rows), lambda g, q: (g, 0, 0, 0))],
        out_shape=[jax.ShapeDtypeStruct((N_KV, nq, ncmp, rows), f32),
                   jax.ShapeDtypeStruct((N_KV, N_SEL_BIAS, Q_TILE, rows), f32),
                   jax.ShapeDtypeStruct((N_KV, 1, Q_TILE, rows), f32)],
        compiler_params=_params(("parallel", "arbitrary")),
        name="bias_tiles",
    )(rel_bias)


def _importance_matrix_t(seq):
    n_cmp = (seq - CMP_LEN) // CMP_STRIDE + 1
    n_sel = seq // SEL_BLOCK
    ratio_sel = SEL_BLOCK // CMP_STRIDE
    ratio_cmp = CMP_LEN // CMP_STRIDE
    jj = np.arange(n_sel)[:, None, None]
    ci = ratio_sel * jj + np.arange(ratio_sel)[None, :, None] - np.arange(ratio_cmp)[None, None, :]
    jb = np.broadcast_to(jj, ci.shape)
    ok = (ci >= 0) & (ci < n_cmp)
    m = np.zeros((n_sel, seq // CMP_STRIDE), np.float32)
    np.add.at(m, (jb[ok], ci[ok]), 1.0)
    return m


def _scores(qt, k_ref, t, first_tile, biases):
    s = _dot(k_ref[0, first_tile * Q_TILE:(t + 1) * Q_TILE, :], qt)
    pieces = []
    for kt in range(first_tile, t + 1):
        piece = s[(kt - first_tile) * Q_TILE:(kt - first_tile + 1) * Q_TILE, :]
        if t - kt in biases:
            piece = piece + biases[t - kt]
        pieces.append(piece)
    return pieces, jnp.max(functools.reduce(jnp.maximum, pieces), axis=0, keepdims=True)


def _weighted_values(pieces, m, vt_ref, t, first_tile):
    p = jnp.concatenate([jnp.exp2(pc - m).astype(bf16) for pc in pieces], axis=0)
    acc = _dot(vt_ref[0:V_ROWS, first_tile * Q_TILE:(t + 1) * Q_TILE], p)
    return acc[0:HEAD_DIM, :], 1.0 / acc[MASK_LO:MASK_LO + 1, :]


def _nsa_body(qt_ref, kc_ref, vct_ref, ks_ref, vst_ref, kw_ref, vwt_ref, gtt_ref, bc_ref, bs_ref, bw_ref,
              mt_ref, o_ref):
    n_sel = mt_ref.shape[0]
    n_tiles = qt_ref.shape[1] // Q_TILE
    cols = HEADS_PER_KV * Q_TILE
    head_cols = [slice(r * Q_TILE, (r + 1) * Q_TILE) for r in range(HEADS_PER_KV)]
    kc = kc_ref[0, 0]
    vct = vct_ref[0, 0]
    mt = mt_ref[...]
    jj = lax.broadcasted_iota(jnp.int32, (n_sel, Q_TILE), 0)
    qcol = lax.broadcasted_iota(jnp.int32, (n_sel, Q_TILE), 1)
    zeros_tail = jnp.zeros((LANES - HEAD_DIM, Q_TILE), bf16)
    zeros_rest = jnp.zeros((LANES - HEAD_DIM - n_sel, Q_TILE), bf16)

    def compressed_scores(t):
        tok = slice(t * Q_TILE, (t + 1) * Q_TILE)
        heads = [qt_ref[r * HEAD_DIM:(r + 1) * HEAD_DIM, tok] for r in range(HEADS_PER_KV)]
        q_plain = jnp.concatenate([jnp.concatenate([hd, zeros_tail], axis=0) for hd in heads], axis=1)
        return heads, q_plain, _dot(kc, q_plain) + bc_ref[0, t]

    def select_stage(t, heads, q_plain, s):
        p = jnp.exp2(s - jnp.max(s, axis=0, keepdims=True))
        norm = 1.0 / jnp.sum(p, axis=0, keepdims=True)
        if (t + 1) * Q_TILE > CMP_LEN - 1 >= t * Q_TILE:
            pos = t * Q_TILE + (lax.broadcasted_iota(jnp.int32, (1, cols), 1) & (Q_TILE - 1))
            norm = jnp.where(pos >= CMP_LEN - 1, norm, 0.0)
        p = p * norm
        o_c = _dot(vct[0:HEAD_DIM, :], p.astype(bf16))

        n_causal = ((t + 1) * Q_TILE - 1) // SEL_BLOCK + 1
        if n_causal <= N_SELECT:
            return q_plain, q_plain, o_c

        psum = p[:, head_cols[0]] + p[:, head_cols[1]] + p[:, head_cols[2]] + p[:, head_cols[3]]
        p_hi = psum.astype(bf16)
        p_lo = (psum - p_hi.astype(f32)).astype(bf16)
        imp = _dot(mt, p_hi) + _dot(mt, p_lo)
        dblk = (t * Q_TILE + qcol) // SEL_BLOCK - jj
        forced = (jj == 0) | ((dblk >= 0) & (dblk < N_LOCAL_FORCED))
        imp = jnp.where(forced, FORCE_SCORE, jnp.where(dblk >= 0, imp, -FORCE_SCORE))
        rank = jnp.zeros((n_sel, Q_TILE), f32)
        for i in range(n_causal):
            row = imp[i:i + 1, :]
            ahead = (row > imp) | ((row == imp) & (jj > i))
            rank = rank + jnp.where(ahead, 1.0, 0.0)
        mask_rows = jnp.where(rank < N_SELECT, 0.0, NEG_INF).astype(bf16)
        q_masked = jnp.concatenate(
            [jnp.concatenate([hd, mask_rows, zeros_rest], axis=0) for hd in heads], axis=1)
        return q_plain, q_masked, o_c

    first_win = lambda t: max(t - WINDOW // Q_TILE, 0)

    def sel_scores(t, q_masked):
        return _scores(q_masked, ks_ref, t, 0, {d: bs_ref[0, d] for d in range(N_SEL_BIAS)})

    def win_scores(t, q_plain):
        band = {d: bs_ref[0, d] for d in range(N_SEL_BIAS)}
        band[WINDOW // Q_TILE] = bw_ref[0, 0]
        return _scores(q_plain, kw_ref, t, first_win(t), band)

    def finish(t, o_c, sel_out, win_out):
        tok = slice(t * Q_TILE, (t + 1) * Q_TILE)
        acc_s, inv_s = sel_out
        acc_w, inv_w = win_out
        gt = jax.nn.sigmoid(gtt_ref[:, tok])
        gate = lambda br: jnp.concatenate([gt[3 * r + br:3 * r + br + 1, :] for r in range(HEADS_PER_KV)], axis=1)
        out_t = gate(0) * o_c + (gate(1) * inv_s) * acc_s + (gate(2) * inv_w) * acc_w
        pairs = [jnp.concatenate([out_t[:, head_cols[2 * h]], out_t[:, head_cols[2 * h + 1]]], axis=0).T
                 for h in range(HEADS_PER_KV // 2)]
        o_ref[0, tok, :] = jnp.concatenate(pairs, axis=-1)

    compressed = {t: compressed_scores(t) for t in range(min(3, n_tiles))}
    selected = {t: select_stage(t, *compressed.pop(t)) for t in range(min(2, n_tiles))}
    s_sel = {0: sel_scores(0, selected[0][1])}
    s_win = {0: win_scores(0, selected[0][0])}
    for t in range(n_tiles):
        o_c = selected.pop(t)[2]
        if t + 3 < n_tiles:
            compressed[t + 3] = compressed_scores(t + 3)
        if t + 1 < n_tiles:
            s_sel[t + 1] = sel_scores(t + 1, selected[t + 1][1])
            s_win[t + 1] = win_scores(t + 1, selected[t + 1][0])
        if t + 2 < n_tiles:
            selected[t + 2] = select_stage(t + 2, *compressed.pop(t + 2))
        sel_out = _weighted_values(*s_sel.pop(t), vst_ref, t, 0)
        win_out = _weighted_values(*s_win.pop(t), vwt_ref, t, first_win(t))
        finish(t, o_c, sel_out, win_out)


def _nsa(qt, kc, vct, ks, vst, kw, vwt, gtt, bc, bs, bw, mt):
    batch, seq, _ = ks.shape
    nq = seq // Q_TILE
    cols = HEADS_PER_KV * Q_TILE
    ncmp = kc.shape[2]
    width = HEADS_PER_KV * HEAD_DIM
    k_spec = pl.BlockSpec((1, seq, LANES), lambda gi, b: (b, 0, gi))
    vt_spec = pl.BlockSpec((LANES, seq), lambda gi, b: (gi, b))
    return pl.pallas_call(
        _nsa_body,
        grid=(N_KV, batch),
        in_specs=[pl.BlockSpec((width, seq), lambda gi, b: (gi, b)),
                  pl.BlockSpec((1, 1, ncmp, LANES), lambda gi, b: (b, gi, 0, 0)),
                  pl.BlockSpec((1, 1, LANES, ncmp), lambda gi, b: (b, gi, 0, 0)),
                  k_spec, vt_spec, k_spec, vt_spec,
                  pl.BlockSpec((GATE_ROWS, seq), lambda gi, b: (gi, b)),
                  pl.BlockSpec((1, nq, ncmp, cols), lambda gi, b: (gi, 0, 0, 0)),
                  pl.BlockSpec((1, N_SEL_BIAS, Q_TILE, cols), lambda gi, b: (gi, 0, 0, 0)),
                  pl.BlockSpec((1, 1, Q_TILE, cols), lambda gi, b: (gi, 0, 0, 0)),
                  _const_spec(mt.shape)],
        out_specs=pl.BlockSpec((1, seq, width), lambda gi, b: (b, 0, gi)),
        out_shape=jax.ShapeDtypeStruct((batch, seq, N_KV * width), f32),
        compiler_params=_params(("parallel", "arbitrary")),
        name="nsa",
    )(qt, kc, vct, ks, vst, kw, vwt, gtt, bc, bs, bw, mt)


def _out_proj_body(x_ref, yl_ref, yn_ref, gn_ref, w_ref, gp_ref, gm_ref, o_ref, a_ref):
    half = x_ref.shape[0] // 2
    rows = [slice(0, half), slice(half, 2 * half)]
    ys = []
    for r in rows:
        yn = _rms(yn_ref[r, :], gn_ref[...]).astype(bf16)
        ys.append(_dot(yl_ref[r, :], w_ref[0:LRU_WIDTH, :]) + _dot(yn, w_ref[LRU_WIDTH:, :]))
    for r, y in zip(rows, ys):
        h = x_ref[r, :] + _rms(y, gp_ref[...])
        o_ref[r, :] = h
        a_ref[r, :] = _rms(h, gm_ref[...]).astype(bf16)


def _out_proj(x2, yl, yn, gn, w, gp, gm, tm=512):
    m = x2.shape[0]
    row = pl.BlockSpec((tm, D_MODEL), lambda i: (i, 0))
    return pl.pallas_call(
        _out_proj_body,
        grid=(m // tm,),
        in_specs=[row,
                  pl.BlockSpec((tm, LRU_WIDTH), lambda i: (i, 0)),
                  pl.BlockSpec((tm, D_MODEL - LRU_WIDTH), lambda i: (i, 0)),
                  _const_spec((1, D_MODEL - LRU_WIDTH)),
                  _const_spec((D_MODEL, D_MODEL)),
                  _const_spec((1, D_MODEL)), _const_spec((1, D_MODEL))],
        out_specs=[row, row],
        out_shape=[jax.ShapeDtypeStruct((m, D_MODEL), f32), jax.ShapeDtypeStruct((m, D_MODEL), bf16)],
        compiler_params=_params(("parallel",)),
        name="out_proj",
    )(x2, yl, yn, gn, w, gp, gm)


def _mlp_body(a_ref, w1_ref, w2_ref, g2_ref, o_ref, acc_sc):
    j = pl.program_id(1)

    @pl.when(j == 0)
    def _():
        acc_sc[...] = jnp.zeros_like(acc_sc)

    hid = jnp.maximum(_dot(a_ref[...], w1_ref[...]), 0.0)
    acc_sc[...] += _dot((hid * hid).astype(bf16), w2_ref[...])

    @pl.when(j == pl.num_programs(1) - 1)
    def _():
        o_ref[...] = _rms(acc_sc[...], g2_ref[...])


def _mlp(a, w1, w2, g2, tm=1024, tf=1024):
    m = a.shape[0]
    return pl.pallas_call(
        _mlp_body,
        grid=(m // tm, D_FF // tf),
        in_specs=[pl.BlockSpec((tm, D_MODEL), lambda i, j: (i, 0)),
                  pl.BlockSpec((D_MODEL, tf), lambda i, j: (0, j)),
                  pl.BlockSpec((tf, D_MODEL), lambda i, j: (j, 0)),
                  _const_spec((1, D_MODEL))],
        out_specs=pl.BlockSpec((tm, D_MODEL), lambda i, j: (i, 0)),
        out_shape=jax.ShapeDtypeStruct((m, D_MODEL), f32),
        scratch_shapes=[pltpu.VMEM((tm, D_MODEL), f32)],
        compiler_params=_params(("parallel", "arbitrary")),
        name="mlp",
    )(a, w1, w2, g2)


def _ple_body(h_ref, f_ref, p_ref, wg_ref, wp_ref, o_ref):
    h = h_ref[...] + f_ref[...]
    gate = jax.nn.sigmoid(_dot(h.astype(bf16), wg_ref[...]))
    o_ref[...] = h + gate * _dot(p_ref[...].astype(bf16), wp_ref[...])


def _ple(h, f, p2, wg, wp, tm=512):
    m = h.shape[0]
    row = pl.BlockSpec((tm, D_MODEL), lambda i: (i, 0))
    return pl.pallas_call(
        _ple_body,
        grid=(m // tm,),
        in_specs=[row, row,
                  pl.BlockSpec((tm, PLE_DIM), lambda i: (i, 0)),
                  _const_spec((D_MODEL, D_MODEL)),
                  _const_spec((PLE_DIM, D_MODEL))],
        out_specs=row,
        out_shape=jax.ShapeDtypeStruct((m, D_MODEL), f32),
        compiler_params=_params(("parallel",)),
        name="ple",
    )(h, f, p2, wg, wp)


def _block_diag_chunks(w):
    per = 256 // LRU_BLOCK_DIM
    w = w.reshape(LRU_BLOCKS // per, per, LRU_BLOCK_DIM, LRU_BLOCK_DIM)
    eye = jnp.eye(per, dtype=w.dtype)
    return jnp.einsum('cpij,pq->cpiqj', w, eye).reshape(LRU_BLOCKS // per, 256, 256)


def _layer(h, p_i, i, prm, bias):
    batch, seq, _ = h.shape
    m = batch * seq
    x2 = h.reshape(m, D_MODEL)
    row = lambda v: v.reshape(1, -1)
    per_batch = lambda t: t.reshape(batch, seq, t.shape[-1])

    w_in = prm["w_in"][i].astype(bf16)
    w_tok = jnp.concatenate([w_in[:, 0:2048], w_in[:, 3072:3840], w_in[:, 4096:4352]], axis=1)
    per = 3 * HEADS_PER_KV
    w_gate = jnp.pad(w_in[:, GATE_LO:IN_DIM].reshape(D_MODEL, N_KV, per), ((0, 0), (0, 0), (0, GATE_ROWS - per)))
    w_feat = jnp.concatenate([w_in[:, 2048:3072], w_in[:, 3840:4096], w_in[:, 4352:4608],
                              w_gate.reshape(D_MODEL, N_KV * GATE_ROWS)], axis=1).T
    u, gl, kc, vc, ks, kw, qt, vst, vwt, gtt = _in_proj(x2, row(prm["norm_mix_pre"][i]), w_tok, w_feat, seq)

    y_lru = _rglru(u, gl, prm["conv_w"][i], row(prm["conv_b"][i]),
                   _block_diag_chunks(prm["lru_wa"][i]).astype(bf16), row(prm["lru_ba"][i]),
                   _block_diag_chunks(prm["lru_wx"][i]).astype(bf16), row(prm["lru_bx"][i]),
                   row(prm["lru_lambda"][i]), row(prm["gnorm_lru"][i]), batch, seq)

    lane_pad = lambda w: jnp.pad(w, ((0, 0), (0, LANES - HEAD_DIM))).astype(bf16)
    kcc, vcct = _compress(per_batch(kc), per_batch(vc),
                          prm["cmp_pe_k"][i].reshape(1, -1), prm["cmp_w1_k"][i].astype(bf16),
                          lane_pad(prm["cmp_w2_k"][i]),
                          prm["cmp_pe_v"][i].reshape(1, -1), prm["cmp_w1_v"][i].astype(bf16),
                          lane_pad(prm["cmp_w2_v"][i]).T)

    bc, bs, bw = bias
    y_nsa = _nsa(qt, kcc, vcct, per_batch(ks), vst, per_batch(kw), vwt, gtt, bc, bs, bw,
                 jnp.asarray(_importance_matrix_t(seq)).astype(bf16))

    h1, a1 = _out_proj(x2, y_lru, y_nsa.reshape(m, -1), row(prm["gnorm_nsa"][i]),
                       prm["w_out"][i].astype(bf16), row(prm["norm_mix_post"][i]), row(prm["norm_mlp_pre"][i]))
    f = _mlp(a1, prm["mlp_w1"][i].astype(bf16), prm["mlp_w2"][i].astype(bf16), row(prm["norm_mlp_post"][i]))
    h3 = _ple(h1, f, p_i.reshape(m, PLE_DIM), prm["ple_gate"][i].astype(bf16), prm["ple_proj"][i].astype(bf16))
    return h3.reshape(batch, seq, D_MODEL)


def kernel(x, p, norm_mix_pre, norm_mix_post, norm_mlp_pre, norm_mlp_post, w_in, conv_w, conv_b, lru_wa, lru_ba, lru_wx, lru_bx, lru_lambda, cmp_pe_k, cmp_w1_k, cmp_w2_k, cmp_pe_v, cmp_w1_v, cmp_w2_v, rel_bias, gnorm_lru, gnorm_nsa, w_out, mlp_w1, mlp_w2, ple_gate, ple_proj):
    prm = dict(norm_mix_pre=norm_mix_pre, norm_mix_post=norm_mix_post, norm_mlp_pre=norm_mlp_pre,
               norm_mlp_post=norm_mlp_post, w_in=w_in, conv_w=conv_w, conv_b=conv_b, lru_wa=lru_wa,
               lru_ba=lru_ba, lru_wx=lru_wx, lru_bx=lru_bx, lru_lambda=lru_lambda, cmp_pe_k=cmp_pe_k,
               cmp_w1_k=cmp_w1_k, cmp_w2_k=cmp_w2_k, cmp_pe_v=cmp_pe_v, cmp_w1_v=cmp_w1_v,
               cmp_w2_v=cmp_w2_v, gnorm_lru=gnorm_lru, gnorm_nsa=gnorm_nsa, w_out=w_out, mlp_w1=mlp_w1,
               mlp_w2=mlp_w2, ple_gate=ple_gate, ple_proj=ple_proj)
    bias = _bias_tiles(rel_bias, x.shape[1])
    h = x
    for i in range(w_in.shape[0]):
        h = _layer(h, p[i], i, prm, bias)
    return h
```

```python
import functools
import math

import numpy as np
import jax
import jax.numpy as jnp
from jax import lax
from jax.experimental import pallas as pl
from jax.experimental.pallas import tpu as pltpu

D_MODEL = 2048
PLE_DIM = 256
LRU_WIDTH = 1024
LRU_BLOCKS = 16
LRU_BLOCK_DIM = 64
CONV_WIDTH = 4
LRU_C = 8.0
HEAD_DIM = 64
N_HEADS = 16
N_KV = 4
HEADS_PER_KV = 4
CMP_LEN = 32
CMP_STRIDE = 16
CMP_HIDDEN = 256
SEL_BLOCK = 64
N_SELECT = 16
N_LOCAL_FORCED = 2
WINDOW = 512
N_BUCKETS = 32
MAX_DISTANCE = 128
D_FF = 4 * D_MODEL
NORM_EPS = 1e-6
SQRT_GUARD = 1e-30
NEG_INF = -1e30
FORCE_SCORE = 1e4
KV_W = N_KV * HEAD_DIM


def _offsets(sizes):
    out, pos = {}, 0
    for name, width in sizes:
        out[name] = (pos, pos + width)
        pos += width
    return out, pos


IN_COLS, IN_DIM = _offsets([("u", LRU_WIDTH), ("gl", LRU_WIDTH), ("q", N_HEADS * HEAD_DIM), ("kc", KV_W),
                            ("vc", KV_W), ("ks", KV_W), ("vs", KV_W), ("kw", KV_W), ("vw", KV_W),
                            ("gates", 3 * N_HEADS)])
GATE_ROWS = 16
TOK_STREAMS = ("u", "gl", "kc", "vc", "ks", "kw")
FEAT_STREAMS = ("q", "vs", "vw")
TOK_COLS, TOK_END = _offsets([(n, IN_COLS[n][1] - IN_COLS[n][0]) for n in TOK_STREAMS])
FEAT_ROWS, FEAT_END = _offsets([(n, IN_COLS[n][1] - IN_COLS[n][0]) for n in FEAT_STREAMS]
                               + [("gates", N_KV * GATE_ROWS)])

SUBLANES = 8
GATE_CHUNK = 256
LANES = 128
MASK_LO = HEAD_DIM
Q_TILE = 128
N_SEL_BIAS = 2
V_ROWS = 80
LOG2E = math.log2(math.e)

VMEM_LIMIT = 56 * 1024 * 1024

f32 = jnp.float32
bf16 = jnp.bfloat16


def _rms(x, g):
    return x * lax.rsqrt(jnp.mean(x * x, axis=-1, keepdims=True) + NORM_EPS) * g


def _dot(a, b):
    return jnp.dot(a, b, preferred_element_type=f32)


def _dot_nt(a, b):
    return lax.dot_general(a, b, (((1,), (1,)), ((), ())), preferred_element_type=f32)


def _dot_tn(a, b):
    return lax.dot_general(a, b, (((0,), (0,)), ((), ())), preferred_element_type=f32)


def _const_spec(shape):
    nd = len(shape)
    return pl.BlockSpec(shape, lambda *_: (0,) * nd)


def _params(sem):
    return pltpu.CompilerParams(dimension_semantics=sem, vmem_limit_bytes=VMEM_LIMIT)


def _spread_groups(z, fill):
    lane = lax.broadcasted_iota(jnp.int32, (z.shape[0], LANES), 1)
    parts = []
    for g in range(N_KV):
        pair = z[:, (g // 2) * LANES:(g // 2 + 1) * LANES]
        if g % 2:
            pair = pltpu.roll(pair, HEAD_DIM, 1)
        parts.append(jnp.where(lane < HEAD_DIM, pair, fill))
    return jnp.concatenate(parts, axis=1)


def _value_rows(z):
    tm = z.shape[1]
    ones_row = jnp.where(lax.broadcasted_iota(jnp.int32, (LANES - HEAD_DIM, tm), 0) == 0, 1.0, 0.0)
    parts = []
    for g in range(N_KV):
        parts += [z[g * HEAD_DIM:(g + 1) * HEAD_DIM, :], ones_row]
    return jnp.concatenate(parts, axis=0)


def _in_proj_body(seq, x_ref, g_ref, w_ref, wt_ref, u_ref, gl_ref, kc_ref, vc_ref, ks_ref, kw_ref,
                  qt_ref, vst_ref, vwt_ref, gtt_ref):
    tm = x_ref.shape[0]
    a = _rms(x_ref[...], g_ref[...]).astype(bf16)

    def tok(name):
        lo, hi = TOK_COLS[name]
        return _dot(a, w_ref[:, lo:hi])

    def feat(name):
        lo, hi = FEAT_ROWS[name]
        return _dot_nt(wt_ref[lo:hi, :], a)

    u_ref[...] = tok("u")
    gl_ref[...] = tok("gl")
    kc_ref[...] = tok("kc")
    vc_ref[...] = tok("vc")
    lane = lax.broadcasted_iota(jnp.int32, (tm, LANES), 1)
    pos = (pl.program_id(0) * tm) % seq + lax.broadcasted_iota(jnp.int32, (tm, LANES), 0)
    block_onehot = jnp.where(lane - MASK_LO == pos // SEL_BLOCK, 1.0, 0.0)
    ks_ref[...] = _spread_groups(tok("ks"), block_onehot).astype(bf16)
    kw_ref[...] = _spread_groups(tok("kw"), 0.0).astype(bf16)

    qt_ref[...] = (feat("q") * (LOG2E * HEAD_DIM ** -0.5)).astype(bf16)
    vst_ref[...] = _value_rows(feat("vs")).astype(bf16)
    vwt_ref[...] = _value_rows(feat("vw")).astype(bf16)
    gtt_ref[...] = feat("gates")


def _in_proj(x2, g, w, wt, seq, tm=512):
    m = x2.shape[0]
    wide = N_KV * LANES
    tok_out = [(LRU_WIDTH, f32), (LRU_WIDTH, f32), (KV_W, f32), (KV_W, f32), (wide, bf16), (wide, bf16)]
    feat_out = [(N_HEADS * HEAD_DIM, bf16), (wide, bf16), (wide, bf16), (N_KV * GATE_ROWS, f32)]
    body = lambda *refs: _in_proj_body(seq, *refs)
    return pl.pallas_call(
        body,
        grid=(m // tm,),
        in_specs=[pl.BlockSpec((tm, D_MODEL), lambda i: (i, 0)),
                  _const_spec((1, D_MODEL)),
                  pl.BlockSpec((D_MODEL, TOK_END), lambda i: (0, 0), pipeline_mode=pl.Buffered(1)),
                  pl.BlockSpec((FEAT_END, D_MODEL), lambda i: (0, 0), pipeline_mode=pl.Buffered(1))],
        out_specs=([pl.BlockSpec((tm, n), lambda i: (i, 0)) for n, _ in tok_out]
                   + [pl.BlockSpec((n, tm), lambda i: (0, i)) for n, _ in feat_out]),
        out_shape=([jax.ShapeDtypeStruct((m, n), dt) for n, dt in tok_out]
                   + [jax.ShapeDtypeStruct((n, m), dt) for n, dt in feat_out]),
        compiler_params=_params(("parallel",)),
        name="in_proj",
    )(x2, g, w, wt)


def _rglru_body(u_ref, gl_ref, cw_ref, cb_ref, wa_ref, ba_ref, wx_ref, bx_ref, lam_ref, gn_ref,
                o_ref, ubuf, a_sc, b_sc, hc_sc):
    t = u_ref.shape[0]

    @pl.when(pl.program_id(1) == 0)
    def _():
        ubuf[...] = jnp.zeros_like(ubuf)
        hc_sc[...] = jnp.zeros_like(hc_sc)

    u = u_ref[...]
    u3 = u.reshape(t // SUBLANES, SUBLANES, LRU_WIDTH)
    tail = ubuf[...]
    cw = cw_ref[...]
    row8w = lax.broadcasted_iota(jnp.int32, (t // SUBLANES, SUBLANES, LRU_WIDTH), 1)
    xc3 = cb_ref[...] + u3 * cw[CONV_WIDTH - 1]
    for d in range(1, CONV_WIDTH):
        cur = pltpu.roll(u3, d, 1)
        prev = jnp.concatenate([pltpu.roll(tail, d, 0)[None], cur[:-1]], axis=0)
        xc3 = xc3 + jnp.where(row8w >= d, cur, prev) * cw[CONV_WIDTH - 1 - d]
    ubuf[...] = u[t - SUBLANES:t, :]
    xc = xc3.reshape(t, LRU_WIDTH)

    xb = xc.astype(bf16)
    sp = jax.nn.softplus(-lam_ref[...])
    row8 = lax.broadcasted_iota(jnp.int32, (t // SUBLANES, SUBLANES, GATE_CHUNK), 1)
    for c in range(LRU_WIDTH // GATE_CHUNK):
        sl = slice(c * GATE_CHUNK, (c + 1) * GATE_CHUNK)
        xcb = xb[:, sl]
        r = jax.nn.sigmoid(_dot(xcb, wa_ref[c]) + ba_ref[:, sl])
        ig = jax.nn.sigmoid(_dot(xcb, wx_ref[c]) + bx_ref[:, sl])
        log_a = (-LRU_C) * r * sp[:, sl]
        a = jnp.exp(log_a)
        y = 1.0 - a * a
        b = (y * lax.rsqrt(jnp.maximum(y, SQRT_GUARD))) * (ig * xc[:, sl])
        a = a.reshape(t // SUBLANES, SUBLANES, GATE_CHUNK)
        b = b.reshape(t // SUBLANES, SUBLANES, GATE_CHUNK)
        for d in (1, 2, 4):
            keep = row8 >= d
            a_prev = pltpu.roll(a, d, 1)
            b_prev = pltpu.roll(b, d, 1)
            b = jnp.where(keep, a * b_prev + b, b)
            a = jnp.where(keep, a * a_prev, a)
        a_sc[:, sl] = a.reshape(t, GATE_CHUNK)
        b_sc[:, sl] = b.reshape(t, GATE_CHUNK)

    def group(gi, h):
        rows = pl.ds(pl.multiple_of(gi * SUBLANES, SUBLANES), SUBLANES)
        hg = b_sc[rows, :] + a_sc[rows, :] * h
        b_sc[rows, :] = hg
        return hg[SUBLANES - 1:SUBLANES, :]

    h_last = lax.fori_loop(0, t // SUBLANES, group, hc_sc[0:1, :])
    hc_sc[0:1, :] = h_last
    y = b_sc[...] * jax.nn.gelu(gl_ref[...])
    o_ref[...] = _rms(y, gn_ref[...]).astype(o_ref.dtype)


def _rglru(u, gl, cw, cb, wa, ba, wx, bx, lam, gn, batch, seq, t=256):
    ns = seq // t
    row = pl.BlockSpec((t, LRU_WIDTH), lambda b, s: (b * ns + s, 0))
    vec = _const_spec((1, LRU_WIDTH))
    wspec = _const_spec((LRU_WIDTH // GATE_CHUNK, GATE_CHUNK, GATE_CHUNK))
    return pl.pallas_call(
        _rglru_body,
        grid=(batch, ns),
        in_specs=[row, row, _const_spec((CONV_WIDTH, LRU_WIDTH)), vec, wspec, vec, wspec, vec, vec, vec],
        out_specs=row,
        out_shape=jax.ShapeDtypeStruct((batch * seq, LRU_WIDTH), bf16),
        scratch_shapes=[pltpu.VMEM((SUBLANES, LRU_WIDTH), f32), pltpu.VMEM((t, LRU_WIDTH), f32),
                        pltpu.VMEM((t, LRU_WIDTH), f32), pltpu.VMEM((SUBLANES, LRU_WIDTH), f32)],
        compiler_params=_params(("parallel", "arbitrary")),
        name="rglru",
    )(u, gl, cw, cb, wa, ba, wx, bx, lam, gn)


def _compress_body(kc_ref, vc_ref, pek_ref, w1k_ref, w2k_ref, pev_ref, w1v_ref, w2v_ref, ko_ref, vo_ref):
    nrow = ko_ref.shape[2]
    rows = lax.broadcasted_iota(jnp.int32, (nrow, LANES), 0)
    cols = lax.broadcasted_iota(jnp.int32, (LANES, nrow), 1)
    lane64 = lax.broadcasted_iota(jnp.int32, (nrow, LANES), 1) < HEAD_DIM
    half = CMP_STRIDE * HEAD_DIM

    def one(t_ref, pe_ref, w1_ref, w2_ref, o_ref, transposed):
        pe = jnp.broadcast_to(pe_ref[...], (8, CMP_LEN * HEAD_DIM)).astype(bf16)
        c0 = _dot(pe, w1_ref[...])[0:1, :]
        toks = [t_ref[0, pl.ds(j, nrow, stride=CMP_STRIDE), :] for j in range(CMP_STRIDE)]
        for g in range(LANES // HEAD_DIM):
            blocks = []
            for i in range(CMP_STRIDE // 2):
                even, odd = toks[2 * i], toks[2 * i + 1]
                if g == 0:
                    blocks.append(jnp.where(lane64, even, pltpu.roll(odd, HEAD_DIM, 1)))
                else:
                    blocks.append(jnp.where(lane64, pltpu.roll(even, HEAD_DIM, 1), odd))
            chunk = jnp.concatenate(blocks, axis=1).astype(bf16)
            lo = _dot(chunk, w1_ref[0:half, :])
            hi = _dot(chunk, w1_ref[half:2 * half, :])
            hid = jax.nn.gelu(lo + pltpu.roll(hi, nrow - 1, 0) + c0)
            if transposed:
                out = _dot_nt(w2_ref[...], hid.astype(bf16))
                o_ref[0, g] = jnp.where(cols < nrow - 1, out, 0.0).astype(o_ref.dtype)
            else:
                out = _dot(hid.astype(bf16), w2_ref[...])
                o_ref[0, g] = jnp.where(rows < nrow - 1, out, 0.0).astype(o_ref.dtype)

    one(kc_ref, pek_ref, w1k_ref, w2k_ref, ko_ref, False)
    one(vc_ref, pev_ref, w1v_ref, w2v_ref, vo_ref, True)


def _compress(kc, vc, pek, w1k, w2k, pev, w1v, w2v):
    batch, seq, width = kc.shape
    nrow = seq // CMP_STRIDE
    pair = LANES // HEAD_DIM
    tspec = pl.BlockSpec((1, seq, LANES), lambda b, h: (b, 0, h))
    ospec = pl.BlockSpec((1, pair, nrow, LANES), lambda b, h: (b, h, 0, 0))
    wts = [_const_spec((1, CMP_LEN * HEAD_DIM)), _const_spec((CMP_LEN * HEAD_DIM, CMP_HIDDEN))]
    tspec_v = pl.BlockSpec((1, pair, LANES, nrow), lambda b, h: (b, h, 0, 0))
    return pl.pallas_call(
        _compress_body,
        grid=(batch, width // LANES),
        in_specs=([tspec, tspec] + wts + [_const_spec((CMP_HIDDEN, LANES))]
                  + wts + [_const_spec((LANES, CMP_HIDDEN))]),
        out_specs=[ospec, tspec_v],
        out_shape=[jax.ShapeDtypeStruct((batch, N_KV, nrow, LANES), bf16),
                   jax.ShapeDtypeStruct((batch, N_KV, LANES, nrow), bf16)],
        compiler_params=_params(("parallel", "parallel")),
        name="compress",
    )(kc, vc, pek, w1k, w2k, pev, w1v, w2v)


def _bucket_thresholds():
    n = np.arange(0, 4096)
    max_exact = N_BUCKETS // 2
    nf = np.maximum(n, 1).astype(np.float32)
    large = max_exact + (np.log(nf / np.float32(max_exact)) / np.float32(math.log(MAX_DISTANCE / max_exact))
                         * np.float32(N_BUCKETS - max_exact)).astype(np.int32)
    large = np.minimum(large, N_BUCKETS - 1)
    bucket = np.where(n < max_exact, n, large)
    assert np.all(np.diff(bucket) >= 0) and bucket[0] == 0 and bucket[-1] == N_BUCKETS - 1
    return [int(np.argmax(bucket >= k)) for k in range(N_BUCKETS)]


_BUCKET_THR = _bucket_thresholds()
assert _BUCKET_THR[-1] <= (N_SEL_BIAS - 1) * Q_TILE + 1


def _bias_of_dist(dist, ok, tab_ref, head):
    last = tab_ref[N_BUCKETS - 1, head]
    val = jnp.full(dist.shape, (tab_ref[0, head] - last) * LOG2E, f32)
    for k in range(1, N_BUCKETS - 1):
        val = jnp.where(dist >= _BUCKET_THR[k], (tab_ref[k, head] - last) * LOG2E, val)
    val = jnp.where(dist >= _BUCKET_THR[N_BUCKETS - 1], 0.0, val)
    return jnp.where(ok, val, NEG_INF)


def _bias_body(tab_ref, bc_ref, bs_ref, bw_ref):
    g = pl.program_id(0)
    qt = pl.program_id(1)
    ncmp = bc_ref.shape[2]
    for r in range(HEADS_PER_KV):
        head = g * HEADS_PER_KV + r
        cols = slice(r * Q_TILE, (r + 1) * Q_TILE)
        n = lax.broadcasted_iota(jnp.int32, (ncmp, Q_TILE), 0)
        i = lax.broadcasted_iota(jnp.int32, (ncmp, Q_TILE), 1)
        dist = qt * Q_TILE + i - (n * CMP_STRIDE + CMP_LEN - 1)
        bc_ref[0, 0, :, cols] = _bias_of_dist(dist, (dist >= 0) & (n < ncmp - 1), tab_ref, head)

    @pl.when(qt == 0)
    def _():
        j = lax.broadcasted_iota(jnp.int32, (Q_TILE, Q_TILE), 0)
        i = lax.broadcasted_iota(jnp.int32, (Q_TILE, Q_TILE), 1)
        for r in range(HEADS_PER_KV):
            head = g * HEADS_PER_KV + r
            cols = slice(r * Q_TILE, (r + 1) * Q_TILE)
            for d in range(N_SEL_BIAS):
                dist = d * Q_TILE + i - j
                bs_ref[0, d, :, cols] = _bias_of_dist(dist, dist >= 0, tab_ref, head)
            dist = WINDOW + i - j
            bw_ref[0, 0, :, cols] = _bias_of_dist(dist, dist < WINDOW, tab_ref, head)


def _bias_tiles(rel_bias, seq):
    nq = seq // Q_TILE
    ncmp = seq // CMP_STRIDE
    rows = HEADS_PER_KV * Q_TILE
    return pl.pallas_call(
        _bias_body,
        grid=(N_KV, nq),
        in_specs=[pl.BlockSpec(memory_space=pltpu.SMEM)],
        out_specs=[pl.BlockSpec((1, 1, ncmp, rows), lambda g, q: (g, q, 0, 0)),
                   pl.BlockSpec((1, N_SEL_BIAS, Q_TILE, rows), lambda g, q: (g, 0, 0, 0)),
                   pl.BlockSpec((1, 1, Q_TILE, rows), lambda g, q: (g, 0, 0, 0))],
        out_shape=[jax.ShapeDtypeStruct((N_KV, nq, ncmp, rows), f32),
                   jax.ShapeDtypeStruct((N_KV, N_SEL_BIAS, Q_TILE, rows), f32),
                   jax.ShapeDtypeStruct((N_KV, 1, Q_TILE, rows), f32)],
        compiler_params=_params(("parallel", "arbitrary")),
        name="bias_tiles",
    )(rel_bias)


def _importance_matrix_t(seq):
    n_cmp = (seq - CMP_LEN) // CMP_STRIDE + 1
    n_sel = seq // SEL_BLOCK
    ratio_sel = SEL_BLOCK // CMP_STRIDE
    ratio_cmp = CMP_LEN // CMP_STRIDE
    jj = np.arange(n_sel)[:, None, None]
    ci = ratio_sel * jj + np.arange(ratio_sel)[None, :, None] - np.arange(ratio_cmp)[None, None, :]
    jb = np.broadcast_to(jj, ci.shape)
    ok = (ci >= 0) & (ci < n_cmp)
    m = np.zeros((n_sel, seq // CMP_STRIDE), np.float32)
    np.add.at(m, (jb[ok], ci[ok]), 1.0)
    return m


def _scores(qt, k_ref, t, first_tile, biases):
    s = _dot(k_ref[0, first_tile * Q_TILE:(t + 1) * Q_TILE, :], qt)
    pieces = []
    for kt in range(first_tile, t + 1):
        piece = s[(kt - first_tile) * Q_TILE:(kt - first_tile + 1) * Q_TILE, :]
        if t - kt in biases:
            piece = piece + biases[t - kt]
        pieces.append(piece)
    return pieces, jnp.max(functools.reduce(jnp.maximum, pieces), axis=0, keepdims=True)


def _weighted_values(pieces, m, vt_ref, t, first_tile):
    p = jnp.concatenate([jnp.exp2(pc - m).astype(bf16) for pc in pieces], axis=0)
    acc = _dot(vt_ref[0:V_ROWS, first_tile * Q_TILE:(t + 1) * Q_TILE], p)
    return acc[0:HEAD_DIM, :], 1.0 / acc[MASK_LO:MASK_LO + 1, :]


def _nsa_body(qt_ref, kc_ref, vct_ref, ks_ref, vst_ref, kw_ref, vwt_ref, gtt_ref, bc_ref, bs_ref, bw_ref,
              mt_ref, o_ref):
    n_sel = mt_ref.shape[0]
    n_tiles = qt_ref.shape[1] // Q_TILE
    cols = HEADS_PER_KV * Q_TILE
    head_cols = [slice(r * Q_TILE, (r + 1) * Q_TILE) for r in range(HEADS_PER_KV)]
    kc = kc_ref[0, 0]
    vct = vct_ref[0, 0]
    mt = mt_ref[...]
    jj = lax.broadcasted_iota(jnp.int32, (n_sel, Q_TILE), 0)
    qcol = lax.broadcasted_iota(jnp.int32, (n_sel, Q_TILE), 1)
    zeros_tail = jnp.zeros((LANES - HEAD_DIM, Q_TILE), bf16)
    zeros_rest = jnp.zeros((LANES - HEAD_DIM - n_sel, Q_TILE), bf16)

    def compressed_scores(t):
        tok = slice(t * Q_TILE, (t + 1) * Q_TILE)
        heads = [qt_ref[r * HEAD_DIM:(r + 1) * HEAD_DIM, tok] for r in range(HEADS_PER_KV)]
        q_plain = jnp.concatenate([jnp.concatenate([hd, zeros_tail], axis=0) for hd in heads], axis=1)
        return heads, q_plain, _dot(kc, q_plain) + bc_ref[0, t]

    def select_stage(t, heads, q_plain, s):
        p = jnp.exp2(s - jnp.max(s, axis=0, keepdims=True))
        norm = 1.0 / jnp.sum(p, axis=0, keepdims=True)
        if (t + 1) * Q_TILE > CMP_LEN - 1 >= t * Q_TILE:
            pos = t * Q_TILE + (lax.broadcasted_iota(jnp.int32, (1, cols), 1) & (Q_TILE - 1))
            norm = jnp.where(pos >= CMP_LEN - 1, norm, 0.0)
        p = p * norm
        o_c = _dot(vct[0:HEAD_DIM, :], p.astype(bf16))

        n_causal = ((t + 1) * Q_TILE - 1) // SEL_BLOCK + 1
        if n_causal <= N_SELECT:
            return q_plain, q_plain, o_c

        psum = p[:, head_cols[0]] + p[:, head_cols[1]] + p[:, head_cols[2]] + p[:, head_cols[3]]
        p_hi = psum.astype(bf16)
        p_lo = (psum - p_hi.astype(f32)).astype(bf16)
        imp = _dot(mt, p_hi) + _dot(mt, p_lo)
        dblk = (t * Q_TILE + qcol) // SEL_BLOCK - jj
        forced = (jj == 0) | ((dblk >= 0) & (dblk < N_LOCAL_FORCED))
        imp = jnp.where(forced, FORCE_SCORE, jnp.where(dblk >= 0, imp, -FORCE_SCORE))
        rank = jnp.zeros((n_sel, Q_TILE), f32)
        for i in range(n_causal):
            row = imp[i:i + 1, :]
            ahead = (row > imp) | ((row == imp) & (jj > i))
            rank = rank + jnp.where(ahead, 1.0, 0.0)
        mask_rows = jnp.where(rank < N_SELECT, 0.0, NEG_INF).astype(bf16)
        q_masked = jnp.concatenate(
            [jnp.concatenate([hd, mask_rows, zeros_rest], axis=0) for hd in heads], axis=1)
        return q_plain, q_masked, o_c

    first_win = lambda t: max(t - WINDOW // Q_TILE, 0)

    def sel_scores(t, q_masked):
        return _scores(q_masked, ks_ref, t, 0, {d: bs_ref[0, d] for d in range(N_SEL_BIAS)})

    def win_scores(t, q_plain):
        band = {d: bs_ref[0, d] for d in range(N_SEL_BIAS)}
        band[WINDOW // Q_TILE] = bw_ref[0, 0]
        return _scores(q_plain, kw_ref, t, first_win(t), band)

    def finish(t, o_c, sel_out, win_out):
        tok = slice(t * Q_TILE, (t + 1) * Q_TILE)
        acc_s, inv_s = sel_out
        acc_w, inv_w = win_out
        gt = jax.nn.sigmoid(gtt_ref[:, tok])
        gate = lambda br: jnp.concatenate([gt[3 * r + br:3 * r + br + 1, :] for r in range(HEADS_PER_KV)], axis=1)
        out_t = gate(0) * o_c + (gate(1) * inv_s) * acc_s + (gate(2) * inv_w) * acc_w
        pairs = [jnp.concatenate([out_t[:, head_cols[2 * h]], out_t[:, head_cols[2 * h + 1]]], axis=0).T
                 for h in range(HEADS_PER_KV // 2)]
        o_ref[0, tok, :] = jnp.concatenate(pairs, axis=-1)

    compressed = {t: compressed_scores(t) for t in range(min(3, n_tiles))}
    selected = {t: select_stage(t, *compressed.pop(t)) for t in range(min(2, n_tiles))}
    s_sel = {0: sel_scores(0, selected[0][1])}
    s_win = {0: win_scores(0, selected[0][0])}
    for t in range(n_tiles):
        o_c = selected.pop(t)[2]
        if t + 3 < n_tiles:
            compressed[t + 3] = compressed_scores(t + 3)
        if t + 1 < n_tiles:
            s_sel[t + 1] = sel_scores(t + 1, selected[t + 1][1])
            s_win[t + 1] = win_scores(t + 1, selected[t + 1][0])
        if t + 2 < n_tiles:
            selected[t + 2] = select_stage(t + 2, *compressed.pop(t + 2))
        sel_out = _weighted_values(*s_sel.pop(t), vst_ref, t, 0)
        win_out = _weighted_values(*s_win.pop(t), vwt_ref, t, first_win(t))
        finish(t, o_c, sel_out, win_out)


def _nsa(qt, kc, vct, ks, vst, kw, vwt, gtt, bc, bs, bw, mt):
    batch, seq, _ = ks.shape
    nq = seq // Q_TILE
    cols = HEADS_PER_KV * Q_TILE
    ncmp = kc.shape[2]
    width = HEADS_PER_KV * HEAD_DIM
    k_spec = pl.BlockSpec((1, seq, LANES), lambda gi, b: (b, 0, gi))
    vt_spec = pl.BlockSpec((LANES, seq), lambda gi, b: (gi, b))
    return pl.pallas_call(
        _nsa_body,
        grid=(N_KV, batch),
        in_specs=[pl.BlockSpec((width, seq), lambda gi, b: (gi, b)),
                  pl.BlockSpec((1, 1, ncmp, LANES), lambda gi, b: (b, gi, 0, 0)),
                  pl.BlockSpec((1, 1, LANES, ncmp), lambda gi, b: (b, gi, 0, 0)),
                  k_spec, vt_spec, k_spec, vt_spec,
                  pl.BlockSpec((GATE_ROWS, seq), lambda gi, b: (gi, b)),
                  pl.BlockSpec((1, nq, ncmp, cols), lambda gi, b: (gi, 0, 0, 0)),
                  pl.BlockSpec((1, N_SEL_BIAS, Q_TILE, cols), lambda gi, b: (gi, 0, 0, 0)),
                  pl.BlockSpec((1, 1, Q_TILE, cols), lambda gi, b: (gi, 0, 0, 0)),
                  _const_spec(mt.shape)],
        out_specs=pl.BlockSpec((1, seq, width), lambda gi, b: (b, 0, gi)),
        out_shape=jax.ShapeDtypeStruct((batch, seq, N_KV * width), f32),
        compiler_params=_params(("parallel", "arbitrary")),
        name="nsa",
    )(qt, kc, vct, ks, vst, kw, vwt, gtt, bc, bs, bw, mt)


def _out_proj_body(x_ref, yl_ref, yn_ref, gn_ref, w_ref, gp_ref, gm_ref, o_ref, a_ref):
    half = x_ref.shape[0] // 2
    rows = [slice(0, half), slice(half, 2 * half)]
    ys = []
    for r in rows:
        yn = _rms(yn_ref[r, :], gn_ref[...]).astype(bf16)
        ys.append(_dot(yl_ref[r, :], w_ref[0:LRU_WIDTH, :]) + _dot(yn, w_ref[LRU_WIDTH:, :]))
    for r, y in zip(rows, ys):
        h = x_ref[r, :] + _rms(y, gp_ref[...])
        o_ref[r, :] = h
        a_ref[r, :] = _rms(h, gm_ref[...]).astype(bf16)


def _out_proj(x2, yl, yn, gn, w, gp, gm, tm=512):
    m = x2.shape[0]
    row = pl.BlockSpec((tm, D_MODEL), lambda i: (i, 0))
    return pl.pallas_call(
        _out_proj_body,
        grid=(m // tm,),
        in_specs=[row,
                  pl.BlockSpec((tm, LRU_WIDTH), lambda i: (i, 0)),
                  pl.BlockSpec((tm, D_MODEL - LRU_WIDTH), lambda i: (i, 0)),
                  _const_spec((1, D_MODEL - LRU_WIDTH)),
                  _const_spec((D_MODEL, D_MODEL)),
                  _const_spec((1, D_MODEL)), _const_spec((1, D_MODEL))],
        out_specs=[row, row],
        out_shape=[jax.ShapeDtypeStruct((m, D_MODEL), f32), jax.ShapeDtypeStruct((m, D_MODEL), bf16)],
        compiler_params=_params(("parallel",)),
        name="out_proj",
    )(x2, yl, yn, gn, w, gp, gm)


def _mlp_body(a_ref, w1_ref, w2_ref, g2_ref, o_ref, acc_sc):
    j = pl.program_id(1)

    @pl.when(j == 0)
    def _():
        acc_sc[...] = jnp.zeros_like(acc_sc)

    hid = jnp.maximum(_dot(a_ref[...], w1_ref[...]), 0.0)
    acc_sc[...] += _dot((hid * hid).astype(bf16), w2_ref[...])

    @pl.when(j == pl.num_programs(1) - 1)
    def _():
        o_ref[...] = _rms(acc_sc[...], g2_ref[...])


def _mlp(a, w1, w2, g2, tm=1024, tf=1024):
    m = a.shape[0]
    return pl.pallas_call(
        _mlp_body,
        grid=(m // tm, D_FF // tf),
        in_specs=[pl.BlockSpec((tm, D_MODEL), lambda i, j: (i, 0)),
                  pl.BlockSpec((D_MODEL, tf), lambda i, j: (0, j)),
                  pl.BlockSpec((tf, D_MODEL), lambda i, j: (j, 0)),
                  _const_spec((1, D_MODEL))],
        out_specs=pl.BlockSpec((tm, D_MODEL), lambda i, j: (i, 0)),
        out_shape=jax.ShapeDtypeStruct((m, D_MODEL), f32),
        scratch_shapes=[pltpu.VMEM((tm, D_MODEL), f32)],
        compiler_params=_params(("parallel", "arbitrary")),
        name="mlp",
    )(a, w1, w2, g2)


def _ple_body(h_ref, f_ref, p_ref, wg_ref, wp_ref, o_ref):
    half = h_ref.shape[0] // 2
    rows = [slice(0, half), slice(half, 2 * half)]
    staged = []
    for r in rows:
        h = h_ref[r, :] + f_ref[r, :]
        staged.append((h, _dot(h.astype(bf16), wg_ref[...]), _dot(p_ref[r, :].astype(bf16), wp_ref[...])))
    for r, (h, gate, emb) in zip(rows, staged):
        o_ref[r, :] = h + jax.nn.sigmoid(gate) * emb


def _ple(h, f, p2, wg, wp, tm=512):
    m = h.shape[0]
    row = pl.BlockSpec((tm, D_MODEL), lambda i: (i, 0))
    return pl.pallas_call(
        _ple_body,
        grid=(m // tm,),
        in_specs=[row, row,
                  pl.BlockSpec((tm, PLE_DIM), lambda i: (i, 0)),
                  _const_spec((D_MODEL, D_MODEL)),
                  _const_spec((PLE_DIM, D_MODEL))],
        out_specs=row,
        out_shape=jax.ShapeDtypeStruct((m, D_MODEL), f32),
        compiler_params=_params(("parallel",)),
        name="ple",
    )(h, f, p2, wg, wp)


def _block_diag_chunks(w):
    per = GATE_CHUNK // LRU_BLOCK_DIM
    w = w.reshape(LRU_BLOCKS // per, per, LRU_BLOCK_DIM, LRU_BLOCK_DIM)
    eye = jnp.eye(per, dtype=w.dtype)
    return jnp.einsum('cpij,pq->cpiqj', w, eye).reshape(LRU_BLOCKS // per, GATE_CHUNK, GATE_CHUNK)


def _layer(h, p_i, i, prm, bias):
    batch, seq, _ = h.shape
    m = batch * seq
    x2 = h.reshape(m, D_MODEL)
    row = lambda v: v.reshape(1, -1)
    per_batch = lambda t: t.reshape(batch, seq, t.shape[-1])

    w_in = prm["w_in"][i].astype(bf16)
    cols = lambda name: w_in[:, IN_COLS[name][0]:IN_COLS[name][1]]
    w_tok = jnp.concatenate([cols(n) for n in TOK_STREAMS], axis=1)
    per = 3 * HEADS_PER_KV
    w_gate = jnp.pad(cols("gates").reshape(D_MODEL, N_KV, per), ((0, 0), (0, 0), (0, GATE_ROWS - per)))
    w_feat = jnp.concatenate([cols(n) for n in FEAT_STREAMS] + [w_gate.reshape(D_MODEL, N_KV * GATE_ROWS)],
                             axis=1).T
    u, gl, kc, vc, ks, kw, qt, vst, vwt, gtt = _in_proj(x2, row(prm["norm_mix_pre"][i]), w_tok, w_feat, seq)

    y_lru = _rglru(u, gl, prm["conv_w"][i], row(prm["conv_b"][i]),
                   _block_diag_chunks(prm["lru_wa"][i]).astype(bf16), row(prm["lru_ba"][i]),
                   _block_diag_chunks(prm["lru_wx"][i]).astype(bf16), row(prm["lru_bx"][i]),
                   row(prm["lru_lambda"][i]), row(prm["gnorm_lru"][i]), batch, seq)

    lane_pad = lambda w: jnp.pad(w, ((0, 0), (0, LANES - HEAD_DIM))).astype(bf16)
    kcc, vcct = _compress(per_batch(kc), per_batch(vc),
                          prm["cmp_pe_k"][i].reshape(1, -1), prm["cmp_w1_k"][i].astype(bf16),
                          lane_pad(prm["cmp_w2_k"][i]),
                          prm["cmp_pe_v"][i].reshape(1, -1), prm["cmp_w1_v"][i].astype(bf16),
                          lane_pad(prm["cmp_w2_v"][i]).T)

    bc, bs, bw = bias
    y_nsa = _nsa(qt, kcc, vcct, per_batch(ks), vst, per_batch(kw), vwt, gtt, bc, bs, bw,
                 jnp.asarray(_importance_matrix_t(seq)).astype(bf16))

    h1, a1 = _out_proj(x2, y_lru, y_nsa.reshape(m, -1), row(prm["gnorm_nsa"][i]),
                       prm["w_out"][i].astype(bf16), row(prm["norm_mix_post"][i]), row(prm["norm_mlp_pre"][i]))
    f = _mlp(a1, prm["mlp_w1"][i].astype(bf16), prm["mlp_w2"][i].astype(bf16), row(prm["norm_mlp_post"][i]))
    h3 = _ple(h1, f, p_i.reshape(m, PLE_DIM), prm["ple_gate"][i].astype(bf16), prm["ple_proj"][i].astype(bf16))
    return h3.reshape(batch, seq, D_MODEL)


def kernel(x, p, norm_mix_pre, norm_mix_post, norm_mlp_pre, norm_mlp_post, w_in, conv_w, conv_b, lru_wa, lru_ba, lru_wx, lru_bx, lru_lambda, cmp_pe_k, cmp_w1_k, cmp_w2_k, cmp_pe_v, cmp_w1_v, cmp_w2_v, rel_bias, gnorm_lru, gnorm_nsa, w_out, mlp_w1, mlp_w2, ple_gate, ple_proj):
    prm = dict(norm_mix_pre=norm_mix_pre, norm_mix_post=norm_mix_post, norm_mlp_pre=norm_mlp_pre,
               norm_mlp_post=norm_mlp_post, w_in=w_in, conv_w=conv_w, conv_b=conv_b, lru_wa=lru_wa,
               lru_ba=lru_ba, lru_wx=lru_wx, lru_bx=lru_bx, lru_lambda=lru_lambda, cmp_pe_k=cmp_pe_k,
               cmp_w1_k=cmp_w1_k, cmp_w2_k=cmp_w2_k, cmp_pe_v=cmp_pe_v, cmp_w1_v=cmp_w1_v,
               cmp_w2_v=cmp_w2_v, gnorm_lru=gnorm_lru, gnorm_nsa=gnorm_nsa, w_out=w_out, mlp_w1=mlp_w1,
               mlp_w2=mlp_w2, ple_gate=ple_gate, ple_proj=ple_proj)
    bias = _bias_tiles(rel_bias, x.shape[1])
    h = x
    for i in range(w_in.shape[0]):
        h = _layer(h, p[i], i, prm, bias)
    return h
```

```python
import functools
import math

import numpy as np
import jax
import jax.numpy as jnp
from jax import lax
from jax.experimental import pallas as pl
from jax.experimental.pallas import tpu as pltpu

D_MODEL = 2048
PLE_DIM = 256
LRU_WIDTH = 1024
LRU_BLOCKS = 16
LRU_BLOCK_DIM = 64
CONV_WIDTH = 4
LRU_C = 8.0
HEAD_DIM = 64
N_HEADS = 16
N_KV = 4
HEADS_PER_KV = 4
CMP_LEN = 32
CMP_STRIDE = 16
CMP_HIDDEN = 256
SEL_BLOCK = 64
N_SELECT = 16
N_LOCAL_FORCED = 2
WINDOW = 512
N_BUCKETS = 32
MAX_DISTANCE = 128
D_FF = 4 * D_MODEL
NORM_EPS = 1e-6
SQRT_GUARD = 1e-30
NEG_INF = -1e30
FORCE_SCORE = 1e4
KV_W = N_KV * HEAD_DIM


def _offsets(sizes):
    out, pos = {}, 0
    for name, width in sizes:
        out[name] = (pos, pos + width)
        pos += width
    return out, pos


IN_COLS, IN_DIM = _offsets([("u", LRU_WIDTH), ("gl", LRU_WIDTH), ("q", N_HEADS * HEAD_DIM), ("kc", KV_W),
                            ("vc", KV_W), ("ks", KV_W), ("vs", KV_W), ("kw", KV_W), ("vw", KV_W),
                            ("gates", 3 * N_HEADS)])
GATE_ROWS = 16
FEAT_STREAMS = ("q", "vs", "vw")
FEAT_ROWS, FEAT_END = _offsets([(n, IN_COLS[n][1] - IN_COLS[n][0]) for n in FEAT_STREAMS]
                               + [("gates", N_KV * GATE_ROWS)])

SUBLANES = 8
GATE_CHUNK = 256
LANES = 128
assert all(lo % LANES == 0 for lo, _ in IN_COLS.values())
MASK_LO = HEAD_DIM
Q_TILE = 128
N_SEL_BIAS = 2
V_ROWS = 80
SKEW = 2
LOG2E = math.log2(math.e)

VMEM_LIMIT = 56 * 1024 * 1024

f32 = jnp.float32
bf16 = jnp.bfloat16


def _rms(x, g):
    return x * lax.rsqrt(jnp.mean(x * x, axis=-1, keepdims=True) + NORM_EPS) * g


def _dot(a, b):
    return jnp.dot(a, b, preferred_element_type=f32)


def _dot_nt(a, b):
    return lax.dot_general(a, b, (((1,), (1,)), ((), ())), preferred_element_type=f32)


def _dot_tn(a, b):
    return lax.dot_general(a, b, (((0,), (0,)), ((), ())), preferred_element_type=f32)


def _const_spec(shape):
    nd = len(shape)
    return pl.BlockSpec(shape, lambda *_: (0,) * nd)


def _params(sem):
    return pltpu.CompilerParams(dimension_semantics=sem, vmem_limit_bytes=VMEM_LIMIT)


def _spread_groups(z, fill):
    lane = lax.broadcasted_iota(jnp.int32, (z.shape[0], LANES), 1)
    parts = []
    for g in range(N_KV):
        pair = z[:, (g // 2) * LANES:(g // 2 + 1) * LANES]
        if g % 2:
            pair = pltpu.roll(pair, HEAD_DIM, 1)
        parts.append(jnp.where(lane < HEAD_DIM, pair, fill))
    return jnp.concatenate(parts, axis=1)


def _value_rows(z):
    tm = z.shape[1]
    ones_row = jnp.where(lax.broadcasted_iota(jnp.int32, (LANES - HEAD_DIM, tm), 0) == 0, 1.0, 0.0)
    parts = []
    for g in range(N_KV):
        parts += [z[g * HEAD_DIM:(g + 1) * HEAD_DIM, :], ones_row]
    return jnp.concatenate(parts, axis=0)


def _in_proj_body(seq, x_ref, g_ref, w_ref, wt_ref, u_ref, gl_ref, kc_ref, vc_ref, ks_ref, kw_ref,
                  qt_ref, vst_ref, vwt_ref, gtt_ref):
    tm = x_ref.shape[0]
    a = _rms(x_ref[...], g_ref[...]).astype(bf16)

    def tok(name):
        lo, hi = IN_COLS[name]
        return _dot(a, w_ref[:, lo:hi])

    def feat(name):
        lo, hi = FEAT_ROWS[name]
        return _dot_nt(wt_ref[lo:hi, :], a)

    u_ref[...] = tok("u")
    gl_ref[...] = tok("gl")
    kc_ref[...] = tok("kc")
    vc_ref[...] = tok("vc")
    lane = lax.broadcasted_iota(jnp.int32, (tm, LANES), 1)
    pos = (pl.program_id(0) * tm) % seq + lax.broadcasted_iota(jnp.int32, (tm, LANES), 0)
    block_onehot = jnp.where(lane - MASK_LO == pos // SEL_BLOCK, 1.0, 0.0)
    ks_ref[...] = _spread_groups(tok("ks"), block_onehot).astype(bf16)
    kw_ref[...] = _spread_groups(tok("kw"), 0.0).astype(bf16)

    qt_ref[...] = (feat("q") * (LOG2E * HEAD_DIM ** -0.5)).astype(bf16)
    vst_ref[...] = _value_rows(feat("vs")).astype(bf16)
    vwt_ref[...] = _value_rows(feat("vw")).astype(bf16)
    gtt_ref[...] = feat("gates")


def _in_proj(x2, g, w, wt, seq, tm=512):
    m = x2.shape[0]
    wide = N_KV * LANES
    tok_out = [(LRU_WIDTH, f32), (LRU_WIDTH, f32), (KV_W, f32), (KV_W, f32), (wide, bf16), (wide, bf16)]
    feat_out = [(N_HEADS * HEAD_DIM, bf16), (wide, bf16), (wide, bf16), (N_KV * GATE_ROWS, f32)]
    body = lambda *refs: _in_proj_body(seq, *refs)
    return pl.pallas_call(
        body,
        grid=(m // tm,),
        in_specs=[pl.BlockSpec((tm, D_MODEL), lambda i: (i, 0)),
                  _const_spec((1, D_MODEL)),
                  pl.BlockSpec((D_MODEL, IN_DIM), lambda i: (0, 0), pipeline_mode=pl.Buffered(1)),
                  pl.BlockSpec((FEAT_END, D_MODEL), lambda i: (0, 0), pipeline_mode=pl.Buffered(1))],
        out_specs=([pl.BlockSpec((tm, n), lambda i: (i, 0)) for n, _ in tok_out]
                   + [pl.BlockSpec((n, tm), lambda i: (0, i)) for n, _ in feat_out]),
        out_shape=([jax.ShapeDtypeStruct((m, n), dt) for n, dt in tok_out]
                   + [jax.ShapeDtypeStruct((n, m), dt) for n, dt in feat_out]),
        compiler_params=_params(("parallel",)),
        name="in_proj",
    )(x2, g, w, wt)


def _rglru_body(u_ref, gl_ref, cw_ref, cb_ref, wa_ref, ba_ref, wx_ref, bx_ref, lam_ref, gn_ref,
                o_ref, ubuf, a_sc, b_sc, hc_sc):
    t = u_ref.shape[0]

    @pl.when(pl.program_id(1) == 0)
    def _():
        ubuf[...] = jnp.zeros_like(ubuf)
        hc_sc[...] = jnp.zeros_like(hc_sc)

    u = u_ref[...]
    u3 = u.reshape(t // SUBLANES, SUBLANES, LRU_WIDTH)
    tail = ubuf[...]
    cw = cw_ref[...]
    row8w = lax.broadcasted_iota(jnp.int32, (t // SUBLANES, SUBLANES, LRU_WIDTH), 1)
    xc3 = cb_ref[...] + u3 * cw[CONV_WIDTH - 1]
    for d in range(1, CONV_WIDTH):
        cur = pltpu.roll(u3, d, 1)
        prev = jnp.concatenate([pltpu.roll(tail, d, 0)[None], cur[:-1]], axis=0)
        xc3 = xc3 + jnp.where(row8w >= d, cur, prev) * cw[CONV_WIDTH - 1 - d]
    ubuf[...] = u[t - SUBLANES:t, :]
    xc = xc3.reshape(t, LRU_WIDTH)

    xb = xc.astype(bf16)
    sp = jax.nn.softplus(-lam_ref[...])
    row8 = lax.broadcasted_iota(jnp.int32, (t // SUBLANES, SUBLANES, GATE_CHUNK), 1)
    for c in range(LRU_WIDTH // GATE_CHUNK):
        sl = slice(c * GATE_CHUNK, (c + 1) * GATE_CHUNK)
        xcb = xb[:, sl]
        r = jax.nn.sigmoid(_dot(xcb, wa_ref[c]) + ba_ref[:, sl])
        ig = jax.nn.sigmoid(_dot(xcb, wx_ref[c]) + bx_ref[:, sl])
        log_a = (-LRU_C) * r * sp[:, sl]
        a = jnp.exp(log_a)
        y = 1.0 - a * a
        b = (y * lax.rsqrt(jnp.maximum(y, SQRT_GUARD))) * (ig * xc[:, sl])
        a = a.reshape(t // SUBLANES, SUBLANES, GATE_CHUNK)
        b = b.reshape(t // SUBLANES, SUBLANES, GATE_CHUNK)
        for d in (1, 2, 4):
            keep = row8 >= d
            a_prev = pltpu.roll(a, d, 1)
            b_prev = pltpu.roll(b, d, 1)
            b = jnp.where(keep, a * b_prev + b, b)
            a = jnp.where(keep, a * a_prev, a)
        a_sc[:, sl] = a.reshape(t, GATE_CHUNK)
        b_sc[:, sl] = b.reshape(t, GATE_CHUNK)

    def group(gi, h):
        rows = pl.ds(pl.multiple_of(gi * SUBLANES, SUBLANES), SUBLANES)
        hg = b_sc[rows, :] + a_sc[rows, :] * h
        b_sc[rows, :] = hg
        return hg[SUBLANES - 1:SUBLANES, :]

    h_last = lax.fori_loop(0, t // SUBLANES, group, hc_sc[0:1, :])
    hc_sc[0:1, :] = h_last
    y = b_sc[...] * jax.nn.gelu(gl_ref[...])
    o_ref[...] = _rms(y, gn_ref[...]).astype(o_ref.dtype)


def _rglru(u, gl, cw, cb, wa, ba, wx, bx, lam, gn, batch, seq, t=256):
    ns = seq // t
    row = pl.BlockSpec((t, LRU_WIDTH), lambda b, s: (b * ns + s, 0))
    vec = _const_spec((1, LRU_WIDTH))
    wspec = _const_spec((LRU_WIDTH // GATE_CHUNK, GATE_CHUNK, GATE_CHUNK))
    return pl.pallas_call(
        _rglru_body,
        grid=(batch, ns),
        in_specs=[row, row, _const_spec((CONV_WIDTH, LRU_WIDTH)), vec, wspec, vec, wspec, vec, vec, vec],
        out_specs=row,
        out_shape=jax.ShapeDtypeStruct((batch * seq, LRU_WIDTH), bf16),
        scratch_shapes=[pltpu.VMEM((SUBLANES, LRU_WIDTH), f32), pltpu.VMEM((t, LRU_WIDTH), f32),
                        pltpu.VMEM((t, LRU_WIDTH), f32), pltpu.VMEM((SUBLANES, LRU_WIDTH), f32)],
        compiler_params=_params(("parallel", "arbitrary")),
        name="rglru",
    )(u, gl, cw, cb, wa, ba, wx, bx, lam, gn)


def _compress_body(kc_ref, vc_ref, pek_ref, w1k_ref, w2k_ref, pev_ref, w1v_ref, w2v_ref, ko_ref, vo_ref):
    nrow = ko_ref.shape[2]
    rows = lax.broadcasted_iota(jnp.int32, (nrow, LANES), 0)
    cols = lax.broadcasted_iota(jnp.int32, (LANES, nrow), 1)
    lane64 = lax.broadcasted_iota(jnp.int32, (nrow, LANES), 1) < HEAD_DIM
    half = CMP_STRIDE * HEAD_DIM

    def one(t_ref, pe_ref, w1_ref, w2_ref, o_ref, transposed):
        pe = jnp.broadcast_to(pe_ref[...], (8, CMP_LEN * HEAD_DIM)).astype(bf16)
        c0 = _dot(pe, w1_ref[...])[0:1, :]
        toks = [t_ref[0, pl.ds(j, nrow, stride=CMP_STRIDE), :] for j in range(CMP_STRIDE)]
        for g in range(LANES // HEAD_DIM):
            blocks = []
            for i in range(CMP_STRIDE // 2):
                even, odd = toks[2 * i], toks[2 * i + 1]
                if g == 0:
                    blocks.append(jnp.where(lane64, even, pltpu.roll(odd, HEAD_DIM, 1)))
                else:
                    blocks.append(jnp.where(lane64, pltpu.roll(even, HEAD_DIM, 1), odd))
            chunk = jnp.concatenate(blocks, axis=1).astype(bf16)
            lo = _dot(chunk, w1_ref[0:half, :])
            hi = _dot(chunk, w1_ref[half:2 * half, :])
            hid = jax.nn.gelu(lo + pltpu.roll(hi, nrow - 1, 0) + c0)
            if transposed:
                out = _dot_nt(w2_ref[...], hid.astype(bf16))
                o_ref[0, g] = jnp.where(cols < nrow - 1, out, 0.0).astype(o_ref.dtype)
            else:
                out = _dot(hid.astype(bf16), w2_ref[...])
                o_ref[0, g] = jnp.where(rows < nrow - 1, out, 0.0).astype(o_ref.dtype)

    one(kc_ref, pek_ref, w1k_ref, w2k_ref, ko_ref, False)
    one(vc_ref, pev_ref, w1v_ref, w2v_ref, vo_ref, True)


def _compress(kc, vc, pek, w1k, w2k, pev, w1v, w2v):
    batch, seq, width = kc.shape
    nrow = seq // CMP_STRIDE
    pair = LANES // HEAD_DIM
    tspec = pl.BlockSpec((1, seq, LANES), lambda b, h: (b, 0, h))
    ospec = pl.BlockSpec((1, pair, nrow, LANES), lambda b, h: (b, h, 0, 0))
    wts = [_const_spec((1, CMP_LEN * HEAD_DIM)), _const_spec((CMP_LEN * HEAD_DIM, CMP_HIDDEN))]
    tspec_v = pl.BlockSpec((1, pair, LANES, nrow), lambda b, h: (b, h, 0, 0))
    return pl.pallas_call(
        _compress_body,
        grid=(batch, width // LANES),
        in_specs=([tspec, tspec] + wts + [_const_spec((CMP_HIDDEN, LANES))]
                  + wts + [_const_spec((LANES, CMP_HIDDEN))]),
        out_specs=[ospec, tspec_v],
        out_shape=[jax.ShapeDtypeStruct((batch, N_KV, nrow, LANES), bf16),
                   jax.ShapeDtypeStruct((batch, N_KV, LANES, nrow), bf16)],
        compiler_params=_params(("parallel", "parallel")),
        name="compress",
    )(kc, vc, pek, w1k, w2k, pev, w1v, w2v)


def _bucket_thresholds():
    n = np.arange(0, 4096)
    max_exact = N_BUCKETS // 2
    nf = np.maximum(n, 1).astype(np.float32)
    large = max_exact + (np.log(nf / np.float32(max_exact)) / np.float32(math.log(MAX_DISTANCE / max_exact))
                         * np.float32(N_BUCKETS - max_exact)).astype(np.int32)
    large = np.minimum(large, N_BUCKETS - 1)
    bucket = np.where(n < max_exact, n, large)
    assert np.all(np.diff(bucket) >= 0) and bucket[0] == 0 and bucket[-1] == N_BUCKETS - 1
    return [int(np.argmax(bucket >= k)) for k in range(N_BUCKETS)]


_BUCKET_THR = _bucket_thresholds()
assert _BUCKET_THR[-1] <= (N_SEL_BIAS - 1) * Q_TILE + 1


def _bias_of_dist(dist, ok, tab_ref, head):
    last = tab_ref[N_BUCKETS - 1, head]
    val = jnp.full(dist.shape, (tab_ref[0, head] - last) * LOG2E, f32)
    for k in range(1, N_BUCKETS - 1):
        val = jnp.where(dist >= _BUCKET_THR[k], (tab_ref[k, head] - last) * LOG2E, val)
    val = jnp.where(dist >= _BUCKET_THR[N_BUCKETS - 1], 0.0, val)
    return jnp.where(ok, val, NEG_INF)


def _bias_body(tab_ref, bc_ref, bs_ref, bw_ref):
    g = pl.program_id(0)
    nq, ncmp = bc_ref.shape[1], bc_ref.shape[2]
    per_tile = Q_TILE // CMP_STRIDE
    back = -(-(_BUCKET_THR[-1] + CMP_LEN - 1) // CMP_STRIDE)
    for r in range(HEADS_PER_KV):
        head = g * HEADS_PER_KV + r
        cols = slice(r * Q_TILE, (r + 1) * Q_TILE)
        for t in range(nq):
            lo = max((t * per_tile - back) // SUBLANES * SUBLANES, 0)
            hi = min(-(-((t + 1) * per_tile) // SUBLANES) * SUBLANES, ncmp)
            n = lo + lax.broadcasted_iota(jnp.int32, (hi - lo, Q_TILE), 0)
            i = lax.broadcasted_iota(jnp.int32, (hi - lo, Q_TILE), 1)
            dist = t * Q_TILE + i - (n * CMP_STRIDE + CMP_LEN - 1)
            if lo > 0:
                bc_ref[0, t, 0:lo, cols] = jnp.zeros((lo, Q_TILE), f32)
            bc_ref[0, t, lo:hi, cols] = _bias_of_dist(dist, (dist >= 0) & (n < ncmp - 1), tab_ref, head)
            if hi < ncmp:
                bc_ref[0, t, hi:ncmp, cols] = jnp.full((ncmp - hi, Q_TILE), NEG_INF, f32)

        j = lax.broadcasted_iota(jnp.int32, (Q_TILE, Q_TILE), 0)
        i = lax.broadcasted_iota(jnp.int32, (Q_TILE, Q_TILE), 1)
        for d in range(N_SEL_BIAS):
            dist = d * Q_TILE + i - j
            bs_ref[0, d, :, cols] = _bias_of_dist(dist, dist >= 0, tab_ref, head)
        dist = WINDOW + i - j
        bw_ref[0, 0, :, cols] = _bias_of_dist(dist, dist < WINDOW, tab_ref, head)


def _bias_tiles(rel_bias, seq):
    nq = seq // Q_TILE
    ncmp = seq // CMP_STRIDE
    rows = HEADS_PER_KV * Q_TILE
    return pl.pallas_call(
        _bias_body,
        grid=(N_KV,),
        in_specs=[pl.BlockSpec(memory_space=pltpu.SMEM)],
        out_specs=[pl.BlockSpec((1, nq, ncmp, rows), lambda g: (g, 0, 0, 0)),
                   pl.BlockSpec((1, N_SEL_BIAS, Q_TILE, rows), lambda g: (g, 0, 0, 0)),
                   pl.BlockSpec((1, 1, Q_TILE, rows), lambda g: (g, 0, 0, 0))],
        out_shape=[jax.ShapeDtypeStruct((N_KV, nq, ncmp, rows), f32),
                   jax.ShapeDtypeStruct((N_KV, N_SEL_BIAS, Q_TILE, rows), f32),
                   jax.ShapeDtypeStruct((N_KV, 1, Q_TILE, rows), f32)],
        compiler_params=_params(("parallel",)),
        name="bias_tiles",
    )(rel_bias)


def _importance_matrix_t(seq):
    n_cmp = (seq - CMP_LEN) // CMP_STRIDE + 1
    n_sel = seq // SEL_BLOCK
    ratio_sel = SEL_BLOCK // CMP_STRIDE
    ratio_cmp = CMP_LEN // CMP_STRIDE
    jj = np.arange(n_sel)[:, None, None]
    ci = ratio_sel * jj + np.arange(ratio_sel)[None, :, None] - np.arange(ratio_cmp)[None, None, :]
    jb = np.broadcast_to(jj, ci.shape)
    ok = (ci >= 0) & (ci < n_cmp)
    m = np.zeros((n_sel, seq // CMP_STRIDE), np.float32)
    np.add.at(m, (jb[ok], ci[ok]), 1.0)
    return m


def _scores(qt, k_ref, t, first_tile, biases):
    s = _dot(k_ref[0, first_tile * Q_TILE:(t + 1) * Q_TILE, :], qt)
    pieces = []
    for kt in range(first_tile, t + 1):
        piece = s[(kt - first_tile) * Q_TILE:(kt - first_tile + 1) * Q_TILE, :]
        if t - kt in biases:
            piece = piece + biases[t - kt]
        pieces.append(piece)
    return pieces, jnp.max(functools.reduce(jnp.maximum, pieces), axis=0, keepdims=True)


def _weighted_values(pieces, m, vt_ref, t, first_tile):
    p = jnp.concatenate([jnp.exp2(pc - m).astype(bf16) for pc in pieces], axis=0)
    acc = _dot(vt_ref[0:V_ROWS, first_tile * Q_TILE:(t + 1) * Q_TILE], p)
    return acc[0:HEAD_DIM, :], 1.0 / acc[MASK_LO:MASK_LO + 1, :]


def _nsa_body(qt_ref, kc_ref, vct_ref, ks_ref, vst_ref, kw_ref, vwt_ref, gtt_ref, bc_ref, bs_ref, bw_ref,
              mt_ref, o_ref):
    n_sel = mt_ref.shape[0]
    n_tiles = qt_ref.shape[1] // Q_TILE
    cols = HEADS_PER_KV * Q_TILE
    head_cols = [slice(r * Q_TILE, (r + 1) * Q_TILE) for r in range(HEADS_PER_KV)]
    kc = kc_ref[0, 0]
    vct = vct_ref[0, 0]
    mt = mt_ref[...]
    jj = lax.broadcasted_iota(jnp.int32, (n_sel, Q_TILE), 0)
    qcol = lax.broadcasted_iota(jnp.int32, (n_sel, Q_TILE), 1)
    zeros_tail = jnp.zeros((LANES - HEAD_DIM, Q_TILE), bf16)
    zeros_rest = jnp.zeros((LANES - HEAD_DIM - n_sel, Q_TILE), bf16)

    def compressed_scores(t):
        tok = slice(t * Q_TILE, (t + 1) * Q_TILE)
        heads = [qt_ref[r * HEAD_DIM:(r + 1) * HEAD_DIM, tok] for r in range(HEADS_PER_KV)]
        q_plain = jnp.concatenate([jnp.concatenate([hd, zeros_tail], axis=0) for hd in heads], axis=1)
        return heads, q_plain, _dot(kc, q_plain) + bc_ref[0, t]

    def select_stage(t, heads, q_plain, s):
        p = jnp.exp2(s - jnp.max(s, axis=0, keepdims=True))
        norm = 1.0 / jnp.sum(p, axis=0, keepdims=True)
        if (t + 1) * Q_TILE > CMP_LEN - 1 >= t * Q_TILE:
            pos = t * Q_TILE + (lax.broadcasted_iota(jnp.int32, (1, cols), 1) & (Q_TILE - 1))
            norm = jnp.where(pos >= CMP_LEN - 1, norm, 0.0)
        p = p * norm
        o_c = _dot(vct[0:HEAD_DIM, :], p.astype(bf16))

        n_causal = ((t + 1) * Q_TILE - 1) // SEL_BLOCK + 1
        if n_causal <= N_SELECT:
            return q_plain, q_plain, o_c

        psum = p[:, head_cols[0]] + p[:, head_cols[1]] + p[:, head_cols[2]] + p[:, head_cols[3]]
        p_hi = psum.astype(bf16)
        p_lo = (psum - p_hi.astype(f32)).astype(bf16)
        imp = _dot(mt, p_hi) + _dot(mt, p_lo)
        dblk = (t * Q_TILE + qcol) // SEL_BLOCK - jj
        forced = (jj == 0) | ((dblk >= 0) & (dblk < N_LOCAL_FORCED))
        imp = jnp.where(forced, FORCE_SCORE, jnp.where(dblk >= 0, imp, -FORCE_SCORE))
        rank = jnp.zeros((n_sel, Q_TILE), f32)
        for i in range(n_causal):
            row = imp[i:i + 1, :]
            ahead = (row > imp) | ((row == imp) & (jj > i))
            rank = rank + jnp.where(ahead, 1.0, 0.0)
        mask_rows = jnp.where(rank < N_SELECT, 0.0, NEG_INF).astype(bf16)
        q_masked = jnp.concatenate(
            [jnp.concatenate([hd, mask_rows, zeros_rest], axis=0) for hd in heads], axis=1)
        return q_plain, q_masked, o_c

    first_win = lambda t: max(t - WINDOW // Q_TILE, 0)

    def sel_scores(t, q_masked):
        return _scores(q_masked, ks_ref, t, 0, {d: bs_ref[0, d] for d in range(N_SEL_BIAS)})

    def win_scores(t, q_plain):
        band = {d: bs_ref[0, d] for d in range(N_SEL_BIAS)}
        band[WINDOW // Q_TILE] = bw_ref[0, 0]
        return _scores(q_plain, kw_ref, t, first_win(t), band)

    def finish(t, o_c, sel_out, win_out):
        tok = slice(t * Q_TILE, (t + 1) * Q_TILE)
        acc_s, inv_s = sel_out
        acc_w, inv_w = win_out
        gt = jax.nn.sigmoid(gtt_ref[:, tok])
        gate = lambda br: jnp.concatenate([gt[3 * r + br:3 * r + br + 1, :] for r in range(HEADS_PER_KV)], axis=1)
        out_t = gate(0) * o_c + (gate(1) * inv_s) * acc_s + (gate(2) * inv_w) * acc_w
        pairs = [jnp.concatenate([out_t[:, head_cols[2 * h]], out_t[:, head_cols[2 * h + 1]]], axis=0).T
                 for h in range(HEADS_PER_KV // 2)]
        o_ref[0, tok, :] = jnp.concatenate(pairs, axis=-1)

    compressed, queries, o_cs, s_sel, s_win = {}, {}, {}, {}, {}

    def emit_compressed(t):
        if t < n_tiles:
            compressed[t] = compressed_scores(t)

    def emit_select(t):
        if t < n_tiles:
            q_plain, q_masked, o_cs[t] = select_stage(t, *compressed.pop(t))
            queries[t] = (q_plain, q_masked)

    def emit_scores(t):
        if t < n_tiles:
            q_plain, q_masked = queries.pop(t)
            s_sel[t] = sel_scores(t, q_masked)
            s_win[t] = win_scores(t, q_plain)

    for t in range(SKEW + 2):
        emit_compressed(t)
    for t in range(SKEW + 1):
        emit_select(t)
    for t in range(SKEW):
        emit_scores(t)
    for t in range(n_tiles):
        emit_compressed(t + SKEW + 2)
        emit_scores(t + SKEW)
        emit_select(t + SKEW + 1)
        sel_out = _weighted_values(*s_sel.pop(t), vst_ref, t, 0)
        win_out = _weighted_values(*s_win.pop(t), vwt_ref, t, first_win(t))
        finish(t, o_cs.pop(t), sel_out, win_out)


def _nsa(qt, kc, vct, ks, vst, kw, vwt, gtt, bc, bs, bw, mt):
    batch, seq, _ = ks.shape
    nq = seq // Q_TILE
    cols = HEADS_PER_KV * Q_TILE
    ncmp = kc.shape[2]
    width = HEADS_PER_KV * HEAD_DIM
    k_spec = pl.BlockSpec((1, seq, LANES), lambda gi, b: (b, 0, gi))
    vt_spec = pl.BlockSpec((LANES, seq), lambda gi, b: (gi, b))
    return pl.pallas_call(
        _nsa_body,
        grid=(N_KV, batch),
        in_specs=[pl.BlockSpec((width, seq), lambda gi, b: (gi, b)),
                  pl.BlockSpec((1, 1, ncmp, LANES), lambda gi, b: (b, gi, 0, 0)),
                  pl.BlockSpec((1, 1, LANES, ncmp), lambda gi, b: (b, gi, 0, 0)),
                  k_spec, vt_spec, k_spec, vt_spec,
                  pl.BlockSpec((GATE_ROWS, seq), lambda gi, b: (gi, b)),
                  pl.BlockSpec((1, nq, ncmp, cols), lambda gi, b: (gi, 0, 0, 0)),
                  pl.BlockSpec((1, N_SEL_BIAS, Q_TILE, cols), lambda gi, b: (gi, 0, 0, 0)),
                  pl.BlockSpec((1, 1, Q_TILE, cols), lambda gi, b: (gi, 0, 0, 0)),
                  _const_spec(mt.shape)],
        out_specs=pl.BlockSpec((1, seq, width), lambda gi, b: (b, 0, gi)),
        out_shape=jax.ShapeDtypeStruct((batch, seq, N_KV * width), f32),
        compiler_params=_params(("parallel", "arbitrary")),
        name="nsa",
    )(qt, kc, vct, ks, vst, kw, vwt, gtt, bc, bs, bw, mt)


def _out_proj_body(x_ref, yl_ref, yn_ref, gn_ref, w_ref, gp_ref, gm_ref, o_ref, a_ref):
    half = x_ref.shape[0] // 2
    rows = [slice(0, half), slice(half, 2 * half)]
    ys = []
    for r in rows:
        yn = _rms(yn_ref[r, :], gn_ref[...]).astype(bf16)
        ys.append(_dot(yl_ref[r, :], w_ref[0:LRU_WIDTH, :]) + _dot(yn, w_ref[LRU_WIDTH:, :]))
    for r, y in zip(rows, ys):
        h = x_ref[r, :] + _rms(y, gp_ref[...])
        o_ref[r, :] = h
        a_ref[r, :] = _rms(h, gm_ref[...]).astype(bf16)


def _out_proj(x2, yl, yn, gn, w, gp, gm, tm=512):
    m = x2.shape[0]
    row = pl.BlockSpec((tm, D_MODEL), lambda i: (i, 0))
    return pl.pallas_call(
        _out_proj_body,
        grid=(m // tm,),
        in_specs=[row,
                  pl.BlockSpec((tm, LRU_WIDTH), lambda i: (i, 0)),
                  pl.BlockSpec((tm, D_MODEL - LRU_WIDTH), lambda i: (i, 0)),
                  _const_spec((1, D_MODEL - LRU_WIDTH)),
                  _const_spec((D_MODEL, D_MODEL)),
                  _const_spec((1, D_MODEL)), _const_spec((1, D_MODEL))],
        out_specs=[row, row],
        out_shape=[jax.ShapeDtypeStruct((m, D_MODEL), f32), jax.ShapeDtypeStruct((m, D_MODEL), bf16)],
        compiler_params=_params(("parallel",)),
        name="out_proj",
    )(x2, yl, yn, gn, w, gp, gm)


def _mlp_body(a_ref, w1_ref, w2_ref, g2_ref, o_ref, acc_sc):
    j = pl.program_id(1)

    @pl.when(j == 0)
    def _():
        acc_sc[...] = jnp.zeros_like(acc_sc)

    hid = jnp.maximum(_dot(a_ref[...], w1_ref[...]), 0.0)
    acc_sc[...] += _dot((hid * hid).astype(bf16), w2_ref[...])

    @pl.when(j == pl.num_programs(1) - 1)
    def _():
        o_ref[...] = _rms(acc_sc[...], g2_ref[...])


def _mlp(a, w1, w2, g2, tm=1024, tf=1024):
    m = a.shape[0]
    return pl.pallas_call(
        _mlp_body,
        grid=(m // tm, D_FF // tf),
        in_specs=[pl.BlockSpec((tm, D_MODEL), lambda i, j: (i, 0)),
                  pl.BlockSpec((D_MODEL, tf), lambda i, j: (0, j)),
                  pl.BlockSpec((tf, D_MODEL), lambda i, j: (j, 0)),
                  _const_spec((1, D_MODEL))],
        out_specs=pl.BlockSpec((tm, D_MODEL), lambda i, j: (i, 0)),
        out_shape=jax.ShapeDtypeStruct((m, D_MODEL), f32),
        scratch_shapes=[pltpu.VMEM((tm, D_MODEL), f32)],
        compiler_params=_params(("parallel", "arbitrary")),
        name="mlp",
    )(a, w1, w2, g2)


def _ple_body(h_ref, f_ref, p_ref, wg_ref, wp_ref, o_ref):
    half = h_ref.shape[0] // 2
    rows = [slice(0, half), slice(half, 2 * half)]
    staged = []
    for r in rows:
        h = h_ref[r, :] + f_ref[r, :]
        staged.append((h, _dot(h.astype(bf16), wg_ref[...]), _dot(p_ref[r, :].astype(bf16), wp_ref[...])))
    for r, (h, gate, emb) in zip(rows, staged):
        o_ref[r, :] = h + jax.nn.sigmoid(gate) * emb


def _ple(h, f, p2, wg, wp, tm=512):
    m = h.shape[0]
    row = pl.BlockSpec((tm, D_MODEL), lambda i: (i, 0))
    return pl.pallas_call(
        _ple_body,
        grid=(m // tm,),
        in_specs=[row, row,
                  pl.BlockSpec((tm, PLE_DIM), lambda i: (i, 0)),
                  _const_spec((D_MODEL, D_MODEL)),
                  _const_spec((PLE_DIM, D_MODEL))],
        out_specs=row,
        out_shape=jax.ShapeDtypeStruct((m, D_MODEL), f32),
        compiler_params=_params(("parallel",)),
        name="ple",
    )(h, f, p2, wg, wp)


def _block_diag_chunks(w):
    per = GATE_CHUNK // LRU_BLOCK_DIM
    w = w.reshape(LRU_BLOCKS // per, per, LRU_BLOCK_DIM, LRU_BLOCK_DIM)
    eye = jnp.eye(per, dtype=w.dtype)
    return jnp.einsum('cpij,pq->cpiqj', w, eye).reshape(LRU_BLOCKS // per, GATE_CHUNK, GATE_CHUNK)


def _layer(h, p_i, i, prm, bias):
    batch, seq, _ = h.shape
    m = batch * seq
    x2 = h.reshape(m, D_MODEL)
    row = lambda v: v.reshape(1, -1)
    per_batch = lambda t: t.reshape(batch, seq, t.shape[-1])

    w_in = prm["w_in"][i].astype(bf16)
    cols = lambda name: w_in[:, IN_COLS[name][0]:IN_COLS[name][1]]
    per = 3 * HEADS_PER_KV
    w_gate = jnp.pad(cols("gates").reshape(D_MODEL, N_KV, per), ((0, 0), (0, 0), (0, GATE_ROWS - per)))
    w_feat = jnp.concatenate([cols(n) for n in FEAT_STREAMS] + [w_gate.reshape(D_MODEL, N_KV * GATE_ROWS)],
                             axis=1).T
    u, gl, kc, vc, ks, kw, qt, vst, vwt, gtt = _in_proj(x2, row(prm["norm_mix_pre"][i]), w_in, w_feat, seq)

    y_lru = _rglru(u, gl, prm["conv_w"][i], row(prm["conv_b"][i]),
                   _block_diag_chunks(prm["lru_wa"][i]).astype(bf16), row(prm["lru_ba"][i]),
                   _block_diag_chunks(prm["lru_wx"][i]).astype(bf16), row(prm["lru_bx"][i]),
                   row(prm["lru_lambda"][i]), row(prm["gnorm_lru"][i]), batch, seq)

    lane_pad = lambda w: jnp.pad(w, ((0, 0), (0, LANES - HEAD_DIM))).astype(bf16)
    kcc, vcct = _compress(per_batch(kc), per_batch(vc),
                          prm["cmp_pe_k"][i].reshape(1, -1), prm["cmp_w1_k"][i].astype(bf16),
                          lane_pad(prm["cmp_w2_k"][i]),
                          prm["cmp_pe_v"][i].reshape(1, -1), prm["cmp_w1_v"][i].astype(bf16),
                          lane_pad(prm["cmp_w2_v"][i]).T)

    bc, bs, bw = bias
    y_nsa = _nsa(qt, kcc, vcct, per_batch(ks), vst, per_batch(kw), vwt, gtt, bc, bs, bw,
                 jnp.asarray(_importance_matrix_t(seq)).astype(bf16))

    h1, a1 = _out_proj(x2, y_lru, y_nsa.reshape(m, -1), row(prm["gnorm_nsa"][i]),
                       prm["w_out"][i].astype(bf16), row(prm["norm_mix_post"][i]), row(prm["norm_mlp_pre"][i]))
    f = _mlp(a1, prm["mlp_w1"][i].astype(bf16), prm["mlp_w2"][i].astype(bf16), row(prm["norm_mlp_post"][i]))
    h3 = _ple(h1, f, p_i.reshape(m, PLE_DIM), prm["ple_gate"][i].astype(bf16), prm["ple_proj"][i].astype(bf16))
    return h3.reshape(batch, seq, D_MODEL)


def kernel(x, p, norm_mix_pre, norm_mix_post, norm_mlp_pre, norm_mlp_post, w_in, conv_w, conv_b, lru_wa, lru_ba, lru_wx, lru_bx, lru_lambda, cmp_pe_k, cmp_w1_k, cmp_w2_k, cmp_pe_v, cmp_w1_v, cmp_w2_v, rel_bias, gnorm_lru, gnorm_nsa, w_out, mlp_w1, mlp_w2, ple_gate, ple_proj):
    prm = dict(norm_mix_pre=norm_mix_pre, norm_mix_post=norm_mix_post, norm_mlp_pre=norm_mlp_pre,
               norm_mlp_post=norm_mlp_post, w_in=w_in, conv_w=conv_w, conv_b=conv_b, lru_wa=lru_wa,
               lru_ba=lru_ba, lru_wx=lru_wx, lru_bx=lru_bx, lru_lambda=lru_lambda, cmp_pe_k=cmp_pe_k,
               cmp_w1_k=cmp_w1_k, cmp_w2_k=cmp_w2_k, cmp_pe_v=cmp_pe_v, cmp_w1_v=cmp_w1_v,
               cmp_w2_v=cmp_w2_v, gnorm_lru=gnorm_lru, gnorm_nsa=gnorm_nsa, w_out=w_out, mlp_w1=mlp_w1,
               mlp_w2=mlp_w2, ple_gate=ple_gate, ple_proj=ple_proj)
    bias = _bias_tiles(rel_bias, x.shape[1])
    h = x
    for i in range(w_in.shape[0]):
        h = _layer(h, p[i], i, prm, bias)
    return h
```

```python
import functools
import math

import numpy as np
import jax
import jax.numpy as jnp
from jax import lax
from jax.experimental import pallas as pl
from jax.experimental.pallas import tpu as pltpu

D_MODEL = 2048
PLE_DIM = 256
LRU_WIDTH = 1024
LRU_BLOCKS = 16
LRU_BLOCK_DIM = 64
CONV_WIDTH = 4
LRU_C = 8.0
HEAD_DIM = 64
N_HEADS = 16
N_KV = 4
HEADS_PER_KV = 4
CMP_LEN = 32
CMP_STRIDE = 16
CMP_HIDDEN = 256
SEL_BLOCK = 64
N_SELECT = 16
N_LOCAL_FORCED = 2
WINDOW = 512
N_BUCKETS = 32
MAX_DISTANCE = 128
D_FF = 4 * D_MODEL
NORM_EPS = 1e-6
SQRT_GUARD = 1e-30
NEG_INF = -1e30
FORCE_SCORE = 1e4
KV_W = N_KV * HEAD_DIM


def _offsets(sizes):
    out, pos = {}, 0
    for name, width in sizes:
        out[name] = (pos, pos + width)
        pos += width
    return out, pos


IN_COLS, IN_DIM = _offsets([("u", LRU_WIDTH), ("gl", LRU_WIDTH), ("q", N_HEADS * HEAD_DIM), ("kc", KV_W),
                            ("vc", KV_W), ("ks", KV_W), ("vs", KV_W), ("kw", KV_W), ("vw", KV_W),
                            ("gates", 3 * N_HEADS)])
GATE_ROWS = 16
FEAT_STREAMS = ("q", "vs", "vw")
FEAT_ROWS, FEAT_END = _offsets([(n, IN_COLS[n][1] - IN_COLS[n][0]) for n in FEAT_STREAMS]
                               + [("gates", N_KV * GATE_ROWS)])

SUBLANES = 8
GATE_CHUNK = 256
LANES = 128
assert all(lo % LANES == 0 for lo, _ in IN_COLS.values())
MASK_LO = HEAD_DIM
Q_TILE = 128
N_SEL_BIAS = 2
V_ROWS = 80
SKEW = 2
LOG2E = math.log2(math.e)

VMEM_LIMIT = 56 * 1024 * 1024

f32 = jnp.float32
bf16 = jnp.bfloat16


def _rms(x, g):
    return x * lax.rsqrt(jnp.mean(x * x, axis=-1, keepdims=True) + NORM_EPS) * g


def _dot(a, b):
    return jnp.dot(a, b, preferred_element_type=f32)


def _dot_nt(a, b):
    return lax.dot_general(a, b, (((1,), (1,)), ((), ())), preferred_element_type=f32)


def _const_spec(shape):
    nd = len(shape)
    return pl.BlockSpec(shape, lambda *_: (0,) * nd)


def _params(sem):
    return pltpu.CompilerParams(dimension_semantics=sem, vmem_limit_bytes=VMEM_LIMIT)


def _spread_groups(z, fill):
    lane = lax.broadcasted_iota(jnp.int32, (z.shape[0], LANES), 1)
    parts = []
    for g in range(N_KV):
        pair = z[:, (g // 2) * LANES:(g // 2 + 1) * LANES]
        if g % 2:
            pair = pltpu.roll(pair, HEAD_DIM, 1)
        parts.append(jnp.where(lane < HEAD_DIM, pair, fill))
    return jnp.concatenate(parts, axis=1)


def _value_rows(z):
    tm = z.shape[1]
    ones_row = jnp.where(lax.broadcasted_iota(jnp.int32, (LANES - HEAD_DIM, tm), 0) == 0, 1.0, 0.0)
    parts = []
    for g in range(N_KV):
        parts += [z[g * HEAD_DIM:(g + 1) * HEAD_DIM, :], ones_row]
    return jnp.concatenate(parts, axis=0)


def _in_proj_body(seq, x_ref, g_ref, w_ref, wt_ref, u_ref, gl_ref, kc_ref, vc_ref, ks_ref, kw_ref,
                  qt_ref, vst_ref, vwt_ref, gtt_ref):
    tm = x_ref.shape[0]
    a = _rms(x_ref[...], g_ref[...]).astype(bf16)

    def tok(name):
        lo, hi = IN_COLS[name]
        return _dot(a, w_ref[:, lo:hi])

    def feat(name):
        lo, hi = FEAT_ROWS[name]
        return _dot_nt(wt_ref[lo:hi, :], a)

    u_ref[...] = tok("u")
    gl_ref[...] = tok("gl")
    kc_ref[...] = tok("kc")
    vc_ref[...] = tok("vc")
    lane = lax.broadcasted_iota(jnp.int32, (tm, LANES), 1)
    pos = (pl.program_id(0) * tm) % seq + lax.broadcasted_iota(jnp.int32, (tm, LANES), 0)
    block_onehot = jnp.where(lane - MASK_LO == pos // SEL_BLOCK, 1.0, 0.0)
    ks_ref[...] = _spread_groups(tok("ks"), block_onehot).astype(bf16)
    kw_ref[...] = _spread_groups(tok("kw"), 0.0).astype(bf16)

    qt_ref[...] = (feat("q") * (LOG2E * HEAD_DIM ** -0.5)).astype(bf16)
    vst_ref[...] = _value_rows(feat("vs")).astype(bf16)
    vwt_ref[...] = _value_rows(feat("vw")).astype(bf16)
    gtt_ref[...] = feat("gates")


def _in_proj(x2, g, w, wt, seq, tm=512):
    m = x2.shape[0]
    wide = N_KV * LANES
    tok_out = [(LRU_WIDTH, f32), (LRU_WIDTH, f32), (KV_W, f32), (KV_W, f32), (wide, bf16), (wide, bf16)]
    feat_out = [(N_HEADS * HEAD_DIM, bf16), (wide, bf16), (wide, bf16), (N_KV * GATE_ROWS, f32)]
    body = lambda *refs: _in_proj_body(seq, *refs)
    return pl.pallas_call(
        body,
        grid=(m // tm,),
        in_specs=[pl.BlockSpec((tm, D_MODEL), lambda i: (i, 0)),
                  _const_spec((1, D_MODEL)),
                  pl.BlockSpec((D_MODEL, IN_DIM), lambda i: (0, 0), pipeline_mode=pl.Buffered(1)),
                  pl.BlockSpec((FEAT_END, D_MODEL), lambda i: (0, 0), pipeline_mode=pl.Buffered(1))],
        out_specs=([pl.BlockSpec((tm, n), lambda i: (i, 0)) for n, _ in tok_out]
                   + [pl.BlockSpec((n, tm), lambda i: (0, i)) for n, _ in feat_out]),
        out_shape=([jax.ShapeDtypeStruct((m, n), dt) for n, dt in tok_out]
                   + [jax.ShapeDtypeStruct((n, m), dt) for n, dt in feat_out]),
        compiler_params=_params(("parallel",)),
        name="in_proj",
    )(x2, g, w, wt)


def _rglru_body(u_ref, gl_ref, cw_ref, cb_ref, wa_ref, ba_ref, wx_ref, bx_ref, lam_ref, gn_ref,
                o_ref, ubuf, a_sc, b_sc, hc_sc):
    t = u_ref.shape[0]

    @pl.when(pl.program_id(1) == 0)
    def _():
        ubuf[...] = jnp.zeros_like(ubuf)
        hc_sc[...] = jnp.zeros_like(hc_sc)

    u = u_ref[...]
    u3 = u.reshape(t // SUBLANES, SUBLANES, LRU_WIDTH)
    tail = ubuf[...]
    cw = cw_ref[...]
    row8w = lax.broadcasted_iota(jnp.int32, (t // SUBLANES, SUBLANES, LRU_WIDTH), 1)
    xc3 = cb_ref[...] + u3 * cw[CONV_WIDTH - 1]
    for d in range(1, CONV_WIDTH):
        cur = pltpu.roll(u3, d, 1)
        prev = jnp.concatenate([pltpu.roll(tail, d, 0)[None], cur[:-1]], axis=0)
        xc3 = xc3 + jnp.where(row8w >= d, cur, prev) * cw[CONV_WIDTH - 1 - d]
    ubuf[...] = u[t - SUBLANES:t, :]
    xc = xc3.reshape(t, LRU_WIDTH)

    xb = xc.astype(bf16)
    sp = jax.nn.softplus(-lam_ref[...])
    row8 = lax.broadcasted_iota(jnp.int32, (t // SUBLANES, SUBLANES, GATE_CHUNK), 1)
    for c in range(LRU_WIDTH // GATE_CHUNK):
        sl = slice(c * GATE_CHUNK, (c + 1) * GATE_CHUNK)
        xcb = xb[:, sl]
        r = jax.nn.sigmoid(_dot(xcb, wa_ref[c]) + ba_ref[:, sl])
        ig = jax.nn.sigmoid(_dot(xcb, wx_ref[c]) + bx_ref[:, sl])
        log_a = (-LRU_C) * r * sp[:, sl]
        a = jnp.exp(log_a)
        y = 1.0 - a * a
        b = (y * lax.rsqrt(jnp.maximum(y, SQRT_GUARD))) * (ig * xc[:, sl])
        a = a.reshape(t // SUBLANES, SUBLANES, GATE_CHUNK)
        b = b.reshape(t // SUBLANES, SUBLANES, GATE_CHUNK)
        for d in (1, 2, 4):
            keep = row8 >= d
            a_prev = pltpu.roll(a, d, 1)
            b_prev = pltpu.roll(b, d, 1)
            b = jnp.where(keep, a * b_prev + b, b)
            a = jnp.where(keep, a * a_prev, a)
        a_sc[:, sl] = a.reshape(t, GATE_CHUNK)
        b_sc[:, sl] = b.reshape(t, GATE_CHUNK)

    def group(gi, h):
        rows = pl.ds(pl.multiple_of(gi * SUBLANES, SUBLANES), SUBLANES)
        hg = b_sc[rows, :] + a_sc[rows, :] * h
        b_sc[rows, :] = hg
        return hg[SUBLANES - 1:SUBLANES, :]

    h_last = lax.fori_loop(0, t // SUBLANES, group, hc_sc[0:1, :])
    hc_sc[0:1, :] = h_last
    y = b_sc[...] * jax.nn.gelu(gl_ref[...])
    o_ref[...] = _rms(y, gn_ref[...]).astype(o_ref.dtype)


def _rglru(u, gl, cw, cb, wa, ba, wx, bx, lam, gn, batch, seq, t=256):
    ns = seq // t
    row = pl.BlockSpec((t, LRU_WIDTH), lambda b, s: (b * ns + s, 0))
    vec = _const_spec((1, LRU_WIDTH))
    wspec = _const_spec((LRU_WIDTH // GATE_CHUNK, GATE_CHUNK, GATE_CHUNK))
    return pl.pallas_call(
        _rglru_body,
        grid=(batch, ns),
        in_specs=[row, row, _const_spec((CONV_WIDTH, LRU_WIDTH)), vec, wspec, vec, wspec, vec, vec, vec],
        out_specs=row,
        out_shape=jax.ShapeDtypeStruct((batch * seq, LRU_WIDTH), bf16),
        scratch_shapes=[pltpu.VMEM((SUBLANES, LRU_WIDTH), f32), pltpu.VMEM((t, LRU_WIDTH), f32),
                        pltpu.VMEM((t, LRU_WIDTH), f32), pltpu.VMEM((SUBLANES, LRU_WIDTH), f32)],
        compiler_params=_params(("parallel", "arbitrary")),
        name="rglru",
    )(u, gl, cw, cb, wa, ba, wx, bx, lam, gn)


def _compress_body(kc_ref, vc_ref, pek_ref, w1k_ref, w2k_ref, pev_ref, w1v_ref, w2v_ref, ko_ref, vo_ref):
    nrow = ko_ref.shape[2]
    rows = lax.broadcasted_iota(jnp.int32, (nrow, LANES), 0)
    cols = lax.broadcasted_iota(jnp.int32, (LANES, nrow), 1)
    lane64 = lax.broadcasted_iota(jnp.int32, (nrow, LANES), 1) < HEAD_DIM
    half = CMP_STRIDE * HEAD_DIM

    def one(t_ref, pe_ref, w1_ref, w2_ref, o_ref, transposed):
        pe = jnp.broadcast_to(pe_ref[...], (8, CMP_LEN * HEAD_DIM)).astype(bf16)
        c0 = _dot(pe, w1_ref[...])[0:1, :]
        toks = [t_ref[0, pl.ds(j, nrow, stride=CMP_STRIDE), :] for j in range(CMP_STRIDE)]
        for g in range(LANES // HEAD_DIM):
            blocks = []
            for i in range(CMP_STRIDE // 2):
                even, odd = toks[2 * i], toks[2 * i + 1]
                if g == 0:
                    blocks.append(jnp.where(lane64, even, pltpu.roll(odd, HEAD_DIM, 1)))
                else:
                    blocks.append(jnp.where(lane64, pltpu.roll(even, HEAD_DIM, 1), odd))
            chunk = jnp.concatenate(blocks, axis=1).astype(bf16)
            lo = _dot(chunk, w1_ref[0:half, :])
            hi = _dot(chunk, w1_ref[half:2 * half, :])
            hid = jax.nn.gelu(lo + pltpu.roll(hi, nrow - 1, 0) + c0)
            if transposed:
                out = _dot_nt(w2_ref[...], hid.astype(bf16))
                o_ref[0, g] = jnp.where(cols < nrow - 1, out, 0.0).astype(o_ref.dtype)
            else:
                out = _dot(hid.astype(bf16), w2_ref[...])
                o_ref[0, g] = jnp.where(rows < nrow - 1, out, 0.0).astype(o_ref.dtype)

    one(kc_ref, pek_ref, w1k_ref, w2k_ref, ko_ref, False)
    one(vc_ref, pev_ref, w1v_ref, w2v_ref, vo_ref, True)


def _compress(kc, vc, pek, w1k, w2k, pev, w1v, w2v):
    batch, seq, width = kc.shape
    nrow = seq // CMP_STRIDE
    pair = LANES // HEAD_DIM
    tspec = pl.BlockSpec((1, seq, LANES), lambda b, h: (b, 0, h))
    ospec = pl.BlockSpec((1, pair, nrow, LANES), lambda b, h: (b, h, 0, 0))
    wts = [_const_spec((1, CMP_LEN * HEAD_DIM)), _const_spec((CMP_LEN * HEAD_DIM, CMP_HIDDEN))]
    tspec_v = pl.BlockSpec((1, pair, LANES, nrow), lambda b, h: (b, h, 0, 0))
    return pl.pallas_call(
        _compress_body,
        grid=(batch, width // LANES),
        in_specs=([tspec, tspec] + wts + [_const_spec((CMP_HIDDEN, LANES))]
                  + wts + [_const_spec((LANES, CMP_HIDDEN))]),
        out_specs=[ospec, tspec_v],
        out_shape=[jax.ShapeDtypeStruct((batch, N_KV, nrow, LANES), bf16),
                   jax.ShapeDtypeStruct((batch, N_KV, LANES, nrow), bf16)],
        compiler_params=_params(("parallel", "parallel")),
        name="compress",
    )(kc, vc, pek, w1k, w2k, pev, w1v, w2v)


def _bucket_thresholds():
    n = np.arange(0, 4096)
    max_exact = N_BUCKETS // 2
    nf = np.maximum(n, 1).astype(np.float32)
    large = max_exact + (np.log(nf / np.float32(max_exact)) / np.float32(math.log(MAX_DISTANCE / max_exact))
                         * np.float32(N_BUCKETS - max_exact)).astype(np.int32)
    large = np.minimum(large, N_BUCKETS - 1)
    bucket = np.where(n < max_exact, n, large)
    assert np.all(np.diff(bucket) >= 0) and bucket[0] == 0 and bucket[-1] == N_BUCKETS - 1
    return [int(np.argmax(bucket >= k)) for k in range(N_BUCKETS)]


_BUCKET_THR = _bucket_thresholds()
assert _BUCKET_THR[-1] <= (N_SEL_BIAS - 1) * Q_TILE + 1


def _bias_of_dist(dist, ok, tab_ref, head):
    last = tab_ref[N_BUCKETS - 1, head]
    val = jnp.full(dist.shape, (tab_ref[0, head] - last) * LOG2E, f32)
    for k in range(1, N_BUCKETS - 1):
        val = jnp.where(dist >= _BUCKET_THR[k], (tab_ref[k, head] - last) * LOG2E, val)
    val = jnp.where(dist >= _BUCKET_THR[N_BUCKETS - 1], 0.0, val)
    return jnp.where(ok, val, NEG_INF)


def _bias_body(tab_ref, bc_ref, bs_ref, bw_ref):
    g = pl.program_id(0)
    nq, ncmp = bc_ref.shape[1], bc_ref.shape[2]
    per_tile = Q_TILE // CMP_STRIDE
    back = -(-(_BUCKET_THR[-1] + CMP_LEN - 1) // CMP_STRIDE)
    for r in range(HEADS_PER_KV):
        head = g * HEADS_PER_KV + r
        cols = slice(r * Q_TILE, (r + 1) * Q_TILE)
        for t in range(nq):
            lo = max((t * per_tile - back) // SUBLANES * SUBLANES, 0)
            hi = min(-(-((t + 1) * per_tile) // SUBLANES) * SUBLANES, ncmp)
            n = lo + lax.broadcasted_iota(jnp.int32, (hi - lo, Q_TILE), 0)
            i = lax.broadcasted_iota(jnp.int32, (hi - lo, Q_TILE), 1)
            dist = t * Q_TILE + i - (n * CMP_STRIDE + CMP_LEN - 1)
            if lo > 0:
                bc_ref[0, t, 0:lo, cols] = jnp.zeros((lo, Q_TILE), f32)
            bc_ref[0, t, lo:hi, cols] = _bias_of_dist(dist, (dist >= 0) & (n < ncmp - 1), tab_ref, head)
            if hi < ncmp:
                bc_ref[0, t, hi:ncmp, cols] = jnp.full((ncmp - hi, Q_TILE), NEG_INF, f32)

        j = lax.broadcasted_iota(jnp.int32, (Q_TILE, Q_TILE), 0)
        i = lax.broadcasted_iota(jnp.int32, (Q_TILE, Q_TILE), 1)
        for d in range(N_SEL_BIAS):
            dist = d * Q_TILE + i - j
            bs_ref[0, d, :, cols] = _bias_of_dist(dist, dist >= 0, tab_ref, head)
        dist = WINDOW + i - j
        bw_ref[0, 0, :, cols] = _bias_of_dist(dist, dist < WINDOW, tab_ref, head)


def _bias_tiles(rel_bias, seq):
    nq = seq // Q_TILE
    ncmp = seq // CMP_STRIDE
    rows = HEADS_PER_KV * Q_TILE
    return pl.pallas_call(
        _bias_body,
        grid=(N_KV,),
        in_specs=[pl.BlockSpec(memory_space=pltpu.SMEM)],
        out_specs=[pl.BlockSpec((1, nq, ncmp, rows), lambda g: (g, 0, 0, 0)),
                   pl.BlockSpec((1, N_SEL_BIAS, Q_TILE, rows), lambda g: (g, 0, 0, 0)),
                   pl.BlockSpec((1, 1, Q_TILE, rows), lambda g: (g, 0, 0, 0))],
        out_shape=[jax.ShapeDtypeStruct((N_KV, nq, ncmp, rows), f32),
                   jax.ShapeDtypeStruct((N_KV, N_SEL_BIAS, Q_TILE, rows), f32),
                   jax.ShapeDtypeStruct((N_KV, 1, Q_TILE, rows), f32)],
        compiler_params=_params(("parallel",)),
        name="bias_tiles",
    )(rel_bias)


def _importance_matrix_t(seq):
    n_cmp = (seq - CMP_LEN) // CMP_STRIDE + 1
    n_sel = seq // SEL_BLOCK
    ratio_sel = SEL_BLOCK // CMP_STRIDE
    ratio_cmp = CMP_LEN // CMP_STRIDE
    jj = np.arange(n_sel)[:, None, None]
    ci = ratio_sel * jj + np.arange(ratio_sel)[None, :, None] - np.arange(ratio_cmp)[None, None, :]
    jb = np.broadcast_to(jj, ci.shape)
    ok = (ci >= 0) & (ci < n_cmp)
    m = np.zeros((n_sel, seq // CMP_STRIDE), np.float32)
    np.add.at(m, (jb[ok], ci[ok]), 1.0)
    return m


def _scores(qt, k_ref, t, first_tile, biases):
    s = _dot(k_ref[0, first_tile * Q_TILE:(t + 1) * Q_TILE, :], qt)
    pieces = []
    for kt in range(first_tile, t + 1):
        piece = s[(kt - first_tile) * Q_TILE:(kt - first_tile + 1) * Q_TILE, :]
        if t - kt in biases:
            piece = piece + biases[t - kt]
        pieces.append(piece)
    fold = lambda pc: jnp.max(pc.reshape(Q_TILE // SUBLANES, SUBLANES, pc.shape[1]), axis=0)
    return pieces, jnp.max(functools.reduce(jnp.maximum, [fold(pc) for pc in pieces]), axis=0, keepdims=True)


def _weighted_values(pieces, m, vt_ref, t, first_tile):
    p = jnp.concatenate([jnp.exp2(pc - m).astype(bf16) for pc in pieces], axis=0)
    acc = _dot(vt_ref[0:V_ROWS, first_tile * Q_TILE:(t + 1) * Q_TILE], p)
    return acc[0:HEAD_DIM, :], 1.0 / acc[MASK_LO:MASK_LO + 1, :]


def _nsa_body(qt_ref, kc_ref, vct_ref, ks_ref, vst_ref, kw_ref, vwt_ref, gtt_ref, bc_ref, bs_ref, bw_ref,
              mt_ref, o_ref):
    n_sel = mt_ref.shape[0]
    n_tiles = qt_ref.shape[1] // Q_TILE
    cols = HEADS_PER_KV * Q_TILE
    head_cols = [slice(r * Q_TILE, (r + 1) * Q_TILE) for r in range(HEADS_PER_KV)]
    kc = kc_ref[0, 0]
    vct = vct_ref[0, 0]
    mt = mt_ref[...]
    jj = lax.broadcasted_iota(jnp.int32, (n_sel, Q_TILE), 0)
    qcol = lax.broadcasted_iota(jnp.int32, (n_sel, Q_TILE), 1)
    zeros_tail = jnp.zeros((LANES - HEAD_DIM, Q_TILE), bf16)
    zeros_rest = jnp.zeros((LANES - HEAD_DIM - n_sel, Q_TILE), bf16)

    def compressed_scores(t):
        tok = slice(t * Q_TILE, (t + 1) * Q_TILE)
        heads = [qt_ref[r * HEAD_DIM:(r + 1) * HEAD_DIM, tok] for r in range(HEADS_PER_KV)]
        q_plain = jnp.concatenate([jnp.concatenate([hd, zeros_tail], axis=0) for hd in heads], axis=1)
        return heads, q_plain, _dot(kc, q_plain) + bc_ref[0, t]

    def select_stage(t, heads, q_plain, s):
        p = jnp.exp2(s - jnp.max(s, axis=0, keepdims=True))
        norm = 1.0 / jnp.sum(p, axis=0, keepdims=True)
        if (t + 1) * Q_TILE > CMP_LEN - 1 >= t * Q_TILE:
            pos = t * Q_TILE + (lax.broadcasted_iota(jnp.int32, (1, cols), 1) & (Q_TILE - 1))
            norm = jnp.where(pos >= CMP_LEN - 1, norm, 0.0)
        p = p * norm
        o_c = _dot(vct[0:HEAD_DIM, :], p.astype(bf16))

        n_causal = ((t + 1) * Q_TILE - 1) // SEL_BLOCK + 1
        if n_causal <= N_SELECT:
            return q_plain, q_plain, o_c

        psum = p[:, head_cols[0]] + p[:, head_cols[1]] + p[:, head_cols[2]] + p[:, head_cols[3]]
        p_hi = psum.astype(bf16)
        p_lo = (psum - p_hi.astype(f32)).astype(bf16)
        imp = _dot(mt, p_hi) + _dot(mt, p_lo)
        dblk = (t * Q_TILE + qcol) // SEL_BLOCK - jj
        forced = (jj == 0) | ((dblk >= 0) & (dblk < N_LOCAL_FORCED))
        imp = jnp.where(forced, FORCE_SCORE, jnp.where(dblk >= 0, imp, -FORCE_SCORE))
        rank = jnp.zeros((n_sel, Q_TILE), f32)
        for i in range(n_causal):
            row = imp[i:i + 1, :]
            ahead = (row > imp) | ((row == imp) & (jj > i))
            rank = rank + jnp.where(ahead, 1.0, 0.0)
        mask_rows = jnp.where(rank < N_SELECT, 0.0, NEG_INF).astype(bf16)
        q_masked = jnp.concatenate(
            [jnp.concatenate([hd, mask_rows, zeros_rest], axis=0) for hd in heads], axis=1)
        return q_plain, q_masked, o_c

    first_win = lambda t: max(t - WINDOW // Q_TILE, 0)

    def sel_scores(t, q_masked):
        return _scores(q_masked, ks_ref, t, 0, {d: bs_ref[0, d] for d in range(N_SEL_BIAS)})

    def win_scores(t, q_plain):
        band = {d: bs_ref[0, d] for d in range(N_SEL_BIAS)}
        band[WINDOW // Q_TILE] = bw_ref[0, 0]
        return _scores(q_plain, kw_ref, t, first_win(t), band)

    def finish(t, o_c, sel_out, win_out):
        tok = slice(t * Q_TILE, (t + 1) * Q_TILE)
        acc_s, inv_s = sel_out
        acc_w, inv_w = win_out
        gt = jax.nn.sigmoid(gtt_ref[:, tok])
        gate = lambda br: jnp.concatenate([gt[3 * r + br:3 * r + br + 1, :] for r in range(HEADS_PER_KV)], axis=1)
        out_t = gate(0) * o_c + (gate(1) * inv_s) * acc_s + (gate(2) * inv_w) * acc_w
        pairs = [jnp.concatenate([out_t[:, head_cols[2 * h]], out_t[:, head_cols[2 * h + 1]]], axis=0).T
                 for h in range(HEADS_PER_KV // 2)]
        o_ref[0, tok, :] = jnp.concatenate(pairs, axis=-1)

    compressed, queries, o_cs, s_sel, s_win = {}, {}, {}, {}, {}

    def emit_compressed(t):
        if t < n_tiles:
            compressed[t] = compressed_scores(t)

    def emit_select(t):
        if t < n_tiles:
            q_plain, q_masked, o_cs[t] = select_stage(t, *compressed.pop(t))
            queries[t] = (q_plain, q_masked)

    def emit_scores(t):
        if t < n_tiles:
            q_plain, q_masked = queries.pop(t)
            s_sel[t] = sel_scores(t, q_masked)
            s_win[t] = win_scores(t, q_plain)

    for t in range(SKEW + 2):
        emit_compressed(t)
    for t in range(SKEW + 1):
        emit_select(t)
    for t in range(SKEW):
        emit_scores(t)
    for t in range(n_tiles):
        emit_compressed(t + SKEW + 2)
        emit_scores(t + SKEW)
        emit_select(t + SKEW + 1)
        sel_out = _weighted_values(*s_sel.pop(t), vst_ref, t, 0)
        win_out = _weighted_values(*s_win.pop(t), vwt_ref, t, first_win(t))
        finish(t, o_cs.pop(t), sel_out, win_out)


def _nsa(qt, kc, vct, ks, vst, kw, vwt, gtt, bc, bs, bw, mt):
    batch, seq, _ = ks.shape
    nq = seq // Q_TILE
    cols = HEADS_PER_KV * Q_TILE
    ncmp = kc.shape[2]
    width = HEADS_PER_KV * HEAD_DIM
    k_spec = pl.BlockSpec((1, seq, LANES), lambda gi, b: (b, 0, gi))
    vt_spec = pl.BlockSpec((LANES, seq), lambda gi, b: (gi, b))
    return pl.pallas_call(
        _nsa_body,
        grid=(N_KV, batch),
        in_specs=[pl.BlockSpec((width, seq), lambda gi, b: (gi, b)),
                  pl.BlockSpec((1, 1, ncmp, LANES), lambda gi, b: (b, gi, 0, 0)),
                  pl.BlockSpec((1, 1, LANES, ncmp), lambda gi, b: (b, gi, 0, 0)),
                  k_spec, vt_spec, k_spec, vt_spec,
                  pl.BlockSpec((GATE_ROWS, seq), lambda gi, b: (gi, b)),
                  pl.BlockSpec((1, nq, ncmp, cols), lambda gi, b: (gi, 0, 0, 0)),
                  pl.BlockSpec((1, N_SEL_BIAS, Q_TILE, cols), lambda gi, b: (gi, 0, 0, 0)),
                  pl.BlockSpec((1, 1, Q_TILE, cols), lambda gi, b: (gi, 0, 0, 0)),
                  _const_spec(mt.shape)],
        out_specs=pl.BlockSpec((1, seq, width), lambda gi, b: (b, 0, gi)),
        out_shape=jax.ShapeDtypeStruct((batch, seq, N_KV * width), f32),
        compiler_params=_params(("parallel", "arbitrary")),
        name="nsa",
    )(qt, kc, vct, ks, vst, kw, vwt, gtt, bc, bs, bw, mt)


def _out_proj_body(x_ref, yl_ref, yn_ref, gn_ref, w_ref, gp_ref, gm_ref, o_ref, a_ref):
    half = x_ref.shape[0] // 2
    rows = [slice(0, half), slice(half, 2 * half)]
    ys = []
    for r in rows:
        yn = _rms(yn_ref[r, :], gn_ref[...]).astype(bf16)
        ys.append(_dot(yl_ref[r, :], w_ref[0:LRU_WIDTH, :]) + _dot(yn, w_ref[LRU_WIDTH:, :]))
    for r, y in zip(rows, ys):
        h = x_ref[r, :] + _rms(y, gp_ref[...])
        o_ref[r, :] = h
        a_ref[r, :] = _rms(h, gm_ref[...]).astype(bf16)


def _out_proj(x2, yl, yn, gn, w, gp, gm, tm=512):
    m = x2.shape[0]
    row = pl.BlockSpec((tm, D_MODEL), lambda i: (i, 0))
    return pl.pallas_call(
        _out_proj_body,
        grid=(m // tm,),
        in_specs=[row,
                  pl.BlockSpec((tm, LRU_WIDTH), lambda i: (i, 0)),
                  pl.BlockSpec((tm, D_MODEL - LRU_WIDTH), lambda i: (i, 0)),
                  _const_spec((1, D_MODEL - LRU_WIDTH)),
                  _const_spec((D_MODEL, D_MODEL)),
                  _const_spec((1, D_MODEL)), _const_spec((1, D_MODEL))],
        out_specs=[row, row],
        out_shape=[jax.ShapeDtypeStruct((m, D_MODEL), f32), jax.ShapeDtypeStruct((m, D_MODEL), bf16)],
        compiler_params=_params(("parallel",)),
        name="out_proj",
    )(x2, yl, yn, gn, w, gp, gm)


def _mlp_body(a_ref, w1_ref, w2_ref, g2_ref, o_ref, acc_sc):
    j = pl.program_id(1)

    @pl.when(j == 0)
    def _():
        acc_sc[...] = jnp.zeros_like(acc_sc)

    hid = jnp.maximum(_dot(a_ref[...], w1_ref[...]), 0.0)
    acc_sc[...] += _dot((hid * hid).astype(bf16), w2_ref[...])

    @pl.when(j == pl.num_programs(1) - 1)
    def _():
        o_ref[...] = _rms(acc_sc[...], g2_ref[...])


def _mlp(a, w1, w2, g2, tm=1024, tf=1024):
    m = a.shape[0]
    return pl.pallas_call(
        _mlp_body,
        grid=(m // tm, D_FF // tf),
        in_specs=[pl.BlockSpec((tm, D_MODEL), lambda i, j: (i, 0)),
                  pl.BlockSpec((D_MODEL, tf), lambda i, j: (0, j)),
                  pl.BlockSpec((tf, D_MODEL), lambda i, j: (j, 0)),
                  _const_spec((1, D_MODEL))],
        out_specs=pl.BlockSpec((tm, D_MODEL), lambda i, j: (i, 0)),
        out_shape=jax.ShapeDtypeStruct((m, D_MODEL), f32),
        scratch_shapes=[pltpu.VMEM((tm, D_MODEL), f32)],
        compiler_params=_params(("parallel", "arbitrary")),
        name="mlp",
    )(a, w1, w2, g2)


def _ple_body(h_ref, f_ref, p_ref, wg_ref, wp_ref, o_ref):
    half = h_ref.shape[0] // 2
    rows = [slice(0, half), slice(half, 2 * half)]
    staged = []
    for r in rows:
        h = h_ref[r, :] + f_ref[r, :]
        staged.append((h, _dot(h.astype(bf16), wg_ref[...]), _dot(p_ref[r, :].astype(bf16), wp_ref[...])))
    for r, (h, gate, emb) in zip(rows, staged):
        o_ref[r, :] = h + jax.nn.sigmoid(gate) * emb


def _ple(h, f, p2, wg, wp, tm=512):
    m = h.shape[0]
    row = pl.BlockSpec((tm, D_MODEL), lambda i: (i, 0))
    return pl.pallas_call(
        _ple_body,
        grid=(m // tm,),
        in_specs=[row, row,
                  pl.BlockSpec((tm, PLE_DIM), lambda i: (i, 0)),
                  _const_spec((D_MODEL, D_MODEL)),
                  _const_spec((PLE_DIM, D_MODEL))],
        out_specs=row,
        out_shape=jax.ShapeDtypeStruct((m, D_MODEL), f32),
        compiler_params=_params(("parallel",)),
        name="ple",
    )(h, f, p2, wg, wp)


def _block_diag_chunks(w):
    per = GATE_CHUNK // LRU_BLOCK_DIM
    w = w.reshape(LRU_BLOCKS // per, per, LRU_BLOCK_DIM, LRU_BLOCK_DIM)
    eye = jnp.eye(per, dtype=w.dtype)
    return jnp.einsum('cpij,pq->cpiqj', w, eye).reshape(LRU_BLOCKS // per, GATE_CHUNK, GATE_CHUNK)


def _layer(h, p_i, i, prm, bias):
    batch, seq, _ = h.shape
    m = batch * seq
    x2 = h.reshape(m, D_MODEL)
    row = lambda v: v.reshape(1, -1)
    per_batch = lambda t: t.reshape(batch, seq, t.shape[-1])

    w_in = prm["w_in"][i].astype(bf16)
    cols = lambda name: w_in[:, IN_COLS[name][0]:IN_COLS[name][1]]
    per = 3 * HEADS_PER_KV
    w_gate = jnp.pad(cols("gates").reshape(D_MODEL, N_KV, per), ((0, 0), (0, 0), (0, GATE_ROWS - per)))
    w_feat = jnp.concatenate([cols(n) for n in FEAT_STREAMS] + [w_gate.reshape(D_MODEL, N_KV * GATE_ROWS)],
                             axis=1).T
    u, gl, kc, vc, ks, kw, qt, vst, vwt, gtt = _in_proj(x2, row(prm["norm_mix_pre"][i]), w_in, w_feat, seq)

    y_lru = _rglru(u, gl, prm["conv_w"][i], row(prm["conv_b"][i]),
                   _block_diag_chunks(prm["lru_wa"][i]).astype(bf16), row(prm["lru_ba"][i]),
                   _block_diag_chunks(prm["lru_wx"][i]).astype(bf16), row(prm["lru_bx"][i]),
                   row(prm["lru_lambda"][i]), row(prm["gnorm_lru"][i]), batch, seq)

    lane_pad = lambda w: jnp.pad(w, ((0, 0), (0, LANES - HEAD_DIM))).astype(bf16)
    kcc, vcct = _compress(per_batch(kc), per_batch(vc),
                          prm["cmp_pe_k"][i].reshape(1, -1), prm["cmp_w1_k"][i].astype(bf16),
                          lane_pad(prm["cmp_w2_k"][i]),
                          prm["cmp_pe_v"][i].reshape(1, -1), prm["cmp_w1_v"][i].astype(bf16),
                          lane_pad(prm["cmp_w2_v"][i]).T)

    bc, bs, bw = bias
    y_nsa = _nsa(qt, kcc, vcct, per_batch(ks), vst, per_batch(kw), vwt, gtt, bc, bs, bw,
                 jnp.asarray(_importance_matrix_t(seq)).astype(bf16))

    h1, a1 = _out_proj(x2, y_lru, y_nsa.reshape(m, -1), row(prm["gnorm_nsa"][i]),
                       prm["w_out"][i].astype(bf16), row(prm["norm_mix_post"][i]), row(prm["norm_mlp_pre"][i]))
    f = _mlp(a1, prm["mlp_w1"][i].astype(bf16), prm["mlp_w2"][i].astype(bf16), row(prm["norm_mlp_post"][i]))
    h3 = _ple(h1, f, p_i.reshape(m, PLE_DIM), prm["ple_gate"][i].astype(bf16), prm["ple_proj"][i].astype(bf16))
    return h3.reshape(batch, seq, D_MODEL)


def kernel(x, p, norm_mix_pre, norm_mix_post, norm_mlp_pre, norm_mlp_post, w_in, conv_w, conv_b, lru_wa, lru_ba, lru_wx, lru_bx, lru_lambda, cmp_pe_k, cmp_w1_k, cmp_w2_k, cmp_pe_v, cmp_w1_v, cmp_w2_v, rel_bias, gnorm_lru, gnorm_nsa, w_out, mlp_w1, mlp_w2, ple_gate, ple_proj):
    prm = dict(norm_mix_pre=norm_mix_pre, norm_mix_post=norm_mix_post, norm_mlp_pre=norm_mlp_pre,
               norm_mlp_post=norm_mlp_post, w_in=w_in, conv_w=conv_w, conv_b=conv_b, lru_wa=lru_wa,
               lru_ba=lru_ba, lru_wx=lru_wx, lru_bx=lru_bx, lru_lambda=lru_lambda, cmp_pe_k=cmp_pe_k,
               cmp_w1_k=cmp_w1_k, cmp_w2_k=cmp_w2_k, cmp_pe_v=cmp_pe_v, cmp_w1_v=cmp_w1_v,
               cmp_w2_v=cmp_w2_v, gnorm_lru=gnorm_lru, gnorm_nsa=gnorm_nsa, w_out=w_out, mlp_w1=mlp_w1,
               mlp_w2=mlp_w2, ple_gate=ple_gate, ple_proj=ple_proj)
    bias = _bias_tiles(rel_bias, x.shape[1])
    h = x
    for i in range(w_in.shape[0]):
        h = _layer(h, p[i], i, prm, bias)
    return h
```

```python
import functools
import math

import numpy as np
import jax
import jax.numpy as jnp
from jax import lax
from jax.experimental import pallas as pl
from jax.experimental.pallas import tpu as pltpu

D_MODEL = 2048
PLE_DIM = 256
LRU_WIDTH = 1024
LRU_BLOCKS = 16
LRU_BLOCK_DIM = 64
CONV_WIDTH = 4
LRU_C = 8.0
HEAD_DIM = 64
N_HEADS = 16
N_KV = 4
HEADS_PER_KV = 4
CMP_LEN = 32
CMP_STRIDE = 16
CMP_HIDDEN = 256
SEL_BLOCK = 64
N_SELECT = 16
N_LOCAL_FORCED = 2
WINDOW = 512
N_BUCKETS = 32
MAX_DISTANCE = 128
D_FF = 4 * D_MODEL
NORM_EPS = 1e-6
SQRT_GUARD = 1e-30
NEG_INF = -1e30
FORCE_SCORE = 1e4
KV_W = N_KV * HEAD_DIM


def _offsets(sizes):
    out, pos = {}, 0
    for name, width in sizes:
        out[name] = (pos, pos + width)
        pos += width
    return out, pos


IN_COLS, IN_DIM = _offsets([("u", LRU_WIDTH), ("gl", LRU_WIDTH), ("q", N_HEADS * HEAD_DIM), ("kc", KV_W),
                            ("vc", KV_W), ("ks", KV_W), ("vs", KV_W), ("kw", KV_W), ("vw", KV_W),
                            ("gates", 3 * N_HEADS)])
GATE_ROWS = 16
FEAT_STREAMS = ("q", "vs", "vw")
FEAT_ROWS, FEAT_END = _offsets([(n, IN_COLS[n][1] - IN_COLS[n][0]) for n in FEAT_STREAMS]
                               + [("gates", N_KV * GATE_ROWS)])

SUBLANES = 8
GATE_CHUNK = 256
LANES = 128
assert all(lo % LANES == 0 for lo, _ in IN_COLS.values())
MASK_LO = HEAD_DIM
Q_TILE = 128
N_SEL_BIAS = 2
V_ROWS = 80
SKEW = 2
LOG2E = math.log2(math.e)

VMEM_LIMIT = 56 * 1024 * 1024

f32 = jnp.float32
bf16 = jnp.bfloat16


def _rms(x, g):
    return x * lax.rsqrt(jnp.mean(x * x, axis=-1, keepdims=True) + NORM_EPS) * g


def _dot(a, b):
    return jnp.dot(a, b, preferred_element_type=f32)


def _dot_nt(a, b):
    return lax.dot_general(a, b, (((1,), (1,)), ((), ())), preferred_element_type=f32)


def _const_spec(shape):
    nd = len(shape)
    return pl.BlockSpec(shape, lambda *_: (0,) * nd)


def _params(sem):
    return pltpu.CompilerParams(dimension_semantics=sem, vmem_limit_bytes=VMEM_LIMIT)


def _spread_groups(z, fill):
    lane = lax.broadcasted_iota(jnp.int32, (z.shape[0], LANES), 1)
    parts = []
    for g in range(N_KV):
        pair = z[:, (g // 2) * LANES:(g // 2 + 1) * LANES]
        if g % 2:
            pair = pltpu.roll(pair, HEAD_DIM, 1)
        parts.append(jnp.where(lane < HEAD_DIM, pair, fill))
    return jnp.concatenate(parts, axis=1)


def _value_rows(z):
    tm = z.shape[1]
    ones_row = jnp.where(lax.broadcasted_iota(jnp.int32, (LANES - HEAD_DIM, tm), 0) == 0, 1.0, 0.0)
    parts = []
    for g in range(N_KV):
        parts += [z[g * HEAD_DIM:(g + 1) * HEAD_DIM, :], ones_row]
    return jnp.concatenate(parts, axis=0)


def _in_proj_body(seq, x_ref, g_ref, w_ref, wt_ref, u_ref, gl_ref, kc_ref, vc_ref, ks_ref, kw_ref,
                  qt_ref, vst_ref, vwt_ref, gtt_ref):
    tm = x_ref.shape[0]
    a = _rms(x_ref[...], g_ref[...]).astype(bf16)

    def tok(name):
        lo, hi = IN_COLS[name]
        return _dot(a, w_ref[:, lo:hi])

    def feat(name):
        lo, hi = FEAT_ROWS[name]
        return _dot_nt(wt_ref[lo:hi, :], a)

    u_ref[...] = tok("u")
    gl_ref[...] = tok("gl")
    kc_ref[...] = tok("kc")
    vc_ref[...] = tok("vc")
    lane = lax.broadcasted_iota(jnp.int32, (tm, LANES), 1)
    pos = (pl.program_id(0) * tm) % seq + lax.broadcasted_iota(jnp.int32, (tm, LANES), 0)
    block_onehot = jnp.where(lane - MASK_LO == pos // SEL_BLOCK, 1.0, 0.0)
    ks_ref[...] = _spread_groups(tok("ks"), block_onehot).astype(bf16)
    kw_ref[...] = _spread_groups(tok("kw"), 0.0).astype(bf16)

    qt_ref[...] = (feat("q") * (LOG2E * HEAD_DIM ** -0.5)).astype(bf16)
    vst_ref[...] = _value_rows(feat("vs")).astype(bf16)
    vwt_ref[...] = _value_rows(feat("vw")).astype(bf16)
    gtt_ref[...] = feat("gates")


def _in_proj(x2, g, w, wt, seq, tm=512):
    m = x2.shape[0]
    wide = N_KV * LANES
    tok_out = [(LRU_WIDTH, f32), (LRU_WIDTH, f32), (KV_W, f32), (KV_W, f32), (wide, bf16), (wide, bf16)]
    feat_out = [(N_HEADS * HEAD_DIM, bf16), (wide, bf16), (wide, bf16), (N_KV * GATE_ROWS, f32)]
    body = lambda *refs: _in_proj_body(seq, *refs)
    return pl.pallas_call(
        body,
        grid=(m // tm,),
        in_specs=[pl.BlockSpec((tm, D_MODEL), lambda i: (i, 0)),
                  _const_spec((1, D_MODEL)),
                  pl.BlockSpec((D_MODEL, IN_DIM), lambda i: (0, 0), pipeline_mode=pl.Buffered(1)),
                  pl.BlockSpec((FEAT_END, D_MODEL), lambda i: (0, 0), pipeline_mode=pl.Buffered(1))],
        out_specs=([pl.BlockSpec((tm, n), lambda i: (i, 0)) for n, _ in tok_out]
                   + [pl.BlockSpec((n, tm), lambda i: (0, i)) for n, _ in feat_out]),
        out_shape=([jax.ShapeDtypeStruct((m, n), dt) for n, dt in tok_out]
                   + [jax.ShapeDtypeStruct((n, m), dt) for n, dt in feat_out]),
        compiler_params=_params(("parallel",)),
        name="in_proj",
    )(x2, g, w, wt)


def _rglru_body(u_ref, gl_ref, cw_ref, cb_ref, wa_ref, ba_ref, wx_ref, bx_ref, lam_ref, gn_ref,
                o_ref, ubuf, a_sc, b_sc, hc_sc):
    t = u_ref.shape[0]

    @pl.when(pl.program_id(1) == 0)
    def _():
        ubuf[...] = jnp.zeros_like(ubuf)
        hc_sc[...] = jnp.zeros_like(hc_sc)

    u = u_ref[...]
    u3 = u.reshape(t // SUBLANES, SUBLANES, LRU_WIDTH)
    tail = ubuf[...]
    cw = cw_ref[...]
    row8w = lax.broadcasted_iota(jnp.int32, (t // SUBLANES, SUBLANES, LRU_WIDTH), 1)
    xc3 = cb_ref[...] + u3 * cw[CONV_WIDTH - 1]
    for d in range(1, CONV_WIDTH):
        cur = pltpu.roll(u3, d, 1)
        prev = jnp.concatenate([pltpu.roll(tail, d, 0)[None], cur[:-1]], axis=0)
        xc3 = xc3 + jnp.where(row8w >= d, cur, prev) * cw[CONV_WIDTH - 1 - d]
    ubuf[...] = u[t - SUBLANES:t, :]
    xc = xc3.reshape(t, LRU_WIDTH)

    xb = xc.astype(bf16)
    sp = jax.nn.softplus(-lam_ref[...])
    row8 = lax.broadcasted_iota(jnp.int32, (t // SUBLANES, SUBLANES, GATE_CHUNK), 1)
    for c in range(LRU_WIDTH // GATE_CHUNK):
        sl = slice(c * GATE_CHUNK, (c + 1) * GATE_CHUNK)
        xcb = xb[:, sl]
        r = jax.nn.sigmoid(_dot(xcb, wa_ref[c]) + ba_ref[:, sl])
        ig = jax.nn.sigmoid(_dot(xcb, wx_ref[c]) + bx_ref[:, sl])
        log_a = (-LRU_C) * r * sp[:, sl]
        a = jnp.exp(log_a)
        y = 1.0 - a * a
        b = (y * lax.rsqrt(jnp.maximum(y, SQRT_GUARD))) * (ig * xc[:, sl])
        a = a.reshape(t // SUBLANES, SUBLANES, GATE_CHUNK)
        b = b.reshape(t // SUBLANES, SUBLANES, GATE_CHUNK)
        for d in (1, 2, 4):
            keep = row8 >= d
            a_prev = pltpu.roll(a, d, 1)
            b_prev = pltpu.roll(b, d, 1)
            b = jnp.where(keep, a * b_prev + b, b)
            a = jnp.where(keep, a * a_prev, a)
        a_sc[:, sl] = a.reshape(t, GATE_CHUNK)
        b_sc[:, sl] = b.reshape(t, GATE_CHUNK)

    def group(gi, h):
        rows = pl.ds(pl.multiple_of(gi * SUBLANES, SUBLANES), SUBLANES)
        hg = b_sc[rows, :] + a_sc[rows, :] * h
        b_sc[rows, :] = hg
        return hg[SUBLANES - 1:SUBLANES, :]

    h_last = lax.fori_loop(0, t // SUBLANES, group, hc_sc[0:1, :])
    hc_sc[0:1, :] = h_last
    y = b_sc[...] * jax.nn.gelu(gl_ref[...])
    o_ref[...] = _rms(y, gn_ref[...]).astype(o_ref.dtype)


def _rglru(u, gl, cw, cb, wa, ba, wx, bx, lam, gn, batch, seq, t=256):
    ns = seq // t
    row = pl.BlockSpec((t, LRU_WIDTH), lambda b, s: (b * ns + s, 0))
    vec = _const_spec((1, LRU_WIDTH))
    wspec = _const_spec((LRU_WIDTH // GATE_CHUNK, GATE_CHUNK, GATE_CHUNK))
    return pl.pallas_call(
        _rglru_body,
        grid=(batch, ns),
        in_specs=[row, row, _const_spec((CONV_WIDTH, LRU_WIDTH)), vec, wspec, vec, wspec, vec, vec, vec],
        out_specs=row,
        out_shape=jax.ShapeDtypeStruct((batch * seq, LRU_WIDTH), bf16),
        scratch_shapes=[pltpu.VMEM((SUBLANES, LRU_WIDTH), f32), pltpu.VMEM((t, LRU_WIDTH), f32),
                        pltpu.VMEM((t, LRU_WIDTH), f32), pltpu.VMEM((SUBLANES, LRU_WIDTH), f32)],
        compiler_params=_params(("parallel", "arbitrary")),
        name="rglru",
    )(u, gl, cw, cb, wa, ba, wx, bx, lam, gn)


def _compress_body(kc_ref, vc_ref, pek_ref, w1k_ref, w2k_ref, pev_ref, w1v_ref, w2v_ref, ko_ref, vo_ref):
    nrow = ko_ref.shape[2]
    rows = lax.broadcasted_iota(jnp.int32, (nrow, LANES), 0)
    cols = lax.broadcasted_iota(jnp.int32, (LANES, nrow), 1)
    lane64 = lax.broadcasted_iota(jnp.int32, (nrow, LANES), 1) < HEAD_DIM
    half = CMP_STRIDE * HEAD_DIM

    def one(t_ref, pe_ref, w1_ref, w2_ref, o_ref, transposed):
        pe = jnp.broadcast_to(pe_ref[...], (8, CMP_LEN * HEAD_DIM)).astype(bf16)
        c0 = _dot(pe, w1_ref[...])[0:1, :]
        toks = [t_ref[0, pl.ds(j, nrow, stride=CMP_STRIDE), :] for j in range(CMP_STRIDE)]
        for g in range(LANES // HEAD_DIM):
            blocks = []
            for i in range(CMP_STRIDE // 2):
                even, odd = toks[2 * i], toks[2 * i + 1]
                if g == 0:
                    blocks.append(jnp.where(lane64, even, pltpu.roll(odd, HEAD_DIM, 1)))
                else:
                    blocks.append(jnp.where(lane64, pltpu.roll(even, HEAD_DIM, 1), odd))
            chunk = jnp.concatenate(blocks, axis=1).astype(bf16)
            lo = _dot(chunk, w1_ref[0:half, :])
            hi = _dot(chunk, w1_ref[half:2 * half, :])
            hid = jax.nn.gelu(lo + pltpu.roll(hi, nrow - 1, 0) + c0)
            if transposed:
                out = _dot_nt(w2_ref[...], hid.astype(bf16))
                o_ref[0, g] = jnp.where(cols < nrow - 1, out, 0.0).astype(o_ref.dtype)
            else:
                out = _dot(hid.astype(bf16), w2_ref[...])
                o_ref[0, g] = jnp.where(rows < nrow - 1, out, 0.0).astype(o_ref.dtype)

    one(kc_ref, pek_ref, w1k_ref, w2k_ref, ko_ref, False)
    one(vc_ref, pev_ref, w1v_ref, w2v_ref, vo_ref, True)


def _compress(kc, vc, pek, w1k, w2k, pev, w1v, w2v):
    batch, seq, width = kc.shape
    nrow = seq // CMP_STRIDE
    pair = LANES // HEAD_DIM
    tspec = pl.BlockSpec((1, seq, LANES), lambda b, h: (b, 0, h))
    ospec = pl.BlockSpec((1, pair, nrow, LANES), lambda b, h: (b, h, 0, 0))
    wts = [_const_spec((1, CMP_LEN * HEAD_DIM)), _const_spec((CMP_LEN * HEAD_DIM, CMP_HIDDEN))]
    tspec_v = pl.BlockSpec((1, pair, LANES, nrow), lambda b, h: (b, h, 0, 0))
    return pl.pallas_call(
        _compress_body,
        grid=(batch, width // LANES),
        in_specs=([tspec, tspec] + wts + [_const_spec((CMP_HIDDEN, LANES))]
                  + wts + [_const_spec((LANES, CMP_HIDDEN))]),
        out_specs=[ospec, tspec_v],
        out_shape=[jax.ShapeDtypeStruct((batch, N_KV, nrow, LANES), bf16),
                   jax.ShapeDtypeStruct((batch, N_KV, LANES, nrow), bf16)],
        compiler_params=_params(("parallel", "parallel")),
        name="compress",
    )(kc, vc, pek, w1k, w2k, pev, w1v, w2v)


def _bucket_thresholds():
    n = np.arange(0, 4096)
    max_exact = N_BUCKETS // 2
    nf = np.maximum(n, 1).astype(np.float32)
    large = max_exact + (np.log(nf / np.float32(max_exact)) / np.float32(math.log(MAX_DISTANCE / max_exact))
                         * np.float32(N_BUCKETS - max_exact)).astype(np.int32)
    large = np.minimum(large, N_BUCKETS - 1)
    bucket = np.where(n < max_exact, n, large)
    assert np.all(np.diff(bucket) >= 0) and bucket[0] == 0 and bucket[-1] == N_BUCKETS - 1
    return [int(np.argmax(bucket >= k)) for k in range(N_BUCKETS)]


_BUCKET_THR = _bucket_thresholds()
assert _BUCKET_THR[-1] <= (N_SEL_BIAS - 1) * Q_TILE + 1


def _bias_of_dist(dist, ok, tab_ref, head):
    last = tab_ref[N_BUCKETS - 1, head]
    val = jnp.full(dist.shape, (tab_ref[0, head] - last) * LOG2E, f32)
    for k in range(1, N_BUCKETS - 1):
        val = jnp.where(dist >= _BUCKET_THR[k], (tab_ref[k, head] - last) * LOG2E, val)
    val = jnp.where(dist >= _BUCKET_THR[N_BUCKETS - 1], 0.0, val)
    return jnp.where(ok, val, NEG_INF)


def _bias_body(tab_ref, bc_ref, bs_ref, bw_ref):
    g = pl.program_id(0)
    nq, ncmp = bc_ref.shape[1], bc_ref.shape[2]
    per_tile = Q_TILE // CMP_STRIDE
    back = -(-(_BUCKET_THR[-1] + CMP_LEN - 1) // CMP_STRIDE)
    for r in range(HEADS_PER_KV):
        head = g * HEADS_PER_KV + r
        cols = slice(r * Q_TILE, (r + 1) * Q_TILE)
        for t in range(nq):
            lo = max((t * per_tile - back) // SUBLANES * SUBLANES, 0)
            hi = min(-(-((t + 1) * per_tile) // SUBLANES) * SUBLANES, ncmp)
            n = lo + lax.broadcasted_iota(jnp.int32, (hi - lo, Q_TILE), 0)
            i = lax.broadcasted_iota(jnp.int32, (hi - lo, Q_TILE), 1)
            dist = t * Q_TILE + i - (n * CMP_STRIDE + CMP_LEN - 1)
            if lo > 0:
                bc_ref[0, t, 0:lo, cols] = jnp.zeros((lo, Q_TILE), f32)
            bc_ref[0, t, lo:hi, cols] = _bias_of_dist(dist, (dist >= 0) & (n < ncmp - 1), tab_ref, head)
            if hi < ncmp:
                bc_ref[0, t, hi:ncmp, cols] = jnp.full((ncmp - hi, Q_TILE), NEG_INF, f32)

        j = lax.broadcasted_iota(jnp.int32, (Q_TILE, Q_TILE), 0)
        i = lax.broadcasted_iota(jnp.int32, (Q_TILE, Q_TILE), 1)
        for d in range(N_SEL_BIAS):
            dist = d * Q_TILE + i - j
            bs_ref[0, d, :, cols] = _bias_of_dist(dist, dist >= 0, tab_ref, head)
        dist = WINDOW + i - j
        bw_ref[0, 0, :, cols] = _bias_of_dist(dist, dist < WINDOW, tab_ref, head)


def _bias_tiles(rel_bias, seq):
    nq = seq // Q_TILE
    ncmp = seq // CMP_STRIDE
    rows = HEADS_PER_KV * Q_TILE
    return pl.pallas_call(
        _bias_body,
        grid=(N_KV,),
        in_specs=[pl.BlockSpec(memory_space=pltpu.SMEM)],
        out_specs=[pl.BlockSpec((1, nq, ncmp, rows), lambda g: (g, 0, 0, 0)),
                   pl.BlockSpec((1, N_SEL_BIAS, Q_TILE, rows), lambda g: (g, 0, 0, 0)),
                   pl.BlockSpec((1, 1, Q_TILE, rows), lambda g: (g, 0, 0, 0))],
        out_shape=[jax.ShapeDtypeStruct((N_KV, nq, ncmp, rows), f32),
                   jax.ShapeDtypeStruct((N_KV, N_SEL_BIAS, Q_TILE, rows), f32),
                   jax.ShapeDtypeStruct((N_KV, 1, Q_TILE, rows), f32)],
        compiler_params=_params(("parallel",)),
        name="bias_tiles",
    )(rel_bias)


def _importance_matrix_t(seq):
    n_cmp = (seq - CMP_LEN) // CMP_STRIDE + 1
    n_sel = seq // SEL_BLOCK
    ratio_sel = SEL_BLOCK // CMP_STRIDE
    ratio_cmp = CMP_LEN // CMP_STRIDE
    jj = np.arange(n_sel)[:, None, None]
    ci = ratio_sel * jj + np.arange(ratio_sel)[None, :, None] - np.arange(ratio_cmp)[None, None, :]
    jb = np.broadcast_to(jj, ci.shape)
    ok = (ci >= 0) & (ci < n_cmp)
    m = np.zeros((n_sel, seq // CMP_STRIDE), np.float32)
    np.add.at(m, (jb[ok], ci[ok]), 1.0)
    return m


def _scores(qt, k_ref, t, first_tile, biases):
    s = _dot(k_ref[0, first_tile * Q_TILE:(t + 1) * Q_TILE, :], qt)
    pieces = []
    for kt in range(first_tile, t + 1):
        piece = s[(kt - first_tile) * Q_TILE:(kt - first_tile + 1) * Q_TILE, :]
        if t - kt in biases:
            piece = piece + biases[t - kt]
        pieces.append(piece)
    fold = lambda pc: jnp.max(pc.reshape(Q_TILE // SUBLANES, SUBLANES, pc.shape[1]), axis=0)
    return pieces, jnp.max(functools.reduce(jnp.maximum, [fold(pc) for pc in pieces]), axis=0, keepdims=True)


def _weighted_values(pieces, m, vt_ref, t, first_tile):
    p = jnp.concatenate([jnp.exp2(pc - m).astype(bf16) for pc in pieces], axis=0)
    acc = _dot(vt_ref[0:V_ROWS, first_tile * Q_TILE:(t + 1) * Q_TILE], p)
    return acc[0:HEAD_DIM, :], 1.0 / acc[MASK_LO:MASK_LO + 1, :]


def _nsa_body(qt_ref, kc_ref, vct_ref, ks_ref, vst_ref, kw_ref, vwt_ref, gtt_ref, bc_ref, bs_ref, bw_ref,
              mt_ref, o_ref):
    n_sel = mt_ref.shape[0]
    n_tiles = qt_ref.shape[1] // Q_TILE
    cols = HEADS_PER_KV * Q_TILE
    head_cols = [slice(r * Q_TILE, (r + 1) * Q_TILE) for r in range(HEADS_PER_KV)]
    kc = kc_ref[0, 0]
    vct = vct_ref[0, 0]
    mt = mt_ref[...]
    jj = lax.broadcasted_iota(jnp.int32, (n_sel, Q_TILE), 0)
    qcol = lax.broadcasted_iota(jnp.int32, (n_sel, Q_TILE), 1)
    zeros_tail = jnp.zeros((LANES - HEAD_DIM, Q_TILE), bf16)
    zeros_rest = jnp.zeros((LANES - HEAD_DIM - n_sel, Q_TILE), bf16)

    def compressed_scores(t):
        tok = slice(t * Q_TILE, (t + 1) * Q_TILE)
        heads = [qt_ref[r * HEAD_DIM:(r + 1) * HEAD_DIM, tok] for r in range(HEADS_PER_KV)]
        q_plain = jnp.concatenate([jnp.concatenate([hd, zeros_tail], axis=0) for hd in heads], axis=1)
        return heads, q_plain, _dot(kc, q_plain) + bc_ref[0, t]

    def select_stage(t, heads, q_plain, s):
        p = jnp.exp2(s - jnp.max(s, axis=0, keepdims=True))
        norm = 1.0 / jnp.sum(p, axis=0, keepdims=True)
        if (t + 1) * Q_TILE > CMP_LEN - 1 >= t * Q_TILE:
            pos = t * Q_TILE + (lax.broadcasted_iota(jnp.int32, (1, cols), 1) & (Q_TILE - 1))
            norm = jnp.where(pos >= CMP_LEN - 1, norm, 0.0)
        p = p * norm
        o_c = _dot(vct[0:HEAD_DIM, :], p.astype(bf16))

        n_causal = ((t + 1) * Q_TILE - 1) // SEL_BLOCK + 1
        if n_causal <= N_SELECT:
            return q_plain, q_plain, o_c

        psum = p[:, head_cols[0]] + p[:, head_cols[1]] + p[:, head_cols[2]] + p[:, head_cols[3]]
        p_hi = psum.astype(bf16)
        p_lo = (psum - p_hi.astype(f32)).astype(bf16)
        imp = _dot(mt, p_hi) + _dot(mt, p_lo)
        dblk = (t * Q_TILE + qcol) // SEL_BLOCK - jj
        forced = (jj == 0) | ((dblk >= 0) & (dblk < N_LOCAL_FORCED))
        imp = jnp.where(forced, FORCE_SCORE, jnp.where(dblk >= 0, imp, -FORCE_SCORE))
        rank = jnp.zeros((n_sel, Q_TILE), f32)
        for i in range(n_causal):
            row = imp[i:i + 1, :]
            ahead = (row > imp) | ((row == imp) & (jj > i))
            rank = rank + jnp.where(ahead, 1.0, 0.0)
        mask_rows = jnp.where(rank < N_SELECT, 0.0, NEG_INF).astype(bf16)
        q_masked = jnp.concatenate(
            [jnp.concatenate([hd, mask_rows, zeros_rest], axis=0) for hd in heads], axis=1)
        return q_plain, q_masked, o_c

    first_win = lambda t: max(t - WINDOW // Q_TILE, 0)

    def sel_scores(t, q_masked):
        return _scores(q_masked, ks_ref, t, 0, {d: bs_ref[0, d] for d in range(N_SEL_BIAS)})

    def win_scores(t, q_plain):
        band = {d: bs_ref[0, d] for d in range(N_SEL_BIAS)}
        band[WINDOW // Q_TILE] = bw_ref[0, 0]
        return _scores(q_plain, kw_ref, t, first_win(t), band)

    def finish(t, o_c, sel_out, win_out):
        tok = slice(t * Q_TILE, (t + 1) * Q_TILE)
        acc_s, inv_s = sel_out
        acc_w, inv_w = win_out
        gt = jax.nn.sigmoid(gtt_ref[:, tok])
        gate = lambda br: jnp.concatenate([gt[3 * r + br:3 * r + br + 1, :] for r in range(HEADS_PER_KV)], axis=1)
        out_t = gate(0) * o_c + (gate(1) * inv_s) * acc_s + (gate(2) * inv_w) * acc_w
        pairs = [jnp.concatenate([out_t[:, head_cols[2 * h]], out_t[:, head_cols[2 * h + 1]]], axis=0).T
                 for h in range(HEADS_PER_KV // 2)]
        o_ref[0, tok, :] = jnp.concatenate(pairs, axis=-1).astype(o_ref.dtype)

    compressed, queries, o_cs, s_sel, s_win = {}, {}, {}, {}, {}

    def emit_compressed(t):
        if t < n_tiles:
            compressed[t] = compressed_scores(t)

    def emit_select(t):
        if t < n_tiles:
            q_plain, q_masked, o_cs[t] = select_stage(t, *compressed.pop(t))
            queries[t] = (q_plain, q_masked)

    def emit_scores(t):
        if t < n_tiles:
            q_plain, q_masked = queries.pop(t)
            s_sel[t] = sel_scores(t, q_masked)
            s_win[t] = win_scores(t, q_plain)

    for t in range(SKEW + 2):
        emit_compressed(t)
    for t in range(SKEW + 1):
        emit_select(t)
    for t in range(SKEW):
        emit_scores(t)
    for t in range(n_tiles):
        emit_compressed(t + SKEW + 2)
        emit_scores(t + SKEW)
        emit_select(t + SKEW + 1)
        sel_out = _weighted_values(*s_sel.pop(t), vst_ref, t, 0)
        win_out = _weighted_values(*s_win.pop(t), vwt_ref, t, first_win(t))
        finish(t, o_cs.pop(t), sel_out, win_out)


def _nsa(qt, kc, vct, ks, vst, kw, vwt, gtt, bc, bs, bw, mt):
    batch, seq, _ = ks.shape
    nq = seq // Q_TILE
    cols = HEADS_PER_KV * Q_TILE
    ncmp = kc.shape[2]
    width = HEADS_PER_KV * HEAD_DIM
    k_spec = pl.BlockSpec((1, seq, LANES), lambda gi, b: (b, 0, gi))
    vt_spec = pl.BlockSpec((LANES, seq), lambda gi, b: (gi, b))
    return pl.pallas_call(
        _nsa_body,
        grid=(N_KV, batch),
        in_specs=[pl.BlockSpec((width, seq), lambda gi, b: (gi, b)),
                  pl.BlockSpec((1, 1, ncmp, LANES), lambda gi, b: (b, gi, 0, 0)),
                  pl.BlockSpec((1, 1, LANES, ncmp), lambda gi, b: (b, gi, 0, 0)),
                  k_spec, vt_spec, k_spec, vt_spec,
                  pl.BlockSpec((GATE_ROWS, seq), lambda gi, b: (gi, b)),
                  pl.BlockSpec((1, nq, ncmp, cols), lambda gi, b: (gi, 0, 0, 0)),
                  pl.BlockSpec((1, N_SEL_BIAS, Q_TILE, cols), lambda gi, b: (gi, 0, 0, 0)),
                  pl.BlockSpec((1, 1, Q_TILE, cols), lambda gi, b: (gi, 0, 0, 0)),
                  _const_spec(mt.shape)],
        out_specs=pl.BlockSpec((1, seq, width), lambda gi, b: (b, 0, gi)),
        out_shape=jax.ShapeDtypeStruct((batch, seq, N_KV * width), bf16),
        compiler_params=_params(("parallel", "arbitrary")),
        name="nsa",
    )(qt, kc, vct, ks, vst, kw, vwt, gtt, bc, bs, bw, mt)


def _out_proj_body(x_ref, yl_ref, yn_ref, gn_ref, w_ref, gp_ref, gm_ref, o_ref, a_ref):
    half = x_ref.shape[0] // 2
    rows = [slice(0, half), slice(half, 2 * half)]
    ys = []
    for r in rows:
        yn = _rms(yn_ref[r, :].astype(f32), gn_ref[...]).astype(bf16)
        ys.append(_dot(yl_ref[r, :], w_ref[0:LRU_WIDTH, :]) + _dot(yn, w_ref[LRU_WIDTH:, :]))
    for r, y in zip(rows, ys):
        h = x_ref[r, :] + _rms(y, gp_ref[...])
        o_ref[r, :] = h
        a_ref[r, :] = _rms(h, gm_ref[...]).astype(bf16)


def _out_proj(x2, yl, yn, gn, w, gp, gm, tm=512):
    m = x2.shape[0]
    row = pl.BlockSpec((tm, D_MODEL), lambda i: (i, 0))
    return pl.pallas_call(
        _out_proj_body,
        grid=(m // tm,),
        in_specs=[row,
                  pl.BlockSpec((tm, LRU_WIDTH), lambda i: (i, 0)),
                  pl.BlockSpec((tm, D_MODEL - LRU_WIDTH), lambda i: (i, 0)),
                  _const_spec((1, D_MODEL - LRU_WIDTH)),
                  _const_spec((D_MODEL, D_MODEL)),
                  _const_spec((1, D_MODEL)), _const_spec((1, D_MODEL))],
        out_specs=[row, row],
        out_shape=[jax.ShapeDtypeStruct((m, D_MODEL), f32), jax.ShapeDtypeStruct((m, D_MODEL), bf16)],
        compiler_params=_params(("parallel",)),
        name="out_proj",
    )(x2, yl, yn, gn, w, gp, gm)


def _mlp_body(a_ref, w1_ref, w2_ref, g2_ref, o_ref, acc_sc):
    j = pl.program_id(1)

    @pl.when(j == 0)
    def _():
        acc_sc[...] = jnp.zeros_like(acc_sc)

    hid = jnp.maximum(_dot(a_ref[...], w1_ref[...]), 0.0)
    acc_sc[...] += _dot((hid * hid).astype(bf16), w2_ref[...])

    @pl.when(j == pl.num_programs(1) - 1)
    def _():
        o_ref[...] = _rms(acc_sc[...], g2_ref[...]).astype(o_ref.dtype)


def _mlp(a, w1, w2, g2, tm=1024, tf=1024):
    m = a.shape[0]
    return pl.pallas_call(
        _mlp_body,
        grid=(m // tm, D_FF // tf),
        in_specs=[pl.BlockSpec((tm, D_MODEL), lambda i, j: (i, 0)),
                  pl.BlockSpec((D_MODEL, tf), lambda i, j: (0, j)),
                  pl.BlockSpec((tf, D_MODEL), lambda i, j: (j, 0)),
                  _const_spec((1, D_MODEL))],
        out_specs=pl.BlockSpec((tm, D_MODEL), lambda i, j: (i, 0)),
        out_shape=jax.ShapeDtypeStruct((m, D_MODEL), bf16),
        scratch_shapes=[pltpu.VMEM((tm, D_MODEL), f32)],
        compiler_params=_params(("parallel", "arbitrary")),
        name="mlp",
    )(a, w1, w2, g2)


def _ple_body(h_ref, f_ref, p_ref, wg_ref, wp_ref, o_ref):
    half = h_ref.shape[0] // 2
    rows = [slice(0, half), slice(half, 2 * half)]
    staged = []
    for r in rows:
        h = h_ref[r, :] + f_ref[r, :].astype(f32)
        staged.append((h, _dot(h.astype(bf16), wg_ref[...]), _dot(p_ref[r, :].astype(bf16), wp_ref[...])))
    for r, (h, gate, emb) in zip(rows, staged):
        o_ref[r, :] = h + jax.nn.sigmoid(gate) * emb


def _ple(h, f, p2, wg, wp, tm=512):
    m = h.shape[0]
    row = pl.BlockSpec((tm, D_MODEL), lambda i: (i, 0))
    return pl.pallas_call(
        _ple_body,
        grid=(m // tm,),
        in_specs=[row, row,
                  pl.BlockSpec((tm, PLE_DIM), lambda i: (i, 0)),
                  _const_spec((D_MODEL, D_MODEL)),
                  _const_spec((PLE_DIM, D_MODEL))],
        out_specs=row,
        out_shape=jax.ShapeDtypeStruct((m, D_MODEL), f32),
        compiler_params=_params(("parallel",)),
        name="ple",
    )(h, f, p2, wg, wp)


def _block_diag_chunks(w):
    per = GATE_CHUNK // LRU_BLOCK_DIM
    w = w.reshape(LRU_BLOCKS // per, per, LRU_BLOCK_DIM, LRU_BLOCK_DIM)
    eye = jnp.eye(per, dtype=w.dtype)
    return jnp.einsum('cpij,pq->cpiqj', w, eye).reshape(LRU_BLOCKS // per, GATE_CHUNK, GATE_CHUNK)


def _layer(h, p_i, i, prm, bias):
    batch, seq, _ = h.shape
    m = batch * seq
    x2 = h.reshape(m, D_MODEL)
    row = lambda v: v.reshape(1, -1)
    per_batch = lambda t: t.reshape(batch, seq, t.shape[-1])

    w_in = prm["w_in"][i].astype(bf16)
    cols = lambda name: w_in[:, IN_COLS[name][0]:IN_COLS[name][1]]
    per = 3 * HEADS_PER_KV
    w_gate = jnp.pad(cols("gates").reshape(D_MODEL, N_KV, per), ((0, 0), (0, 0), (0, GATE_ROWS - per)))
    w_feat = jnp.concatenate([cols(n) for n in FEAT_STREAMS] + [w_gate.reshape(D_MODEL, N_KV * GATE_ROWS)],
                             axis=1).T
    u, gl, kc, vc, ks, kw, qt, vst, vwt, gtt = _in_proj(x2, row(prm["norm_mix_pre"][i]), w_in, w_feat, seq)

    y_lru = _rglru(u, gl, prm["conv_w"][i], row(prm["conv_b"][i]),
                   _block_diag_chunks(prm["lru_wa"][i]).astype(bf16), row(prm["lru_ba"][i]),
                   _block_diag_chunks(prm["lru_wx"][i]).astype(bf16), row(prm["lru_bx"][i]),
                   row(prm["lru_lambda"][i]), row(prm["gnorm_lru"][i]), batch, seq)

    lane_pad = lambda w: jnp.pad(w, ((0, 0), (0, LANES - HEAD_DIM))).astype(bf16)
    kcc, vcct = _compress(per_batch(kc), per_batch(vc),
                          prm["cmp_pe_k"][i].reshape(1, -1), prm["cmp_w1_k"][i].astype(bf16),
                          lane_pad(prm["cmp_w2_k"][i]),
                          prm["cmp_pe_v"][i].reshape(1, -1), prm["cmp_w1_v"][i].astype(bf16),
                          lane_pad(prm["cmp_w2_v"][i]).T)

    bc, bs, bw = bias
    y_nsa = _nsa(qt, kcc, vcct, per_batch(ks), vst, per_batch(kw), vwt, gtt, bc, bs, bw,
                 jnp.asarray(_importance_matrix_t(seq)).astype(bf16))

    h1, a1 = _out_proj(x2, y_lru, y_nsa.reshape(m, -1), row(prm["gnorm_nsa"][i]),
                       prm["w_out"][i].astype(bf16), row(prm["norm_mix_post"][i]), row(prm["norm_mlp_pre"][i]))
    f = _mlp(a1, prm["mlp_w1"][i].astype(bf16), prm["mlp_w2"][i].astype(bf16), row(prm["norm_mlp_post"][i]))
    h3 = _ple(h1, f, p_i.reshape(m, PLE_DIM), prm["ple_gate"][i].astype(bf16), prm["ple_proj"][i].astype(bf16))
    return h3.reshape(batch, seq, D_MODEL)


def kernel(x, p, norm_mix_pre, norm_mix_post, norm_mlp_pre, norm_mlp_post, w_in, conv_w, conv_b, lru_wa, lru_ba, lru_wx, lru_bx, lru_lambda, cmp_pe_k, cmp_w1_k, cmp_w2_k, cmp_pe_v, cmp_w1_v, cmp_w2_v, rel_bias, gnorm_lru, gnorm_nsa, w_out, mlp_w1, mlp_w2, ple_gate, ple_proj):
    prm = dict(norm_mix_pre=norm_mix_pre, norm_mix_post=norm_mix_post, norm_mlp_pre=norm_mlp_pre,
               norm_mlp_post=norm_mlp_post, w_in=w_in, conv_w=conv_w, conv_b=conv_b, lru_wa=lru_wa,
               lru_ba=lru_ba, lru_wx=lru_wx, lru_bx=lru_bx, lru_lambda=lru_lambda, cmp_pe_k=cmp_pe_k,
               cmp_w1_k=cmp_w1_k, cmp_w2_k=cmp_w2_k, cmp_pe_v=cmp_pe_v, cmp_w1_v=cmp_w1_v,
               cmp_w2_v=cmp_w2_v, gnorm_lru=gnorm_lru, gnorm_nsa=gnorm_nsa, w_out=w_out, mlp_w1=mlp_w1,
               mlp_w2=mlp_w2, ple_gate=ple_gate, ple_proj=ple_proj)
    bias = _bias_tiles(rel_bias, x.shape[1])
    h = x
    for i in range(w_in.shape[0]):
        h = _layer(h, p[i], i, prm, bias)
    return h
```

```python
import functools
import math

import numpy as np
import jax
import jax.numpy as jnp
from jax import lax
from jax.experimental import pallas as pl
from jax.experimental.pallas import tpu as pltpu

D_MODEL = 2048
PLE_DIM = 256
LRU_WIDTH = 1024
LRU_BLOCKS = 16
LRU_BLOCK_DIM = 64
CONV_WIDTH = 4
LRU_C = 8.0
HEAD_DIM = 64
N_HEADS = 16
N_KV = 4
HEADS_PER_KV = 4
CMP_LEN = 32
CMP_STRIDE = 16
CMP_HIDDEN = 256
SEL_BLOCK = 64
N_SELECT = 16
N_LOCAL_FORCED = 2
WINDOW = 512
N_BUCKETS = 32
MAX_DISTANCE = 128
D_FF = 4 * D_MODEL
NORM_EPS = 1e-6
SQRT_GUARD = 1e-30
NEG_INF = -1e30
FORCE_SCORE = 1e4
KV_W = N_KV * HEAD_DIM


def _offsets(sizes):
    out, pos = {}, 0
    for name, width in sizes:
        out[name] = (pos, pos + width)
        pos += width
    return out, pos


IN_COLS, IN_DIM = _offsets([("u", LRU_WIDTH), ("gl", LRU_WIDTH), ("q", N_HEADS * HEAD_DIM), ("kc", KV_W),
                            ("vc", KV_W), ("ks", KV_W), ("vs", KV_W), ("kw", KV_W), ("vw", KV_W),
                            ("gates", 3 * N_HEADS)])
GATE_ROWS = 16
FEAT_STREAMS = ("q", "vs", "vw")
FEAT_ROWS, FEAT_END = _offsets([(n, IN_COLS[n][1] - IN_COLS[n][0]) for n in FEAT_STREAMS]
                               + [("gates", N_KV * GATE_ROWS)])

SUBLANES = 8
GATE_CHUNK = 256
LANES = 128
assert all(lo % LANES == 0 for lo, _ in IN_COLS.values())
MASK_LO = HEAD_DIM
Q_TILE = 128
N_SEL_BIAS = 2
V_ROWS = 80
SKEW = 2
LOG2E = math.log2(math.e)

VMEM_LIMIT = 56 * 1024 * 1024

f32 = jnp.float32
bf16 = jnp.bfloat16


def _rms(x, g):
    return x * lax.rsqrt(jnp.mean(x * x, axis=-1, keepdims=True) + NORM_EPS) * g


def _dot(a, b):
    return jnp.dot(a, b, preferred_element_type=f32)


def _dot_nt(a, b):
    return lax.dot_general(a, b, (((1,), (1,)), ((), ())), preferred_element_type=f32)


def _const_spec(shape):
    nd = len(shape)
    return pl.BlockSpec(shape, lambda *_: (0,) * nd)


def _params(sem):
    return pltpu.CompilerParams(dimension_semantics=sem, vmem_limit_bytes=VMEM_LIMIT)


def _spread_groups(z, fill):
    lane = lax.broadcasted_iota(jnp.int32, (z.shape[0], LANES), 1)
    parts = []
    for g in range(N_KV):
        pair = z[:, (g // 2) * LANES:(g // 2 + 1) * LANES]
        if g % 2:
            pair = pltpu.roll(pair, HEAD_DIM, 1)
        parts.append(jnp.where(lane < HEAD_DIM, pair, fill))
    return jnp.concatenate(parts, axis=1)


def _value_rows(z):
    tm = z.shape[1]
    ones_row = jnp.where(lax.broadcasted_iota(jnp.int32, (LANES - HEAD_DIM, tm), 0) == 0, 1.0, 0.0)
    parts = []
    for g in range(N_KV):
        parts += [z[g * HEAD_DIM:(g + 1) * HEAD_DIM, :], ones_row]
    return jnp.concatenate(parts, axis=0)


def _in_proj_body(seq, x_ref, g_ref, w_ref, wt_ref, cw_ref, cb_ref, wa_ref, ba_ref, wx_ref, bx_ref, lam_ref,
                  gn_ref, yl_ref, kc_ref, vc_ref, ks_ref, kw_ref, qt_ref, vst_ref, vwt_ref, gtt_ref,
                  ubuf, a_sc, b_sc, hc_sc):
    tm = x_ref.shape[0]
    a = _rms(x_ref[...], g_ref[...]).astype(bf16)

    def tok(name):
        lo, hi = IN_COLS[name]
        return _dot(a, w_ref[:, lo:hi])

    def feat(name):
        lo, hi = FEAT_ROWS[name]
        return _dot_nt(wt_ref[lo:hi, :], a)

    lane = lax.broadcasted_iota(jnp.int32, (tm, LANES), 1)
    tile_start = (pl.program_id(0) * tm) % seq
    pos = tile_start + lax.broadcasted_iota(jnp.int32, (tm, LANES), 0)
    block_onehot = jnp.where(lane - MASK_LO == pos // SEL_BLOCK, 1.0, 0.0)

    def emit_kc(): kc_ref[...] = tok("kc")
    def emit_vc(): vc_ref[...] = tok("vc")
    def emit_ks(): ks_ref[...] = _spread_groups(tok("ks"), block_onehot).astype(bf16)
    def emit_kw(): kw_ref[...] = _spread_groups(tok("kw"), 0.0).astype(bf16)
    def emit_q(): qt_ref[...] = (feat("q") * (LOG2E * HEAD_DIM ** -0.5)).astype(bf16)
    def emit_vs(): vst_ref[...] = _value_rows(feat("vs")).astype(bf16)
    def emit_vw(): vwt_ref[...] = _value_rows(feat("vw")).astype(bf16)
    def emit_gates(): gtt_ref[...] = feat("gates")

    gl_box = []
    pending = [[lambda: gl_box.append(tok("gl"))], [emit_q], [emit_kc, emit_vc], [emit_ks, emit_kw],
               [emit_vs, emit_vw], [emit_gates]]

    def interleave():
        if pending:
            for emit in pending.pop(0):
                emit()

    _rglru_tile(tok("u"), lambda: gl_box[0], tile_start == 0, cw_ref, cb_ref, wa_ref, ba_ref, wx_ref, bx_ref,
                lam_ref, gn_ref, yl_ref, ubuf, a_sc, b_sc, hc_sc, interleave)
    while pending:
        interleave()


def _in_proj(x2, g, w, wt, lru, seq, tm=512):
    m = x2.shape[0]
    wide = N_KV * LANES
    tok_out = [(LRU_WIDTH, bf16), (KV_W, f32), (KV_W, f32), (wide, bf16), (wide, bf16)]
    feat_out = [(N_HEADS * HEAD_DIM, bf16), (wide, bf16), (wide, bf16), (N_KV * GATE_ROWS, f32)]
    body = lambda *refs: _in_proj_body(seq, *refs)
    vec = _const_spec((1, LRU_WIDTH))
    wspec = _const_spec((LRU_WIDTH // GATE_CHUNK, GATE_CHUNK, GATE_CHUNK))
    return pl.pallas_call(
        body,
        grid=(m // tm,),
        in_specs=[pl.BlockSpec((tm, D_MODEL), lambda i: (i, 0)),
                  _const_spec((1, D_MODEL)),
                  pl.BlockSpec((D_MODEL, IN_DIM), lambda i: (0, 0), pipeline_mode=pl.Buffered(1)),
                  pl.BlockSpec((FEAT_END, D_MODEL), lambda i: (0, 0), pipeline_mode=pl.Buffered(1)),
                  _const_spec((CONV_WIDTH, LRU_WIDTH)), vec, wspec, vec, wspec, vec, vec, vec],
        out_specs=([pl.BlockSpec((tm, n), lambda i: (i, 0)) for n, _ in tok_out]
                   + [pl.BlockSpec((n, tm), lambda i: (0, i)) for n, _ in feat_out]),
        out_shape=([jax.ShapeDtypeStruct((m, n), dt) for n, dt in tok_out]
                   + [jax.ShapeDtypeStruct((n, m), dt) for n, dt in feat_out]),
        scratch_shapes=[pltpu.VMEM((SUBLANES, LRU_WIDTH), f32), pltpu.VMEM((tm, LRU_WIDTH), f32),
                        pltpu.VMEM((tm, LRU_WIDTH), f32), pltpu.VMEM((SUBLANES, LRU_WIDTH), f32)],
        compiler_params=_params(("arbitrary",)),
        name="in_proj",
    )(x2, g, w, wt, *lru)


def _rglru_tile(u, gate_branch, sequence_start, cw_ref, cb_ref, wa_ref, ba_ref, wx_ref, bx_ref, lam_ref,
                gn_ref, o_ref, ubuf, a_sc, b_sc, hc_sc, interleave):
    t = u.shape[0]

    @pl.when(sequence_start)
    def _():
        ubuf[...] = jnp.zeros_like(ubuf)
        hc_sc[...] = jnp.zeros_like(hc_sc)

    u3 = u.reshape(t // SUBLANES, SUBLANES, LRU_WIDTH)
    tail = ubuf[...]
    cw = cw_ref[...]
    row8w = lax.broadcasted_iota(jnp.int32, (t // SUBLANES, SUBLANES, LRU_WIDTH), 1)
    xc3 = cb_ref[...] + u3 * cw[CONV_WIDTH - 1]
    for d in range(1, CONV_WIDTH):
        cur = pltpu.roll(u3, d, 1)
        prev = jnp.concatenate([pltpu.roll(tail, d, 0)[None], cur[:-1]], axis=0)
        xc3 = xc3 + jnp.where(row8w >= d, cur, prev) * cw[CONV_WIDTH - 1 - d]
    ubuf[...] = u[t - SUBLANES:t, :]
    xc = xc3.reshape(t, LRU_WIDTH)
    interleave()

    xb = xc.astype(bf16)
    sp = jax.nn.softplus(-lam_ref[...])
    row8 = lax.broadcasted_iota(jnp.int32, (t // SUBLANES, SUBLANES, GATE_CHUNK), 1)
    for c in range(LRU_WIDTH // GATE_CHUNK):
        sl = slice(c * GATE_CHUNK, (c + 1) * GATE_CHUNK)
        xcb = xb[:, sl]
        r = jax.nn.sigmoid(_dot(xcb, wa_ref[c]) + ba_ref[:, sl])
        ig = jax.nn.sigmoid(_dot(xcb, wx_ref[c]) + bx_ref[:, sl])
        log_a = (-LRU_C) * r * sp[:, sl]
        a = jnp.exp(log_a)
        y = 1.0 - a * a
        b = (y * lax.rsqrt(jnp.maximum(y, SQRT_GUARD))) * (ig * xc[:, sl])
        a = a.reshape(t // SUBLANES, SUBLANES, GATE_CHUNK)
        b = b.reshape(t // SUBLANES, SUBLANES, GATE_CHUNK)
        for d in (1, 2, 4):
            keep = row8 >= d
            a_prev = pltpu.roll(a, d, 1)
            b_prev = pltpu.roll(b, d, 1)
            b = jnp.where(keep, a * b_prev + b, b)
            a = jnp.where(keep, a * a_prev, a)
        a_sc[:, sl] = a.reshape(t, GATE_CHUNK)
        b_sc[:, sl] = b.reshape(t, GATE_CHUNK)
        interleave()

    h = hc_sc[0:1, :]
    for gi in range(t // SUBLANES):
        rows = slice(gi * SUBLANES, (gi + 1) * SUBLANES)
        hg = b_sc[rows, :] + a_sc[rows, :] * h
        b_sc[rows, :] = hg
        h = hg[SUBLANES - 1:SUBLANES, :]
    hc_sc[0:1, :] = h
    interleave()
    y = b_sc[...] * jax.nn.gelu(gate_branch())
    o_ref[...] = _rms(y, gn_ref[...]).astype(o_ref.dtype)


def _compress_body(kc_ref, vc_ref, pek_ref, w1k_ref, w2k_ref, pev_ref, w1v_ref, w2v_ref, ko_ref, vo_ref):
    nrow = ko_ref.shape[2]
    rows = lax.broadcasted_iota(jnp.int32, (nrow, LANES), 0)
    cols = lax.broadcasted_iota(jnp.int32, (LANES, nrow), 1)
    lane64 = lax.broadcasted_iota(jnp.int32, (nrow, LANES), 1) < HEAD_DIM
    half = CMP_STRIDE * HEAD_DIM

    def one(t_ref, pe_ref, w1_ref, w2_ref, o_ref, transposed):
        pe = jnp.broadcast_to(pe_ref[...], (8, CMP_LEN * HEAD_DIM)).astype(bf16)
        c0 = _dot(pe, w1_ref[...])[0:1, :]
        toks = [t_ref[0, pl.ds(j, nrow, stride=CMP_STRIDE), :] for j in range(CMP_STRIDE)]
        for g in range(LANES // HEAD_DIM):
            blocks = []
            for i in range(CMP_STRIDE // 2):
                even, odd = toks[2 * i], toks[2 * i + 1]
                if g == 0:
                    blocks.append(jnp.where(lane64, even, pltpu.roll(odd, HEAD_DIM, 1)))
                else:
                    blocks.append(jnp.where(lane64, pltpu.roll(even, HEAD_DIM, 1), odd))
            chunk = jnp.concatenate(blocks, axis=1).astype(bf16)
            lo = _dot(chunk, w1_ref[0:half, :])
            hi = _dot(chunk, w1_ref[half:2 * half, :])
            hid = jax.nn.gelu(lo + pltpu.roll(hi, nrow - 1, 0) + c0)
            if transposed:
                out = _dot_nt(w2_ref[...], hid.astype(bf16))
                o_ref[0, g] = jnp.where(cols < nrow - 1, out, 0.0).astype(o_ref.dtype)
            else:
                out = _dot(hid.astype(bf16), w2_ref[...])
                o_ref[0, g] = jnp.where(rows < nrow - 1, out, 0.0).astype(o_ref.dtype)

    one(kc_ref, pek_ref, w1k_ref, w2k_ref, ko_ref, False)
    one(vc_ref, pev_ref, w1v_ref, w2v_ref, vo_ref, True)


def _compress(kc, vc, pek, w1k, w2k, pev, w1v, w2v):
    batch, seq, width = kc.shape
    nrow = seq // CMP_STRIDE
    pair = LANES // HEAD_DIM
    tspec = pl.BlockSpec((1, seq, LANES), lambda b, h: (b, 0, h))
    ospec = pl.BlockSpec((1, pair, nrow, LANES), lambda b, h: (b, h, 0, 0))
    wts = [_const_spec((1, CMP_LEN * HEAD_DIM)), _const_spec((CMP_LEN * HEAD_DIM, CMP_HIDDEN))]
    tspec_v = pl.BlockSpec((1, pair, LANES, nrow), lambda b, h: (b, h, 0, 0))
    return pl.pallas_call(
        _compress_body,
        grid=(batch, width // LANES),
        in_specs=([tspec, tspec] + wts + [_const_spec((CMP_HIDDEN, LANES))]
                  + wts + [_const_spec((LANES, CMP_HIDDEN))]),
        out_specs=[ospec, tspec_v],
        out_shape=[jax.ShapeDtypeStruct((batch, N_KV, nrow, LANES), bf16),
                   jax.ShapeDtypeStruct((batch, N_KV, LANES, nrow), bf16)],
        compiler_params=_params(("parallel", "parallel")),
        name="compress",
    )(kc, vc, pek, w1k, w2k, pev, w1v, w2v)


def _bucket_thresholds():
    n = np.arange(0, 4096)
    max_exact = N_BUCKETS // 2
    nf = np.maximum(n, 1).astype(np.float32)
    large = max_exact + (np.log(nf / np.float32(max_exact)) / np.float32(math.log(MAX_DISTANCE / max_exact))
                         * np.float32(N_BUCKETS - max_exact)).astype(np.int32)
    large = np.minimum(large, N_BUCKETS - 1)
    bucket = np.where(n < max_exact, n, large)
    assert np.all(np.diff(bucket) >= 0) and bucket[0] == 0 and bucket[-1] == N_BUCKETS - 1
    return [int(np.argmax(bucket >= k)) for k in range(N_BUCKETS)]


_BUCKET_THR = _bucket_thresholds()
assert _BUCKET_THR[-1] <= (N_SEL_BIAS - 1) * Q_TILE + 1


def _bias_of_dist(dist, ok, tab_ref, head):
    last = tab_ref[N_BUCKETS - 1, head]
    val = jnp.full(dist.shape, (tab_ref[0, head] - last) * LOG2E, f32)
    for k in range(1, N_BUCKETS - 1):
        val = jnp.where(dist >= _BUCKET_THR[k], (tab_ref[k, head] - last) * LOG2E, val)
    val = jnp.where(dist >= _BUCKET_THR[N_BUCKETS - 1], 0.0, val)
    return jnp.where(ok, val, NEG_INF)


def _bias_body(tab_ref, bc_ref, bs_ref, bw_ref):
    g = pl.program_id(0)
    nq, ncmp = bc_ref.shape[1], bc_ref.shape[2]
    per_tile = Q_TILE // CMP_STRIDE
    back = -(-(_BUCKET_THR[-1] + CMP_LEN - 1) // CMP_STRIDE)
    for r in range(HEADS_PER_KV):
        head = g * HEADS_PER_KV + r
        cols = slice(r * Q_TILE, (r + 1) * Q_TILE)
        for t in range(nq):
            lo = max((t * per_tile - back) // SUBLANES * SUBLANES, 0)
            hi = min(-(-((t + 1) * per_tile) // SUBLANES) * SUBLANES, ncmp)
            n = lo + lax.broadcasted_iota(jnp.int32, (hi - lo, Q_TILE), 0)
            i = lax.broadcasted_iota(jnp.int32, (hi - lo, Q_TILE), 1)
            dist = t * Q_TILE + i - (n * CMP_STRIDE + CMP_LEN - 1)
            if lo > 0:
                bc_ref[0, t, 0:lo, cols] = jnp.zeros((lo, Q_TILE), f32)
            bc_ref[0, t, lo:hi, cols] = _bias_of_dist(dist, (dist >= 0) & (n < ncmp - 1), tab_ref, head)
            if hi < ncmp:
                bc_ref[0, t, hi:ncmp, cols] = jnp.full((ncmp - hi, Q_TILE), NEG_INF, f32)

        j = lax.broadcasted_iota(jnp.int32, (Q_TILE, Q_TILE), 0)
        i = lax.broadcasted_iota(jnp.int32, (Q_TILE, Q_TILE), 1)
        for d in range(N_SEL_BIAS):
            dist = d * Q_TILE + i - j
            bs_ref[0, d, :, cols] = _bias_of_dist(dist, dist >= 0, tab_ref, head)
        dist = WINDOW + i - j
        bw_ref[0, 0, :, cols] = _bias_of_dist(dist, dist < WINDOW, tab_ref, head)


def _bias_tiles(rel_bias, seq):
    nq = seq // Q_TILE
    ncmp = seq // CMP_STRIDE
    rows = HEADS_PER_KV * Q_TILE
    return pl.pallas_call(
        _bias_body,
        grid=(N_KV,),
        in_specs=[pl.BlockSpec(memory_space=pltpu.SMEM)],
        out_specs=[pl.BlockSpec((1, nq, ncmp, rows), lambda g: (g, 0, 0, 0)),
                   pl.BlockSpec((1, N_SEL_BIAS, Q_TILE, rows), lambda g: (g, 0, 0, 0)),
                   pl.BlockSpec((1, 1, Q_TILE, rows), lambda g: (g, 0, 0, 0))],
        out_shape=[jax.ShapeDtypeStruct((N_KV, nq, ncmp, rows), f32),
                   jax.ShapeDtypeStruct((N_KV, N_SEL_BIAS, Q_TILE, rows), f32),
                   jax.ShapeDtypeStruct((N_KV, 1, Q_TILE, rows), f32)],
        compiler_params=_params(("parallel",)),
        name="bias_tiles",
    )(rel_bias)


def _importance_matrix_t(seq):
    n_cmp = (seq - CMP_LEN) // CMP_STRIDE + 1
    n_sel = seq // SEL_BLOCK
    ratio_sel = SEL_BLOCK // CMP_STRIDE
    ratio_cmp = CMP_LEN // CMP_STRIDE
    jj = np.arange(n_sel)[:, None, None]
    ci = ratio_sel * jj + np.arange(ratio_sel)[None, :, None] - np.arange(ratio_cmp)[None, None, :]
    jb = np.broadcast_to(jj, ci.shape)
    ok = (ci >= 0) & (ci < n_cmp)
    m = np.zeros((n_sel, seq // CMP_STRIDE), np.float32)
    np.add.at(m, (jb[ok], ci[ok]), 1.0)
    return m


def _scores(qt, k_ref, t, first_tile, biases):
    s = _dot(k_ref[0, first_tile * Q_TILE:(t + 1) * Q_TILE, :], qt)
    pieces = []
    for kt in range(first_tile, t + 1):
        piece = s[(kt - first_tile) * Q_TILE:(kt - first_tile + 1) * Q_TILE, :]
        if t - kt in biases:
            piece = piece + biases[t - kt]
        pieces.append(piece)
    fold = lambda pc: jnp.max(pc.reshape(Q_TILE // SUBLANES, SUBLANES, pc.shape[1]), axis=0)
    return pieces, jnp.max(functools.reduce(jnp.maximum, [fold(pc) for pc in pieces]), axis=0, keepdims=True)


def _weighted_values(pieces, m, vt_ref, t, first_tile):
    p = jnp.concatenate([jnp.exp2(pc - m).astype(bf16) for pc in pieces], axis=0)
    acc = _dot(vt_ref[0:V_ROWS, first_tile * Q_TILE:(t + 1) * Q_TILE], p)
    return acc[0:HEAD_DIM, :], 1.0 / acc[MASK_LO:MASK_LO + 1, :]


def _nsa_body(qt_ref, kc_ref, vct_ref, ks_ref, vst_ref, kw_ref, vwt_ref, gtt_ref, bc_ref, bs_ref, bw_ref,
              mt_ref, o_ref):
    n_sel = mt_ref.shape[0]
    n_tiles = qt_ref.shape[1] // Q_TILE
    cols = HEADS_PER_KV * Q_TILE
    head_cols = [slice(r * Q_TILE, (r + 1) * Q_TILE) for r in range(HEADS_PER_KV)]
    kc = kc_ref[0, 0]
    vct = vct_ref[0, 0]
    mt = mt_ref[...]
    jj = lax.broadcasted_iota(jnp.int32, (n_sel, Q_TILE), 0)
    qcol = lax.broadcasted_iota(jnp.int32, (n_sel, Q_TILE), 1)
    zeros_tail = jnp.zeros((LANES - HEAD_DIM, Q_TILE), bf16)
    zeros_rest = jnp.zeros((LANES - HEAD_DIM - n_sel, Q_TILE), bf16)

    def compressed_scores(t):
        tok = slice(t * Q_TILE, (t + 1) * Q_TILE)
        heads = [qt_ref[r * HEAD_DIM:(r + 1) * HEAD_DIM, tok] for r in range(HEADS_PER_KV)]
        q_plain = jnp.concatenate([jnp.concatenate([hd, zeros_tail], axis=0) for hd in heads], axis=1)
        return heads, q_plain, _dot(kc, q_plain) + bc_ref[0, t]

    def select_stage(t, heads, q_plain, s):
        p = jnp.exp2(s - jnp.max(s, axis=0, keepdims=True))
        norm = 1.0 / jnp.sum(p, axis=0, keepdims=True)
        if (t + 1) * Q_TILE > CMP_LEN - 1 >= t * Q_TILE:
            pos = t * Q_TILE + (lax.broadcasted_iota(jnp.int32, (1, cols), 1) & (Q_TILE - 1))
            norm = jnp.where(pos >= CMP_LEN - 1, norm, 0.0)
        p = p * norm
        o_c = _dot(vct[0:HEAD_DIM, :], p.astype(bf16))

        n_causal = ((t + 1) * Q_TILE - 1) // SEL_BLOCK + 1
        if n_causal <= N_SELECT:
            return q_plain, q_plain, o_c

        psum = p[:, head_cols[0]] + p[:, head_cols[1]] + p[:, head_cols[2]] + p[:, head_cols[3]]
        p_hi = psum.astype(bf16)
        p_lo = (psum - p_hi.astype(f32)).astype(bf16)
        imp = _dot(mt, p_hi) + _dot(mt, p_lo)
        dblk = (t * Q_TILE + qcol) // SEL_BLOCK - jj
        forced = (jj == 0) | ((dblk >= 0) & (dblk < N_LOCAL_FORCED))
        imp = jnp.where(forced, FORCE_SCORE, jnp.where(dblk >= 0, imp, -FORCE_SCORE))
        rank = jnp.zeros((n_sel, Q_TILE), f32)
        for i in range(n_causal):
            row = imp[i:i + 1, :]
            ahead = (row > imp) | ((row == imp) & (jj > i))
            rank = rank + jnp.where(ahead, 1.0, 0.0)
        mask_rows = jnp.where(rank < N_SELECT, 0.0, NEG_INF).astype(bf16)
        q_masked = jnp.concatenate(
            [jnp.concatenate([hd, mask_rows, zeros_rest], axis=0) for hd in heads], axis=1)
        return q_plain, q_masked, o_c

    first_win = lambda t: max(t - WINDOW // Q_TILE, 0)

    def sel_scores(t, q_masked):
        return _scores(q_masked, ks_ref, t, 0, {d: bs_ref[0, d] for d in range(N_SEL_BIAS)})

    def win_scores(t, q_plain):
        band = {d: bs_ref[0, d] for d in range(N_SEL_BIAS)}
        band[WINDOW // Q_TILE] = bw_ref[0, 0]
        return _scores(q_plain, kw_ref, t, first_win(t), band)

    def finish(t, o_c, sel_out, win_out):
        tok = slice(t * Q_TILE, (t + 1) * Q_TILE)
        acc_s, inv_s = sel_out
        acc_w, inv_w = win_out
        gt = jax.nn.sigmoid(gtt_ref[:, tok])
        gate = lambda br: jnp.concatenate([gt[3 * r + br:3 * r + br + 1, :] for r in range(HEADS_PER_KV)], axis=1)
        out_t = gate(0) * o_c + (gate(1) * inv_s) * acc_s + (gate(2) * inv_w) * acc_w
        pairs = [jnp.concatenate([out_t[:, head_cols[2 * h]], out_t[:, head_cols[2 * h + 1]]], axis=0).T
                 for h in range(HEADS_PER_KV // 2)]
        o_ref[0, tok, :] = jnp.concatenate(pairs, axis=-1)

    compressed, queries, o_cs, s_sel, s_win = {}, {}, {}, {}, {}

    def emit_compressed(t):
        if t < n_tiles:
            compressed[t] = compressed_scores(t)

    def emit_select(t):
        if t < n_tiles:
            q_plain, q_masked, o_cs[t] = select_stage(t, *compressed.pop(t))
            queries[t] = (q_plain, q_masked)

    def emit_scores(t):
        if t < n_tiles:
            q_plain, q_masked = queries.pop(t)
            s_sel[t] = sel_scores(t, q_masked)
            s_win[t] = win_scores(t, q_plain)

    for t in range(SKEW + 2):
        emit_compressed(t)
    for t in range(SKEW + 1):
        emit_select(t)
    for t in range(SKEW):
        emit_scores(t)
    for t in range(n_tiles):
        emit_compressed(t + SKEW + 2)
        emit_scores(t + SKEW)
        emit_select(t + SKEW + 1)
        sel_out = _weighted_values(*s_sel.pop(t), vst_ref, t, 0)
        win_out = _weighted_values(*s_win.pop(t), vwt_ref, t, first_win(t))
        finish(t, o_cs.pop(t), sel_out, win_out)


def _nsa(qt, kc, vct, ks, vst, kw, vwt, gtt, bc, bs, bw, mt):
    batch, seq, _ = ks.shape
    nq = seq // Q_TILE
    cols = HEADS_PER_KV * Q_TILE
    ncmp = kc.shape[2]
    width = HEADS_PER_KV * HEAD_DIM
    k_spec = pl.BlockSpec((1, seq, LANES), lambda gi, b: (b, 0, gi))
    vt_spec = pl.BlockSpec((LANES, seq), lambda gi, b: (gi, b))
    return pl.pallas_call(
        _nsa_body,
        grid=(N_KV, batch),
        in_specs=[pl.BlockSpec((width, seq), lambda gi, b: (gi, b)),
                  pl.BlockSpec((1, 1, ncmp, LANES), lambda gi, b: (b, gi, 0, 0)),
                  pl.BlockSpec((1, 1, LANES, ncmp), lambda gi, b: (b, gi, 0, 0)),
                  k_spec, vt_spec, k_spec, vt_spec,
                  pl.BlockSpec((GATE_ROWS, seq), lambda gi, b: (gi, b)),
                  pl.BlockSpec((1, nq, ncmp, cols), lambda gi, b: (gi, 0, 0, 0)),
                  pl.BlockSpec((1, N_SEL_BIAS, Q_TILE, cols), lambda gi, b: (gi, 0, 0, 0)),
                  pl.BlockSpec((1, 1, Q_TILE, cols), lambda gi, b: (gi, 0, 0, 0)),
                  _const_spec(mt.shape)],
        out_specs=pl.BlockSpec((1, seq, width), lambda gi, b: (b, 0, gi)),
        out_shape=jax.ShapeDtypeStruct((batch, seq, N_KV * width), f32),
        compiler_params=_params(("parallel", "arbitrary")),
        name="nsa",
    )(qt, kc, vct, ks, vst, kw, vwt, gtt, bc, bs, bw, mt)


def _out_proj_body(x_ref, yl_ref, yn_ref, gn_ref, w_ref, gp_ref, gm_ref, o_ref, a_ref):
    half = x_ref.shape[0] // 2
    rows = [slice(0, half), slice(half, 2 * half)]
    ys = []
    for r in rows:
        yn = _rms(yn_ref[r, :], gn_ref[...]).astype(bf16)
        ys.append(_dot(yl_ref[r, :], w_ref[0:LRU_WIDTH, :]) + _dot(yn, w_ref[LRU_WIDTH:, :]))
    for r, y in zip(rows, ys):
        h = x_ref[r, :] + _rms(y, gp_ref[...])
        o_ref[r, :] = h
        a_ref[r, :] = _rms(h, gm_ref[...]).astype(bf16)


def _out_proj(x2, yl, yn, gn, w, gp, gm, tm=512):
    m = x2.shape[0]
    row = pl.BlockSpec((tm, D_MODEL), lambda i: (i, 0))
    return pl.pallas_call(
        _out_proj_body,
        grid=(m // tm,),
        in_specs=[row,
                  pl.BlockSpec((tm, LRU_WIDTH), lambda i: (i, 0)),
                  pl.BlockSpec((tm, D_MODEL - LRU_WIDTH), lambda i: (i, 0)),
                  _const_spec((1, D_MODEL - LRU_WIDTH)),
                  _const_spec((D_MODEL, D_MODEL)),
                  _const_spec((1, D_MODEL)), _const_spec((1, D_MODEL))],
        out_specs=[row, row],
        out_shape=[jax.ShapeDtypeStruct((m, D_MODEL), f32), jax.ShapeDtypeStruct((m, D_MODEL), bf16)],
        compiler_params=_params(("parallel",)),
        name="out_proj",
    )(x2, yl, yn, gn, w, gp, gm)


def _mlp_body(a_ref, w1_ref, w2_ref, g2_ref, o_ref, acc_sc):
    j = pl.program_id(1)

    @pl.when(j == 0)
    def _():
        acc_sc[...] = jnp.zeros_like(acc_sc)

    hid = jnp.maximum(_dot(a_ref[...], w1_ref[...]), 0.0)
    acc_sc[...] += _dot((hid * hid).astype(bf16), w2_ref[...])

    @pl.when(j == pl.num_programs(1) - 1)
    def _():
        o_ref[...] = _rms(acc_sc[...], g2_ref[...])


def _mlp(a, w1, w2, g2, tm=1024, tf=1024):
    m = a.shape[0]
    return pl.pallas_call(
        _mlp_body,
        grid=(m // tm, D_FF // tf),
        in_specs=[pl.BlockSpec((tm, D_MODEL), lambda i, j: (i, 0)),
                  pl.BlockSpec((D_MODEL, tf), lambda i, j: (0, j)),
                  pl.BlockSpec((tf, D_MODEL), lambda i, j: (j, 0)),
                  _const_spec((1, D_MODEL))],
        out_specs=pl.BlockSpec((tm, D_MODEL), lambda i, j: (i, 0)),
        out_shape=jax.ShapeDtypeStruct((m, D_MODEL), f32),
        scratch_shapes=[pltpu.VMEM((tm, D_MODEL), f32)],
        compiler_params=_params(("parallel", "arbitrary")),
        name="mlp",
    )(a, w1, w2, g2)


def _ple_body(h_ref, f_ref, p_ref, wg_ref, wp_ref, o_ref):
    half = h_ref.shape[0] // 2
    rows = [slice(0, half), slice(half, 2 * half)]
    staged = []
    for r in rows:
        h = h_ref[r, :] + f_ref[r, :]
        staged.append((h, _dot(h.astype(bf16), wg_ref[...]), _dot(p_ref[r, :].astype(bf16), wp_ref[...])))
    for r, (h, gate, emb) in zip(rows, staged):
        o_ref[r, :] = h + jax.nn.sigmoid(gate) * emb


def _ple(h, f, p2, wg, wp, tm=512):
    m = h.shape[0]
    row = pl.BlockSpec((tm, D_MODEL), lambda i: (i, 0))
    return pl.pallas_call(
        _ple_body,
        grid=(m // tm,),
        in_specs=[row, row,
                  pl.BlockSpec((tm, PLE_DIM), lambda i: (i, 0)),
                  _const_spec((D_MODEL, D_MODEL)),
                  _const_spec((PLE_DIM, D_MODEL))],
        out_specs=row,
        out_shape=jax.ShapeDtypeStruct((m, D_MODEL), f32),
        compiler_params=_params(("parallel",)),
        name="ple",
    )(h, f, p2, wg, wp)


def _block_diag_chunks(w):
    per = GATE_CHUNK // LRU_BLOCK_DIM
    w = w.reshape(LRU_BLOCKS // per, per, LRU_BLOCK_DIM, LRU_BLOCK_DIM)
    eye = jnp.eye(per, dtype=w.dtype)
    return jnp.einsum('cpij,pq->cpiqj', w, eye).reshape(LRU_BLOCKS // per, GATE_CHUNK, GATE_CHUNK)


def _layer(h, p_i, i, prm, bias):
    batch, seq, _ = h.shape
    m = batch * seq
    x2 = h.reshape(m, D_MODEL)
    row = lambda v: v.reshape(1, -1)
    per_batch = lambda t: t.reshape(batch, seq, t.shape[-1])

    w_in = prm["w_in"][i].astype(bf16)
    cols = lambda name: w_in[:, IN_COLS[name][0]:IN_COLS[name][1]]
    per = 3 * HEADS_PER_KV
    w_gate = jnp.pad(cols("gates").reshape(D_MODEL, N_KV, per), ((0, 0), (0, 0), (0, GATE_ROWS - per)))
    w_feat = jnp.concatenate([cols(n) for n in FEAT_STREAMS] + [w_gate.reshape(D_MODEL, N_KV * GATE_ROWS)],
                             axis=1).T
    lru = (prm["conv_w"][i], row(prm["conv_b"][i]),
           _block_diag_chunks(prm["lru_wa"][i]).astype(bf16), row(prm["lru_ba"][i]),
           _block_diag_chunks(prm["lru_wx"][i]).astype(bf16), row(prm["lru_bx"][i]),
           row(prm["lru_lambda"][i]), row(prm["gnorm_lru"][i]))
    y_lru, kc, vc, ks, kw, qt, vst, vwt, gtt = _in_proj(x2, row(prm["norm_mix_pre"][i]), w_in, w_feat, lru, seq)

    lane_pad = lambda w: jnp.pad(w, ((0, 0), (0, LANES - HEAD_DIM))).astype(bf16)
    kcc, vcct = _compress(per_batch(kc), per_batch(vc),
                          prm["cmp_pe_k"][i].reshape(1, -1), prm["cmp_w1_k"][i].astype(bf16),
                          lane_pad(prm["cmp_w2_k"][i]),
                          prm["cmp_pe_v"][i].reshape(1, -1), prm["cmp_w1_v"][i].astype(bf16),
                          lane_pad(prm["cmp_w2_v"][i]).T)

    bc, bs, bw = bias
    y_nsa = _nsa(qt, kcc, vcct, per_batch(ks), vst, per_batch(kw), vwt, gtt, bc, bs, bw,
                 jnp.asarray(_importance_matrix_t(seq)).astype(bf16))

    h1, a1 = _out_proj(x2, y_lru, y_nsa.reshape(m, -1), row(prm["gnorm_nsa"][i]),
                       prm["w_out"][i].astype(bf16), row(prm["norm_mix_post"][i]), row(prm["norm_mlp_pre"][i]))
    f = _mlp(a1, prm["mlp_w1"][i].astype(bf16), prm["mlp_w2"][i].astype(bf16), row(prm["norm_mlp_post"][i]))
    h3 = _ple(h1, f, p_i.reshape(m, PLE_DIM), prm["ple_gate"][i].astype(bf16), prm["ple_proj"][i].astype(bf16))
    return h3.reshape(batch, seq, D_MODEL)


def kernel(x, p, norm_mix_pre, norm_mix_post, norm_mlp_pre, norm_mlp_post, w_in, conv_w, conv_b, lru_wa, lru_ba, lru_wx, lru_bx, lru_lambda, cmp_pe_k, cmp_w1_k, cmp_w2_k, cmp_pe_v, cmp_w1_v, cmp_w2_v, rel_bias, gnorm_lru, gnorm_nsa, w_out, mlp_w1, mlp_w2, ple_gate, ple_proj):
    prm = dict(norm_mix_pre=norm_mix_pre, norm_mix_post=norm_mix_post, norm_mlp_pre=norm_mlp_pre,
               norm_mlp_post=norm_mlp_post, w_in=w_in, conv_w=conv_w, conv_b=conv_b, lru_wa=lru_wa,
               lru_ba=lru_ba, lru_wx=lru_wx, lru_bx=lru_bx, lru_lambda=lru_lambda, cmp_pe_k=cmp_pe_k,
               cmp_w1_k=cmp_w1_k, cmp_w2_k=cmp_w2_k, cmp_pe_v=cmp_pe_v, cmp_w1_v=cmp_w1_v,
               cmp_w2_v=cmp_w2_v, gnorm_lru=gnorm_lru, gnorm_nsa=gnorm_nsa, w_out=w_out, mlp_w1=mlp_w1,
               mlp_w2=mlp_w2, ple_gate=ple_gate, ple_proj=ple_proj)
    bias = _bias_tiles(rel_bias, x.shape[1])
    h = x
    for i in range(w_in.shape[0]):
        h = _layer(h, p[i], i, prm, bias)
    return h
```
